```python
import math
import jax, jax.numpy as jnp
from jax import lax
import numpy as np

D_MODEL = 1024
BATCH = 16
SEQ = 4096
DEPTH = 1

CTX_LEN = 256
GRID_W = 64
EPS = 1e-6
F32 = jnp.float32

HY_WIDTH = D_MODEL
HY_ORDER = 2
HY_EMB = 33
HY_BANDS = (HY_EMB - 1) // 2
HY_FFN = 64
HY_SHORT = 3
HY_FAST_DECAY = 0.3
HY_SLOW_DECAY = 1.5
HY_TARGET = 1e-2
HY_COLS = (HY_ORDER + 1) * HY_WIDTH

GLA_HEADS = 4
GLA_DK = D_MODEL // 2 // GLA_HEADS
GLA_DV = D_MODEL // GLA_HEADS
GLA_RANK = 16
GLA_TAU = 16.0
GLA_CHUNK = 64
QK_W = GLA_HEADS * GLA_DK
V_W = GLA_HEADS * GLA_DV
A_W = 2 * GLA_RANK
STATE_COLS = QK_W + V_W + A_W
GATE_COLS = 2 * D_MODEL
N_IN = STATE_COLS + QK_W + V_W + HY_COLS + GATE_COLS

N_EXPERTS = 64
N_GROUPS = 8
TOPK_GROUPS = 4
TOP_K = 8
D_EXPERT = D_MODEL // 4
D_SHARED = D_EXPERT
ROUTED_SCALE = 2.5
MOE_BLOCK = 128

kernel_name = 'hybrid_hyena_gla_moe_dit_block'


def rms_norm(x, w):
    xf = x.astype(F32)
    y = xf * lax.rsqrt(jnp.mean(xf * xf, axis=-1, keepdims=True) + EPS)
    return (y * w.astype(F32)).astype(x.dtype)


def modulate(xn, shift, scale):
    return xn * (1.0 + scale) + shift


def short_conv(u, w, b, n_rows):
    bn, L, ch = u.shape
    row_len = L // n_rows
    pad = HY_SHORT // 2
    p = jnp.pad(u.reshape(bn, n_rows, row_len, ch), ((0, 0), (0, 0), (pad, pad), (0, 0)))
    y = sum(p[:, :, j:j + row_len] * w[j] for j in range(HY_SHORT))
    return (y + b).reshape(bn, L, ch)


def hyena_filter_spectra(L, lp):
    t = jnp.linspace(0.0, 1.0, L, dtype=F32)[:, None]
    w = 2.0 * math.pi * jnp.arange(L, dtype=F32)[:, None] / L
    f = jnp.linspace(1e-4, HY_BANDS - 1, HY_BANDS, dtype=F32)[None, :]
    z = jnp.concatenate([t, jnp.cos(f * w), -jnp.sin(f * w)], axis=-1)
    freq = lp['hy_freq'].astype(F32)
    h = jnp.sin(freq[0] * (z @ lp['hy_w1'].astype(F32) + lp['hy_b1'].astype(F32)))
    h = jnp.sin(freq[1] * (h @ lp['hy_w2'].astype(F32) + lp['hy_b2'].astype(F32)))
    h = (h @ lp['hy_w3'].astype(F32)).reshape(L, 2 * HY_ORDER, HY_WIDTH)
    max_decay = math.log(HY_TARGET) / HY_FAST_DECAY
    min_decay = math.log(HY_TARGET) / HY_SLOW_DECAY
    deltas = jnp.abs(jnp.linspace(min_decay, max_decay, HY_WIDTH, dtype=F32))
    h = h * jnp.exp(-t * deltas)[:, None, :]
    hf, hb = h[:, 0::2], h[:, 1::2]
    kern = jnp.concatenate([hf, jnp.zeros_like(hf[:1]), hb[:0:-1]], axis=0)
    kern = kern * lax.rsqrt(jnp.sum(kern * kern, axis=0, keepdims=True))
    return jnp.fft.rfft(kern, axis=0)


def fft_long_conv(u, spec, bias):
    L = u.shape[1]
    uf = u.astype(F32)
    y = jnp.fft.irfft(jnp.fft.rfft(uf, n=2 * L, axis=1) * spec[None], n=2 * L, axis=1)[:, :L]
    return (y + uf * bias.astype(F32)).astype(u.dtype)


def hyena_mixer(u_hy, n_rows, lp):
    L = u_hy.shape[1]
    parts = jnp.split(short_conv(u_hy, lp['hy_conv_w'], lp['hy_conv_b'], n_rows), HY_ORDER + 1, axis=-1)
    spec = hyena_filter_spectra(L, lp)
    z = parts[0]
    for o in range(HY_ORDER):
        z = parts[o + 1] * fft_long_conv(z, spec[:, o], lp['hy_bias'][o])
    return z


def heads(a, d):
    return a.reshape(a.shape[0], a.shape[1], -1, d).transpose(0, 2, 1, 3)


def gla_inputs(u_k, u_v, u_a, lp):
    bn, L, _ = u_k.shape
    gl = jnp.einsum('blnr,nrk->nblk', u_a.reshape(bn, L, 2, GLA_RANK), lp['gla_a_w2']) + lp['gla_a_b'][:, None, None, :]
    gl = jax.nn.log_sigmoid(gl.astype(F32)) / GLA_TAU
    return heads(u_k, GLA_DK), heads(u_v, GLA_DV), heads(gl[0], GLA_DK), heads(gl[1], GLA_DK)


def gla_chunked(k, v, g, s0, exclusive, q=None):
    bn, H, L, _ = k.shape
    nc = L // GLA_CHUNK

    def chunks(a):
        return jnp.moveaxis(a.astype(F32).reshape(bn, H, nc, GLA_CHUNK, a.shape[-1]), 2, 0)

    idx = jnp.arange(GLA_CHUNK)
    mask = (idx[:, None] > idx[None, :]) if exclusive else (idx[:, None] >= idx[None, :])
    xs = (chunks(k), chunks(v), chunks(g)) if q is None else (chunks(k), chunks(v), chunks(g), chunks(q))

    def step(s, inp):
        kc, vc, gc = inp[0], inp[1], inp[2]
        b = jnp.cumsum(gc, axis=-2)
        b_last = b[..., -1:, :]
        s_new = s * jnp.exp(b_last[..., 0, :])[..., :, None] + jnp.einsum('bhck,bhcv->bhkv', kc * jnp.exp(b_last - b), vc)
        if q is None:
            return s_new, None
        qc = inp[3]
        bq = b - gc if exclusive else b
        o_inter = jnp.einsum('bhck,bhkv->bhcv', qc * jnp.exp(bq), s)
        decay = jnp.exp(jnp.where(mask[:, :, None], bq[..., :, None, :] - b[..., None, :, :], -jnp.inf))
        attn = jnp.einsum('bhik,bhjk,bhijk->bhij', qc, kc, decay)
        return s_new, o_inter + jnp.einsum('bhij,bhjv->bhiv', attn, vc)

    s_fin, o = lax.scan(step, s0, xs)
    if q is not None:
        o = jnp.moveaxis(o, 0, 2).reshape(bn, H, L, v.shape[-1])
    return o, s_fin


def gla_bidirectional(k, v, gf, gb, s0f, s0b, q=None):
    flip = lambda a: jnp.flip(a, axis=2)
    o_f, s_f = gla_chunked(k, v, gf, s0f, False, q)
    o_b, s_b = gla_chunked(flip(k), flip(v), flip(gb), s0b, True, None if q is None else flip(q))
    o = None if q is None else o_f + flip(o_b)
    return o, s_f, s_b


def token_mixer(un, n_rows, s0f, s0b, lp):
    bn, L, _ = un.shape
    u = un @ lp['w_in']
    c0 = STATE_COLS + QK_W + V_W
    u_k, u_v, u_a, u_q, u_g, u_hy, u_gate = jnp.split(
        u, [QK_W, QK_W + V_W, STATE_COLS, STATE_COLS + QK_W, c0, c0 + HY_COLS], axis=-1)
    y_hy = hyena_mixer(u_hy, n_rows, lp)
    k, v, gf, gb = gla_inputs(u_k, u_v, u_a, lp)
    q = heads(u_q, GLA_DK) * (GLA_DK ** -0.5)
    o, s_f, s_b = gla_bidirectional(k, v, gf, gb, s0f, s0b, q)
    o = o.transpose(0, 2, 1, 3).astype(un.dtype)
    o = rms_norm(o, lp['gla_norm_w'].reshape(GLA_HEADS, GLA_DV)) * jax.nn.silu(u_g).reshape(bn, L, GLA_HEADS, GLA_DV)
    y_gla = o.reshape(bn, L, V_W)
    gate_hy, gate_gla = jnp.split(jax.nn.sigmoid(u_gate), 2, axis=-1)
    merged = gate_hy * (y_hy @ lp['proj_hy']) + gate_gla * (y_gla @ lp['proj_gla'])
    return merged @ lp['w_out'], s_f, s_b


def context_states(cn, lp):
    u = cn @ lp['w_in'][:, :STATE_COLS]
    u_k, u_v, u_a = jnp.split(u, [QK_W, QK_W + V_W], axis=-1)
    k, v, gf, gb = gla_inputs(u_k, u_v, u_a, lp)
    s0 = jnp.zeros((cn.shape[0], GLA_HEADS, GLA_DK, GLA_DV), F32)
    _, s_f, s_b = gla_bidirectional(k, v, gf, gb, s0, s0)
    return s_f, s_b


def moe_ffn(h, lp):
    T = h.shape[0]
    scores = jax.nn.sigmoid((h @ lp['router_w']).astype(F32))
    sel = scores + lp['router_bias'].astype(F32)
    grp_score = lax.top_k(sel.reshape(T, N_GROUPS, N_EXPERTS // N_GROUPS), 2)[0].sum(-1)
    _, g_idx = lax.top_k(grp_score, TOPK_GROUPS)
    g_mask = jax.nn.one_hot(g_idx, N_GROUPS, dtype=F32).sum(1)
    e_mask = jnp.repeat(g_mask, N_EXPERTS // N_GROUPS, axis=-1)
    _, e_idx = lax.top_k(jnp.where(e_mask > 0, sel, -jnp.inf), TOP_K)
    w = jnp.take_along_axis(scores, e_idx, axis=-1)
    w = w / jnp.sum(w, axis=-1, keepdims=True) * ROUTED_SCALE
    comb = jnp.einsum('tk,tke->te', w, jax.nn.one_hot(e_idx, N_EXPERTS, dtype=F32)).astype(h.dtype)

    def block(args):
        hb, cb = args
        a = jax.nn.silu(jnp.einsum('td,edf->tef', hb, lp['exp_w1'])) * jnp.einsum('td,edf->tef', hb, lp['exp_w3'])
        return jnp.einsum('tef,efd->td', a * cb[:, :, None], lp['exp_w2'])

    nb = T // MOE_BLOCK
    routed = lax.map(block, (h.reshape(nb, MOE_BLOCK, -1), comb.reshape(nb, MOE_BLOCK, -1))).reshape(T, -1)
    shared = (jax.nn.silu(h @ lp['sh_w1']) * (h @ lp['sh_w3'])) @ lp['sh_w2']
    return routed + shared


def setup_inputs(seed: int = 0) -> dict:
    key = jax.random.key(seed)
    ks = jax.random.split(key, 33)
    D = D_MODEL

    def nrm(k, shape, scale):
        return scale * jax.random.normal(k, shape, F32)

    return {
        'x': nrm(ks[0], (BATCH, SEQ, D), 1.0),
        'c': nrm(ks[1], (BATCH, D), 1.0),
        'ctx': nrm(ks[2], (BATCH, CTX_LEN, D), 1.0),
        'c_ctx': nrm(ks[3], (D,), 1.0),
        'ada_w': nrm(ks[4], (DEPTH, D, 6 * D), 0.5 * D ** -0.5),
        'ada_b': nrm(ks[5], (DEPTH, 6 * D), 0.02),
        'norm1_w': 1.0 + nrm(ks[6], (DEPTH, D), 0.02),
        'norm2_w': 1.0 + nrm(ks[7], (DEPTH, D), 0.02),
        'w_in': nrm(ks[8], (DEPTH, D, N_IN), D ** -0.5),
        'hy_conv_w': nrm(ks[9], (DEPTH, HY_SHORT, HY_COLS), HY_SHORT ** -0.5),
        'hy_conv_b': nrm(ks[10], (DEPTH, HY_COLS), 0.02),
        'hy_w1': nrm(ks[11], (DEPTH, HY_EMB, HY_FFN), HY_EMB ** -0.5),
        'hy_b1': nrm(ks[12], (DEPTH, HY_FFN), 0.1),
        'hy_freq': 1.0 + nrm(ks[13], (DEPTH, 2, HY_FFN), 0.02),
        'hy_w2': nrm(ks[14], (DEPTH, HY_FFN, HY_FFN), HY_FFN ** -0.5),
        'hy_b2': nrm(ks[15], (DEPTH, HY_FFN), 0.1),
        'hy_w3': nrm(ks[16], (DEPTH, HY_FFN, 2 * HY_ORDER * HY_WIDTH), HY_FFN ** -0.5),
        'hy_bias': nrm(ks[17], (DEPTH, HY_ORDER, HY_WIDTH), 0.5),
        'gla_a_w2': nrm(ks[18], (DEPTH, 2, GLA_RANK, QK_W), GLA_RANK ** -0.5),
        'gla_a_b': nrm(ks[19], (DEPTH, 2, QK_W), 0.1),
        'gla_norm_w': 1.0 + nrm(ks[20], (DEPTH, V_W), 0.02),
        'proj_hy': nrm(ks[21], (DEPTH, HY_WIDTH, D), HY_WIDTH ** -0.5),
        'proj_gla': nrm(ks[22], (DEPTH, V_W, D), V_W ** -0.5),
        'w_out': nrm(ks[23], (DEPTH, D, D), D ** -0.5),
        'router_w': nrm(ks[24], (DEPTH, D, N_EXPERTS), D ** -0.5),
        'router_bias': nrm(ks[25], (DEPTH, N_EXPERTS), 0.01),
        'exp_w1': nrm(ks[26], (DEPTH, N_EXPERTS, D, D_EXPERT), D ** -0.5),
        'exp_w3': nrm(ks[27], (DEPTH, N_EXPERTS, D, D_EXPERT), D ** -0.5),
        'exp_w2': nrm(ks[28], (DEPTH, N_EXPERTS, D_EXPERT, D), D_EXPERT ** -0.5),
        'sh_w1': nrm(ks[29], (DEPTH, D, D_SHARED), D ** -0.5),
        'sh_w3': nrm(ks[30], (DEPTH, D, D_SHARED), D ** -0.5),
        'sh_w2': nrm(ks[31], (DEPTH, D_SHARED, D), D_SHARED ** -0.5),
        'final_norm_w': 1.0 + nrm(ks[32], (D,), 0.02),
    }


def reference(x, c, ctx, c_ctx, ada_w, ada_b, norm1_w, norm2_w, w_in, hy_conv_w, hy_conv_b,
              hy_w1, hy_b1, hy_freq, hy_w2, hy_b2, hy_w3, hy_bias, gla_a_w2, gla_a_b, gla_norm_w,
              proj_hy, proj_gla, w_out, router_w, router_bias, exp_w1, exp_w3, exp_w2,
              sh_w1, sh_w3, sh_w2, final_norm_w):
    n_rows = x.shape[1] // GRID_W
    for i in range(DEPTH):
        last = i == DEPTH - 1
        lp = {
            'w_in': w_in[i], 'hy_conv_w': hy_conv_w[i], 'hy_conv_b': hy_conv_b[i],
            'hy_w1': hy_w1[i], 'hy_b1': hy_b1[i], 'hy_freq': hy_freq[i], 'hy_w2': hy_w2[i],
            'hy_b2': hy_b2[i], 'hy_w3': hy_w3[i], 'hy_bias': hy_bias[i],
            'gla_a_w2': gla_a_w2[i], 'gla_a_b': gla_a_b[i], 'gla_norm_w': gla_norm_w[i],
            'proj_hy': proj_hy[i], 'proj_gla': proj_gla[i], 'w_out': w_out[i],
            'router_w': router_w[i], 'router_bias': router_bias[i],
            'exp_w1': exp_w1[i], 'exp_w3': exp_w3[i], 'exp_w2': exp_w2[i],
            'sh_w1': sh_w1[i], 'sh_w3': sh_w3[i], 'sh_w2': sh_w2[i],
        }
        sh1, sc1, g1, sh2, sc2, g2 = [m[:, None, :] for m in jnp.split(jax.nn.silu(c) @ ada_w[i] + ada_b[i], 6, axis=-1)]
        csh1, csc1, cg1, csh2, csc2, cg2 = jnp.split(jax.nn.silu(c_ctx) @ ada_w[i] + ada_b[i], 6, axis=-1)

        cn = modulate(rms_norm(ctx, norm1_w[i]), csh1, csc1)
        if last:
            s_f, s_b = context_states(cn, lp)
        else:
            s0 = jnp.zeros((ctx.shape[0], GLA_HEADS, GLA_DK, GLA_DV), F32)
            y_c, s_f, s_b = token_mixer(cn, 1, s0, s0, lp)
        xn = modulate(rms_norm(x, norm1_w[i]), sh1, sc1)
        y_x, _, _ = token_mixer(xn, n_rows, s_f, s_b, lp)
        x = x + g1 * y_x

        xn2 = modulate(rms_norm(x, norm2_w[i]), sh2, sc2)
        x = x + g2 * moe_ffn(xn2.reshape(-1, D_MODEL), lp).reshape(x.shape)
        if not last:
            ctx = ctx + cg1 * y_c
            cn2 = modulate(rms_norm(ctx, norm2_w[i]), csh2, csc2)
            ctx = ctx + cg2 * moe_ffn(cn2.reshape(-1, D_MODEL), lp).reshape(ctx.shape)
    return rms_norm(x, final_norm_w)
```

```python
import functools
import math

import jax
import jax.numpy as jnp
import numpy as np
from jax import lax
from jax.experimental import pallas as pl
from jax.experimental.pallas import tpu as pltpu

F32 = jnp.float32
BF16 = jnp.bfloat16
HIGHEST = lax.Precision.HIGHEST

GRID_W = 64
EPS = 1e-6
HY_ORDER = 2
HY_BANDS = 16
HY_FAST_DECAY = 0.3
HY_SLOW_DECAY = 1.5
HY_TARGET = 1e-2
GLA_HEADS = 4
GLA_RANK = 16
GLA_TAU = 16.0
N_EXPERTS = 64
N_GROUPS = 8
TOPK_GROUPS = 4
TOP_K = 8
ROUTED_SCALE = 2.5

LANES = 128
V7X_VMEM_BYTES = 64 * 1024 * 1024
VMEM_CAP_BYTES = 56 * 1024 * 1024

GLA_CHUNK = 64
GLA_BLOCK = 256
FFT_N2 = 64
EXP_PER_STEP = 4


def _params(sem, vmem_bytes):
    limit = int(min(VMEM_CAP_BYTES, max(16 * 1024 * 1024, vmem_bytes * 5 // 4 + (2 << 20))))
    return pltpu.CompilerParams(dimension_semantics=sem, vmem_limit_bytes=limit)


def _nt(a, b, **kw):
    return lax.dot_general(a, b, (((1,), (1,)), ((), ())), preferred_element_type=F32, **kw)


def _tn(a, b):
    return lax.dot_general(a, b, (((0,), (0,)), ((), ())), preferred_element_type=F32)


def _dot(a, b, **kw):
    return jnp.dot(a, b, preferred_element_type=F32, **kw)


def _silu(x):
    return x * jax.nn.sigmoid(x)


def _mods_body(c_ref, w_ref, b_ref, o_ref):
    o_ref[...] = _dot(_silu(c_ref[...]), w_ref[...], precision=HIGHEST) + b_ref[...]


def _mods(cc, w, b):
    rows, d = cc.shape
    n = w.shape[1]
    tn = n // 4
    return pl.pallas_call(
        _mods_body,
        grid=(n // tn,),
        in_specs=[pl.BlockSpec((rows, d), lambda j: (0, 0)),
                  pl.BlockSpec((d, tn), lambda j: (0, j)),
                  pl.BlockSpec((1, tn), lambda j: (0, j))],
        out_specs=pl.BlockSpec((rows, tn), lambda j: (0, j)),
        out_shape=jax.ShapeDtypeStruct((rows, n), F32),
        compiler_params=_params(("arbitrary",), 2 * d * tn * 4),
        name="mods",
    )(cc, w, b)


def _norm_mod(x, w, shift, scale):
    ms = jnp.mean(x * x, axis=-1, keepdims=True)
    return (x * lax.rsqrt(ms + EPS) * w) * (1.0 + scale) + shift


def _inproj_body(x_ref, nw_ref, sh_ref, sc_ref, w_ref, o_ref, xn_ref):
    @pl.when(pl.program_id(2) == 0)
    def _():
        xn_ref[...] = _norm_mod(x_ref[0], nw_ref[...], sh_ref[0], sc_ref[0]).astype(BF16)

    o_ref[0] = _dot(xn_ref[...], w_ref[...]).astype(BF16)


def _inproj(x, nw, shift, scale, w, tm, tn):
    b, l, d = x.shape
    n = w.shape[1]
    vm = 2 * tm * d * 4 + 2 * d * tn * 2 + 2 * tm * tn * 2 + tm * d * 2
    return pl.pallas_call(
        _inproj_body,
        grid=(b, l // tm, n // tn),
        in_specs=[pl.BlockSpec((1, tm, d), lambda bi, i, j: (bi, i, 0)),
                  pl.BlockSpec((1, d), lambda bi, i, j: (0, 0)),
                  pl.BlockSpec((1, 1, d), lambda bi, i, j: (bi, 0, 0)),
                  pl.BlockSpec((1, 1, d), lambda bi, i, j: (bi, 0, 0)),
                  pl.BlockSpec((d, tn), lambda bi, i, j: (0, j))],
        out_specs=pl.BlockSpec((1, tm, tn), lambda bi, i, j: (bi, i, j)),
        out_shape=jax.ShapeDtypeStruct((b, l, n), BF16),
        scratch_shapes=[pltpu.VMEM((tm, d), BF16)],
        compiler_params=_params(("arbitrary", "arbitrary", "arbitrary"), vm),
        name="inproj",
    )(x, nw, shift, scale, w)


def _gla_tables(c, inclusive, flip):
    idx = np.arange(c)
    i = idx[:, None]
    x = idx[None, :]
    blocks = [(x <= i) if inclusive else (x < i), x > i]
    masks = []
    h = c // 2
    while h >= 1:
        mid = (idx // (2 * h)) * (2 * h) + h
        mi = mid[:, None]
        hi = i if inclusive else i - 1
        blocks.append((i >= mi) & (x >= mi) & (x <= hi))
        blocks.append((i < mi) & (x > i) & (x <= mi - 1))
        same = (idx[:, None] // (2 * h)) == (idx[None, :] // (2 * h))
        masks.append(same & (idx[:, None] >= mi) & (idx[None, :] < mid[None, :]))
        h //= 2
    masks.append(np.eye(c, dtype=bool))
    if flip:
        blocks = [b[::-1, ::-1] for b in blocks]
        masks = [m[::-1, ::-1] for m in masks]
    lall = np.concatenate(blocks + [np.ones((8, c), bool)], axis=0)
    return lall.astype(np.float32), np.stack(masks).astype(np.float32)


def _gla_chunk(q, k, v, a, wa, ba, lall, masks_ref, st_ref, inclusive, q_scale):
    c, dk = k.shape
    n_levels = int(math.log2(c))
    xg = _dot(a, wa) + ba
    g = (jnp.minimum(xg, 0.0) - jnp.log(1.0 + jnp.exp(-jnp.abs(xg)))) * (1.0 / GLA_TAU)
    g_hi = g.astype(BF16)
    r1 = g - g_hi.astype(F32)
    g_mid = r1.astype(BF16)
    g_lo = (r1 - g_mid.astype(F32)).astype(BF16)
    e3 = _dot(lall, jnp.concatenate([g_hi, g_mid, g_lo], axis=1))
    e = e3[:, :dk] + e3[:, dk:2 * dk] + e3[:, 2 * dk:]
    ex = jnp.exp(jnp.minimum(e, 0.0))

    kf = k.astype(F32)
    st = st_ref[...]
    k1 = (kf * ex[c:2 * c]).astype(BF16)
    st_ref[...] = st * ex[(2 + 2 * n_levels) * c:(2 + 2 * n_levels) * c + 1] + _tn(v, k1)
    if q is None:
        return None
    qf = q.astype(F32) * q_scale
    o = _nt((qf * ex[0:c]).astype(BF16), st.astype(BF16))
    attn = jnp.zeros((c, c), F32)
    for lv in range(n_levels):
        ql = (qf * ex[(2 + 2 * lv) * c:(3 + 2 * lv) * c]).astype(BF16)
        kl = (kf * ex[(3 + 2 * lv) * c:(4 + 2 * lv) * c]).astype(BF16)
        attn = attn + _nt(ql, kl) * masks_ref[lv]
    if inclusive:
        attn = attn + _nt(qf.astype(BF16), k) * masks_ref[n_levels]
    return o + _dot(attn.astype(BF16), v)


def _gla_body(*refs, with_q, n_sub, chunk, q_scale):
    if with_q:
        (kf_ref, vf_ref, af_ref, qf_ref, kb_ref, vb_ref, ab_ref, qb_ref, waf_ref, wab_ref, baf_ref, bab_ref,
         lf_ref, lb_ref, mf_ref, mb_ref, s0f_ref, s0b_ref, of_ref, ob_ref, sf_ref, sb_ref, stf_ref, stb_ref) = refs
    else:
        (kf_ref, vf_ref, af_ref, kb_ref, vb_ref, ab_ref, waf_ref, wab_ref, baf_ref, bab_ref,
         lf_ref, lb_ref, mf_ref, mb_ref, s0f_ref, s0b_ref, sf_ref, sb_ref, stf_ref, stb_ref) = refs
        qf_ref = qb_ref = of_ref = ob_ref = None

    @pl.when(pl.program_id(2) == 0)
    def _():
        stf_ref[...] = s0f_ref[0, 0]
        stb_ref[...] = s0b_ref[0, 0]

    for s in range(n_sub):
        sl = slice(s * chunk, (s + 1) * chunk)
        o = _gla_chunk(None if qf_ref is None else qf_ref[0, sl, :], kf_ref[0, sl, :], vf_ref[0, sl, :],
                       af_ref[0, sl, :], waf_ref[0], baf_ref[0], lf_ref[...], mf_ref, stf_ref, True, q_scale)
        if with_q:
            of_ref[0, sl, :] = o.astype(BF16)
    for s in reversed(range(n_sub)):
        sl = slice(s * chunk, (s + 1) * chunk)
        o = _gla_chunk(None if qb_ref is None else qb_ref[0, sl, :], kb_ref[0, sl, :], vb_ref[0, sl, :],
                       ab_ref[0, sl, :], wab_ref[0], bab_ref[0], lb_ref[...], mb_ref, stb_ref, False, q_scale)
        if with_q:
            ob_ref[0, sl, :] = o.astype(BF16)
    sf_ref[0, 0] = stf_ref[...]
    sb_ref[0, 0] = stb_ref[...]


def _gla(u, cols, waf, wab, baf, bab, s0f, s0b, dk, dv, with_q, tb):
    b, l, _ = u.shape
    h = GLA_HEADS
    nb = l // tb
    n_sub = tb // GLA_CHUNK
    lf, mf = _gla_tables(GLA_CHUNK, True, False)
    lb, mb = _gla_tables(GLA_CHUNK, False, True)
    lf, lb = jnp.asarray(lf, BF16), jnp.asarray(lb, BF16)
    mf, mb = jnp.asarray(mf), jnp.asarray(mb)
    kc, vc, ac, qc = cols

    def seq_specs(rev):
        def blk(i):
            return (nb - 1 - i) if rev else i
        specs = [pl.BlockSpec((1, tb, dk), lambda bi, hi, i: (bi, blk(i), kc // dk + hi)),
                 pl.BlockSpec((1, tb, dv), lambda bi, hi, i: (bi, blk(i), vc // dv + hi)),
                 pl.BlockSpec((1, tb, LANES), lambda bi, hi, i: (bi, blk(i), ac // LANES))]
        if with_q:
            specs.append(pl.BlockSpec((1, tb, dk), lambda bi, hi, i: (bi, blk(i), qc // dk + hi)))
        return specs

    def const_spec(shape):
        nd = len(shape)
        return pl.BlockSpec(shape, lambda bi, hi, i: (0,) * nd)

    head_w = pl.BlockSpec((1, LANES, dk), lambda bi, hi, i: (hi, 0, 0))
    head_b = pl.BlockSpec((1, 1, dk), lambda bi, hi, i: (hi, 0, 0))
    st_spec = pl.BlockSpec((1, 1, dv, dk), lambda bi, hi, i: (bi, hi, 0, 0))
    in_specs = (seq_specs(False) + seq_specs(True) + [head_w, head_w, head_b, head_b,
                const_spec(lf.shape), const_spec(lb.shape), const_spec(mf.shape), const_spec(mb.shape),
                st_spec, st_spec])
    st_shape = jax.ShapeDtypeStruct((b, h, dv, dk), F32)
    if with_q:
        o_shape = jax.ShapeDtypeStruct((b, l, h * dv), BF16)
        out_shape = (o_shape, o_shape, st_shape, st_shape)
        out_specs = (pl.BlockSpec((1, tb, dv), lambda bi, hi, i: (bi, i, hi)),
                     pl.BlockSpec((1, tb, dv), lambda bi, hi, i: (bi, nb - 1 - i, hi)),
                     st_spec, st_spec)
        args = (u,) * 8
    else:
        out_shape = (st_shape, st_shape)
        out_specs = (st_spec, st_spec)
        args = (u,) * 6
    vm = 4 * tb * (2 * dk + dv + LANES) * 2 * 2 + 8 * dv * dk * 4 + 4 * tb * dv * 2 + (4 << 20)
    outs = pl.pallas_call(
        functools.partial(_gla_body, with_q=with_q, n_sub=n_sub, chunk=GLA_CHUNK, q_scale=dk ** -0.5),
        grid=(b, h, nb),
        in_specs=in_specs,
        out_specs=out_specs,
        out_shape=out_shape,
        scratch_shapes=[pltpu.VMEM((dv, dk), F32), pltpu.VMEM((dv, dk), F32)],
        compiler_params=_params(("arbitrary", "arbitrary", "arbitrary"), vm),
        name="gla" if with_q else "gla_ctx",
    )(*args, waf, wab, baf, bab, lf, lb, mf, mb, s0f, s0b)
    if with_q:
        return outs
    return None, None, outs[0], outs[1]


def _hy_tables(l):
    t = np.linspace(0.0, 1.0, l, dtype=np.float32).astype(np.float64)[:, None]
    w = 2.0 * math.pi * np.arange(l, dtype=np.float64)[:, None] / l
    f = np.linspace(1e-4, HY_BANDS - 1, HY_BANDS, dtype=np.float32).astype(np.float64)[None, :]
    z = np.concatenate([t, np.cos(f * w), -np.sin(f * w)], axis=-1)
    rev = (l - np.arange(l)) % l

    def pad(a):
        out = np.zeros((l, LANES), np.float32)
        out[:, :a.shape[1]] = a
        return out

    tt = np.broadcast_to(t, (l, LANES)).astype(np.float32)
    return pad(z), pad(z[rev]), tt, np.ascontiguousarray(tt[rev])


def _hyfilt_body(z1_ref, z2_ref, t1_ref, t2_ref, w1_ref, b1_ref, f1_ref, w2_ref, b2_ref, f2_ref,
                 w3f_ref, w3b_ref, dl_ref, o_ref, ha_ref, hb_ref):
    l = z1_ref.shape[0]

    @pl.when((pl.program_id(0) == 0) & (pl.program_id(1) == 0))
    def _():
        for z_ref, h_ref in ((z1_ref, ha_ref), (z2_ref, hb_ref)):
            h = jnp.sin(f1_ref[...] * (_dot(z_ref[...], w1_ref[...], precision=HIGHEST) + b1_ref[...]))
            h_ref[...] = jnp.sin(f2_ref[...] * (_dot(h, w2_ref[...], precision=HIGHEST) + b2_ref[...]))

    dl = dl_ref[...]
    hf = _dot(ha_ref[...], w3f_ref[...], precision=HIGHEST) * jnp.exp(-t1_ref[...] * dl)
    hb = _dot(hb_ref[...], w3b_ref[...], precision=HIGHEST) * jnp.exp(-t2_ref[...] * dl)
    row = lax.broadcasted_iota(jnp.int32, hb.shape, 0)
    hb = jnp.where(row == 0, 0.0, hb)
    ss = jnp.sum(hf * hf, axis=0, keepdims=True) + jnp.sum(hb * hb, axis=0, keepdims=True)
    scale = lax.rsqrt(ss)
    o_ref[0, 0:l, :] = hf * scale
    o_ref[0, l:2 * l, :] = hb * scale


def _hyfilt(l, c, w1, b1, f1, w2, b2, f2, w3, deltas):
    z1, z2, t1, t2 = (jnp.asarray(a) for a in _hy_tables(l))
    ncb = c // LANES
    tab = pl.BlockSpec((l, LANES), lambda o, j: (0, 0))
    sq = pl.BlockSpec((LANES, LANES), lambda o, j: (0, 0))
    row = pl.BlockSpec((1, LANES), lambda o, j: (0, 0))
    return pl.pallas_call(
        _hyfilt_body,
        grid=(HY_ORDER, ncb),
        in_specs=[tab, tab, tab, tab, sq, row, row, sq, row, row,
                  pl.BlockSpec((LANES, LANES), lambda o, j: (0, 2 * o * ncb + j)),
                  pl.BlockSpec((LANES, LANES), lambda o, j: (0, (2 * o + 1) * ncb + j)),
                  pl.BlockSpec((1, LANES), lambda o, j: (0, j))],
        out_specs=pl.BlockSpec((1, 2 * l, LANES), lambda o, j: (o, 0, j)),
        out_shape=jax.ShapeDtypeStruct((HY_ORDER, 2 * l, c), F32),
        scratch_shapes=[pltpu.VMEM((l, LANES), F32), pltpu.VMEM((l, LANES), F32)],
        compiler_params=_params(("arbitrary", "arbitrary"), 24 * l * LANES * 4),
        name="hyfilt",
    )(z1, z2, t1, t2, w1, b1, f1, w2, b2, f2, w3, w3, deltas)


def _fft_tables(n, n2):
    n1 = n // n2
    h = n1 // 2
    k1 = np.arange(n1)[:, None]
    a = 2.0 * math.pi * k1 * np.arange(h)[None, :] / n1
    c, s = np.cos(a), np.sin(a)
    f1c = np.block([[c, s], [-s, c]])
    a = 2.0 * math.pi * k1 * np.arange(n1)[None, :] / n1
    f1r = np.concatenate([np.cos(a), -np.sin(a)], axis=0)
    kk = np.arange(n1)[:, None, None] + n1 * np.arange(n2)[None, :, None]
    a = 2.0 * math.pi * kk * np.arange(n2)[None, None, :] / n
    c, s = np.cos(a), np.sin(a)
    gf = np.concatenate([np.concatenate([c, s], axis=2), np.concatenate([-s, c], axis=2)], axis=1)
    ct, st = np.swapaxes(c, 1, 2), np.swapaxes(s, 1, 2)
    gi = np.concatenate([np.concatenate([ct, -st], axis=2), np.concatenate([st, ct], axis=2)], axis=1)
    a = 2.0 * math.pi * np.arange(h)[:, None] * np.arange(n1)[None, :] / n1
    c, s = np.cos(a), np.sin(a)
    if1 = np.block([[c, -s], [s, c]])
    return tuple(jnp.asarray(m, BF16) for m in (f1c, f1r, gf, gi, if1))


def _hyspec_body(k_ref, f1_ref, gf_ref, o_ref, as_ref, *, n):
    n1, n2 = k_ref.shape[1], k_ref.shape[2]

    def stage1(j, carry):
        r = k_ref[0, :, j, :].astype(BF16)
        as_ref[j, 0:2 * n1, :] = _dot(f1_ref[...], r)
        return carry

    lax.fori_loop(0, n2, stage1, 0)

    def stage2(k1, carry):
        r = jnp.concatenate([as_ref[:, k1, :], as_ref[:, n1 + k1, :]], axis=0).astype(BF16)
        o_ref[0, k1] = (_dot(gf_ref[k1], r) * (1.0 / n)).astype(BF16)
        return carry

    lax.fori_loop(0, n1, stage2, 0)


def _hyspec(kern, f1r, gf, n2):
    order, n, c = kern.shape
    n1 = n // n2
    pitch = 2 * n1 + 8
    return pl.pallas_call(
        functools.partial(_hyspec_body, n=n),
        grid=(order, c // LANES),
        in_specs=[pl.BlockSpec((1, n1, n2, LANES), lambda o, j: (o, 0, 0, j)),
                  pl.BlockSpec(f1r.shape, lambda o, j: (0, 0)),
                  pl.BlockSpec(gf.shape, lambda o, j: (0, 0, 0))],
        out_specs=pl.BlockSpec((1, n1, 2 * n2, LANES), lambda o, j: (o, 0, 0, j)),
        out_shape=jax.ShapeDtypeStruct((order, n1, 2 * n2, c), BF16),
        scratch_shapes=[pltpu.VMEM((n2, pitch, LANES), F32)],
        compiler_params=_params(("arbitrary", "arbitrary"),
                                2 * n * LANES * 4 + 2 * gf.size * 2 + n2 * pitch * LANES * 4 + 2 * n * LANES * 2),
        name="hyspec",
    )(kern.reshape(order, n1, n2, c), f1r, gf)


def _short_conv(u, w_ref, b_ref):
    r = u.shape[0]
    row = lax.broadcasted_iota(jnp.int32, u.shape, 0)
    up = jnp.where(row == 0, 0.0, pltpu.roll(u, 1, axis=0))
    dn = jnp.where(row == r - 1, 0.0, pltpu.roll(u, r - 1, axis=0))
    return up * w_ref[0:1, :] + u * w_ref[1:2, :] + dn * w_ref[2:3, :] + b_ref[...]


def _hyconv_body(z_ref, g_ref, zw_ref, zb_ref, gw_ref, gb_ref, sp_ref, hb_ref, f1_ref, gf_ref, gi_ref, if1_ref,
                 o_ref, x_ref, as_ref, y_ref, *, conv_z):
    n1h, n2 = x_ref.shape[1], z_ref.shape[1] // x_ref.shape[1]
    n1 = 2 * n1h

    def load_z(p, i):
        u = z_ref[p, pl.ds(pl.multiple_of(i * n2, n2), n2), :].astype(F32)
        return _short_conv(u, zw_ref, zb_ref) if conv_z else u

    def fill(i, carry):
        for p in range(2):
            x_ref[p, i, 0:n2, :] = load_z(p, i)
        return carry

    lax.fori_loop(0, n1h, fill, 0)

    def stage1(j, carry):
        r = jnp.concatenate([x_ref[0, :, j, :], x_ref[1, :, j, :]], axis=0).astype(BF16)
        as_ref[j, 0:2 * n1, :] = _dot(f1_ref[...], r)
        return carry

    lax.fori_loop(0, n2, stage1, 0)

    def stage2(k1, carry):
        r = jnp.concatenate([as_ref[:, k1, :], as_ref[:, n1 + k1, :]], axis=0).astype(BF16)
        xk = _dot(gf_ref[k1], r)
        xr, xi = xk[0:n2], xk[n2:2 * n2]
        sp = sp_ref[0, k1].astype(F32)
        sr, si = sp[0:n2], sp[n2:2 * n2]
        yk = jnp.concatenate([xr * sr - xi * si, xr * si + xi * sr], axis=0).astype(BF16)
        bk = _dot(gi_ref[k1], yk)
        as_ref[:, k1, :] = bk[0:n2]
        as_ref[:, n1 + k1, :] = bk[n2:2 * n2]
        return carry

    lax.fori_loop(0, n1, stage2, 0)

    def stage3(j, carry):
        yn = _dot(if1_ref[...], as_ref[j, 0:2 * n1, :].astype(BF16))
        y_ref[0, :, j, :] = yn[0:n1h]
        y_ref[1, :, j, :] = yn[n1h:n1]
        return carry

    lax.fori_loop(0, n2, stage3, 0)

    def finish(i, carry):
        rows = pl.ds(pl.multiple_of(i * n2, n2), n2)
        for p in range(2):
            gate = _short_conv(g_ref[p, rows, :].astype(F32), gw_ref, gb_ref)
            z = x_ref[p, i, 0:n2, :]
            o_ref[p, rows, :] = (gate * (y_ref[p, i, 0:n2, :] + z * hb_ref[...])).astype(BF16)
        return carry

    lax.fori_loop(0, n1h, finish, 0)


def _hyconv(z, z_col, g, g_col, conv_w, conv_b, zw_col, gw_col, spec, order, hy_bias, tabs, conv_z):
    b, l, _ = z.shape
    f1c, _, gf, gi, if1 = tabs
    n1, n2x2 = gf.shape[0], gf.shape[1]
    n2 = n2x2 // 2
    n1h = n1 // 2
    c = spec.shape[-1]
    ncb = c // LANES
    xp = n2 + 8
    ap = 2 * n1 + 8
    vm = (2 * 2 * 2 * l * LANES * 2 + 2 * 2 * l * LANES * 2 + 2 * n1 * n2x2 * LANES * 2
          + 2 * 2 * gf.size * 2 + 2 * 2 * n1h * xp * LANES * 4 + n2 * ap * LANES * 4)
    return pl.pallas_call(
        functools.partial(_hyconv_body, conv_z=conv_z),
        grid=(ncb, b // 2),
        in_specs=[pl.BlockSpec((2, l, LANES), lambda j, p: (p, 0, z_col // LANES + j)),
                  pl.BlockSpec((2, l, LANES), lambda j, p: (p, 0, g_col // LANES + j)),
                  pl.BlockSpec((3, LANES), lambda j, p: (0, zw_col // LANES + j)),
                  pl.BlockSpec((1, LANES), lambda j, p: (0, zw_col // LANES + j)),
                  pl.BlockSpec((3, LANES), lambda j, p: (0, gw_col // LANES + j)),
                  pl.BlockSpec((1, LANES), lambda j, p: (0, gw_col // LANES + j)),
                  pl.BlockSpec((1, n1, n2x2, LANES), lambda j, p: (order, 0, 0, j)),
                  pl.BlockSpec((1, LANES), lambda j, p: (0, j)),
                  pl.BlockSpec(f1c.shape, lambda j, p: (0, 0)),
                  pl.BlockSpec(gf.shape, lambda j, p: (0, 0, 0)),
                  pl.BlockSpec(gi.shape, lambda j, p: (0, 0, 0)),
                  pl.BlockSpec(if1.shape, lambda j, p: (0, 0))],
        out_specs=pl.BlockSpec((2, l, LANES), lambda j, p: (p, 0, j)),
        out_shape=jax.ShapeDtypeStruct((b, l, c), BF16),
        scratch_shapes=[pltpu.VMEM((2, n1h, xp, LANES), F32),
                        pltpu.VMEM((n2, ap, LANES), F32),
                        pltpu.VMEM((2, n1h, xp, LANES), F32)],
        compiler_params=_params(("arbitrary", "arbitrary"), vm),
        name="hyconv%d" % order,
    )(z, g, conv_w, conv_b, conv_w, conv_b, spec, hy_bias, f1c, gf, gi, if1)


def _mix_body(x_ref, of_ref, ob_ref, ug_ref, ugate_ref, yhy_ref, gnw_ref, phy_ref, pgla_ref, wout_ref,
              g1_ref, n2w_ref, sh2_ref, sc2_ref, rwt_ref, x1_ref, xn2_ref, lg_ref, *, heads):
    d = x_ref.shape[2]
    o = of_ref[0].astype(F32) + ob_ref[0].astype(F32)
    dv = o.shape[1] // heads
    parts = []
    for h in range(heads):
        seg = o[:, h * dv:(h + 1) * dv]
        parts.append(seg * lax.rsqrt(jnp.mean(seg * seg, axis=-1, keepdims=True) + EPS))
    y_gla = jnp.concatenate(parts, axis=1) * gnw_ref[...] * _silu(ug_ref[0].astype(F32))
    gates = jax.nn.sigmoid(ugate_ref[0].astype(F32))
    merged = (gates[:, :d] * _dot(yhy_ref[0], phy_ref[...])
              + gates[:, d:] * _dot(y_gla.astype(BF16), pgla_ref[...]))
    x1 = x_ref[0] + g1_ref[0] * _dot(merged.astype(BF16), wout_ref[...])
    x1_ref[0] = x1
    xn2 = _norm_mod(x1, n2w_ref[...], sh2_ref[0], sc2_ref[0])
    xn2_ref[0] = xn2.astype(BF16)
    lg_ref[0] = _nt(rwt_ref[...], xn2, precision=HIGHEST)


def _mix(x, o_f, o_b, u, g_col, gate_col, y_hy, gnw, phy, pgla, wout, g1, n2w, sh2, sc2, rwt, tm):
    b, l, d = x.shape
    vw = o_f.shape[2]
    ne = rwt.shape[0]
    tok = lambda w: pl.BlockSpec((1, tm, w), lambda bi, i: (bi, i, 0))
    per_b = pl.BlockSpec((1, 1, d), lambda bi, i: (bi, 0, 0))
    const = lambda shape: pl.BlockSpec(shape, lambda bi, i: (0, 0))
    vm = (2 * tm * d * 4 * 2 + 2 * tm * (3 * vw + 2 * d + 2 * d) * 2 + 2 * 3 * d * d * 2 + 12 * tm * d * 4)
    return pl.pallas_call(
        functools.partial(_mix_body, heads=GLA_HEADS),
        grid=(b, l // tm),
        in_specs=[tok(d), tok(vw), tok(vw),
                  pl.BlockSpec((1, tm, vw), lambda bi, i: (bi, i, g_col // vw)),
                  pl.BlockSpec((1, tm, 2 * d), lambda bi, i: (bi, i, gate_col // (2 * d))),
                  tok(d), const((1, vw)), const((d, d)), const((vw, d)), const((d, d)),
                  per_b, const((1, d)), per_b, per_b, const((ne, d))],
        out_specs=(tok(d), tok(d), pl.BlockSpec((1, ne, tm), lambda bi, i: (bi, 0, i))),
        out_shape=(jax.ShapeDtypeStruct((b, l, d), F32), jax.ShapeDtypeStruct((b, l, d), BF16),
                   jax.ShapeDtypeStruct((b, ne, l), F32)),
        compiler_params=_params(("arbitrary", "arbitrary"), vm),
        name="mix",
    )(x, o_f, o_b, u, u, y_hy, gnw, phy, pgla, wout, g1, n2w, sh2, sc2, rwt)


def _select_body(lg_ref, bias_ref, o_ref):
    ne, tn = lg_ref.shape[1], lg_ref.shape[2]
    ng = N_GROUPS
    pg = ne // ng
    scores = jax.nn.sigmoid(lg_ref[0]).reshape(ng, pg, tn)
    sel = scores + bias_ref[...]
    ie = lax.broadcasted_iota(jnp.int32, sel.shape, 1)
    m1 = jnp.max(sel, axis=1, keepdims=True)
    i1 = jnp.min(jnp.where(sel == m1, ie, pg), axis=1, keepdims=True)
    m2 = jnp.max(jnp.where(ie == i1, -jnp.inf, sel), axis=1, keepdims=True)
    grp = m1 + m2
    ig = lax.broadcasted_iota(jnp.int32, grp.shape, 0)
    rank = jnp.zeros(grp.shape, jnp.int32)
    for g in range(ng):
        other = grp[g:g + 1]
        rank = rank + jnp.where((other > grp) | ((other == grp) & (g < ig)), 1, 0)
    cand = jnp.where(rank < TOPK_GROUPS, sel, -jnp.inf)
    flat = ig * pg + ie
    rank = jnp.zeros(sel.shape, jnp.int32)
    for g in range(ng):
        for e in range(pg):
            other = cand[g:g + 1, e:e + 1, :]
            rank = rank + jnp.where((other > cand) | ((other == cand) & (g * pg + e < flat)), 1, 0)
    w = jnp.where(rank < TOP_K, scores, 0.0)
    tot = jnp.sum(jnp.sum(w, axis=1, keepdims=True), axis=0, keepdims=True)
    comb = (w / tot * ROUTED_SCALE).reshape(ne, tn)
    o_ref[...] = jnp.concatenate([comb, jnp.zeros((LANES - ne, tn), F32)], axis=0).T


def _select(logits_t, bias, tn):
    b, ne, l = logits_t.shape
    nb = l // tn
    return pl.pallas_call(
        _select_body,
        grid=(b, nb),
        in_specs=[pl.BlockSpec((1, ne, tn), lambda bi, i: (bi, 0, i)),
                  pl.BlockSpec(bias.shape, lambda bi, i: (0, 0, 0))],
        out_specs=pl.BlockSpec((tn, LANES), lambda bi, i: (bi * nb + i, 0)),
        out_shape=jax.ShapeDtypeStruct((b * l, LANES), F32),
        compiler_params=_params(("arbitrary", "arbitrary"), 64 * ne * tn * 4),
        name="select",
    )(logits_t, bias)


def _moe_body(x_ref, comb_ref, pick_ref, w1_ref, w3_ref, w2_ref, sw1_ref, sw3_ref, sw2_ref,
              x1_ref, g2_ref, fnw_ref, o_ref, acc_ref):
    g = pl.program_id(1)
    x = x_ref[...]
    per = w1_ref.shape[0]

    @pl.when(g == 0)
    def _():
        hs = _silu(_dot(x, sw1_ref[...])) * _dot(x, sw3_ref[...])
        acc_ref[...] = _dot(hs.astype(BF16), sw2_ref[...])

    comb = comb_ref[...]
    c_hi = comb.astype(BF16)
    c_lo = (comb - c_hi.astype(F32)).astype(BF16)
    cg = _dot(c_hi, pick_ref[0]) + _dot(c_lo, pick_ref[0])
    acts = []
    for e in range(per):
        a = _silu(_dot(x, w1_ref[e])) * _dot(x, w3_ref[e])
        acts.append((a * cg[:, e:e + 1]).astype(BF16))
    w2 = w2_ref[...]
    acc_ref[...] += _dot(jnp.concatenate(acts, axis=1), w2.reshape(per * w2.shape[1], w2.shape[2]))

    @pl.when(g == pl.num_programs(1) - 1)
    def _():
        y = x1_ref[...] + g2_ref[0] * acc_ref[...]
        ms = jnp.mean(y * y, axis=-1, keepdims=True)
        o_ref[...] = y * lax.rsqrt(ms + EPS) * fnw_ref[...]


def _moe(xn2, comb, w1, w3, w2, sw1, sw3, sw2, x1, g2, fnw, tokens_per_batch, tm):
    t, d = xn2.shape
    ne, _, f = w1.shape
    per = EXP_PER_STEP
    ng = ne // per
    pick = np.zeros((ng, LANES, LANES), np.float32)
    for gi in range(ng):
        for e in range(per):
            pick[gi, gi * per + e, e] = 1.0
    pick = jnp.asarray(pick, BF16)
    bpb = tokens_per_batch // tm
    tok = lambda w: pl.BlockSpec((tm, w), lambda i, g: (i, 0))
    const = lambda shape: pl.BlockSpec(shape, lambda i, g: (0,) * len(shape))
    vm = (2 * tm * d * 2 + 2 * tm * LANES * 4 + 2 * 3 * per * d * f * 2 + 2 * 3 * d * f * 2
          + 2 * tm * d * 4 * 2 + tm * d * 4 + 6 * tm * per * f * 4)
    return pl.pallas_call(
        _moe_body,
        grid=(t // tm, ng),
        in_specs=[tok(d), tok(LANES),
                  pl.BlockSpec((1, LANES, LANES), lambda i, g: (g, 0, 0)),
                  pl.BlockSpec((per, d, f), lambda i, g: (g, 0, 0)),
                  pl.BlockSpec((per, d, f), lambda i, g: (g, 0, 0)),
                  pl.BlockSpec((per, f, d), lambda i, g: (g, 0, 0)),
                  const(sw1.shape), const(sw3.shape), const(sw2.shape),
                  tok(d),
                  pl.BlockSpec((1, 1, d), lambda i, g: (i // bpb, 0, 0)),
                  const((1, d))],
        out_specs=tok(d),
        out_shape=jax.ShapeDtypeStruct((t, d), F32),
        scratch_shapes=[pltpu.VMEM((tm, d), F32)],
        compiler_params=_params(("arbitrary", "arbitrary"), vm),
        name="moe",
    )(xn2, comb, pick, w1, w3, w2, sw1, sw3, sw2, x1, g2, fnw)


def _pad_to(a, rows, cols):
    return jnp.pad(a, ((0, rows - a.shape[0]), (0, cols - a.shape[1])))


def kernel(x, c, ctx, c_ctx, ada_w, ada_b, norm1_w, norm2_w, w_in, hy_conv_w, hy_conv_b, hy_w1, hy_b1, hy_freq, hy_w2, hy_b2, hy_w3, hy_bias, gla_a_w2, gla_a_b, gla_norm_w, proj_hy, proj_gla, w_out, router_w, router_bias, exp_w1, exp_w3, exp_w2, sh_w1, sh_w3, sh_w2, final_norm_w):
    b, l, d = x.shape
    assert ada_w.shape[0] == 1, "single-layer block"
    assert l // GRID_W * GRID_W == l and FFT_N2 == GRID_W
    heads = GLA_HEADS
    qk_w = d // 2
    dk = qk_w // heads
    v_w = d
    dv = v_w // heads
    a_w = 2 * GLA_RANK
    hy_w = d
    hy_cols = (HY_ORDER + 1) * hy_w

    rows = -(-(b + 1) // 8) * 8
    cc = jnp.zeros((rows, d), F32).at[:b].set(c).at[b].set(c_ctx)
    mods = _mods(cc, ada_w[0], ada_b[0][None])
    sh1, sc1, g1, sh2, sc2, g2 = [m[:b, None, :] for m in jnp.split(mods, 6, axis=-1)]
    csh1, csc1 = [jnp.broadcast_to(m[b][None, None, :], (b, 1, d)) for m in jnp.split(mods, 6, axis=-1)[:2]]

    w = w_in[0]
    o_a = qk_w + v_w
    o_q = o_a + a_w
    o_g = o_q + qk_w
    o_hy = o_g + v_w
    o_gate = o_hy + hy_cols
    w_k, w_v = w[:, :qk_w], w[:, qk_w:o_a]
    w_a = jnp.pad(w[:, o_a:o_q], ((0, 0), (0, LANES - a_w)))
    wp = jnp.concatenate([w[:, o_gate:], w[:, o_g:o_hy], w_v, w_k, w[:, o_q:o_g], w[:, o_hy:o_gate], w_a],
                         axis=1).astype(BF16)
    p_gate = 0
    p_g = 2 * d
    p_v = p_g + v_w
    p_k = p_v + v_w
    p_q = p_k + qk_w
    p_hy = p_q + qk_w
    p_a = p_hy + hy_cols
    n_all = p_a + LANES
    w_ctx = jnp.concatenate([w_k, w_v, w_a], axis=1).astype(BF16)
    cols_ctx = (0, qk_w, qk_w + v_w, None)
    cols = (p_k, p_v, p_a, p_q)

    nw1 = norm1_w[0][None]
    u_ctx = _inproj(ctx, nw1, csh1, csc1, w_ctx, ctx.shape[1], w_ctx.shape[1])
    u = _inproj(x, nw1, sh1, sc1, wp, 1024, n_all // 5)

    wa = gla_a_w2[0].reshape(2, GLA_RANK, heads, dk).transpose(0, 2, 1, 3)
    waf = jnp.pad(wa[0], ((0, 0), (0, LANES - GLA_RANK), (0, 0))).astype(BF16)
    wab = jnp.pad(wa[1], ((0, 0), (GLA_RANK, LANES - 2 * GLA_RANK), (0, 0))).astype(BF16)
    ba = gla_a_b[0].reshape(2, heads, 1, dk)
    zeros_state = jnp.zeros((b, heads, dv, dk), F32)
    _, _, s_f, s_b = _gla(u_ctx, cols_ctx, waf, wab, ba[0], ba[1], zeros_state, zeros_state, dk, dv, False,
                          ctx.shape[1])
    o_f, o_b, _, _ = _gla(u, cols, waf, wab, ba[0], ba[1], s_f, s_b, dk, dv, True, GLA_BLOCK)

    n = 2 * l
    max_decay = math.log(HY_TARGET) / HY_FAST_DECAY
    min_decay = math.log(HY_TARGET) / HY_SLOW_DECAY
    deltas = jnp.asarray(np.abs(np.linspace(min_decay, max_decay, hy_w, dtype=np.float32))[None])
    ffn = hy_w1.shape[2]
    kern = _hyfilt(l, hy_w,
                   _pad_to(hy_w1[0], LANES, LANES), _pad_to(hy_b1[0][None], 1, LANES),
                   _pad_to(hy_freq[0, 0][None], 1, LANES),
                   _pad_to(hy_w2[0], LANES, LANES), _pad_to(hy_b2[0][None], 1, LANES),
                   _pad_to(hy_freq[0, 1][None], 1, LANES),
                   jnp.pad(hy_w3[0], ((0, LANES - ffn), (0, 0))), deltas)
    tabs = _fft_tables(n, FFT_N2)
    spec = _hyspec(kern, tabs[1], tabs[2], FFT_N2)
    cw, cb = hy_conv_w[0], hy_conv_b[0][None]
    z1 = _hyconv(u, p_hy, u, p_hy + hy_w, cw, cb, 0, hy_w, spec, 0, hy_bias[0, 0][None], tabs, True)
    y_hy = _hyconv(z1, 0, u, p_hy + 2 * hy_w, cw, cb, 0, 2 * hy_w, spec, 1, hy_bias[0, 1][None], tabs, False)

    x1, xn2, logits_t = _mix(x, o_f, o_b, u, p_g, p_gate, y_hy, gla_norm_w[0][None],
                             proj_hy[0].astype(BF16), proj_gla[0].astype(BF16), w_out[0].astype(BF16),
                             g1, norm2_w[0][None], sh2, sc2, router_w[0].T, 512)
    comb = _select(logits_t, router_bias[0].reshape(N_GROUPS, N_EXPERTS // N_GROUPS, 1), 512)
    out = _moe(xn2.reshape(b * l, d), comb, exp_w1[0].astype(BF16), exp_w3[0].astype(BF16),
               exp_w2[0].astype(BF16), sh_w1[0].astype(BF16), sh_w3[0].astype(BF16), sh_w2[0].astype(BF16),
               x1.reshape(b * l, d), g2, final_norm_w[None], l, 1024)
    return out.reshape(b, l, d)
```

```python
import functools
import math

import jax
import jax.numpy as jnp
import numpy as np
from jax import lax
from jax.experimental import pallas as pl
from jax.experimental.pallas import tpu as pltpu

F32 = jnp.float32
BF16 = jnp.bfloat16
HIGHEST = lax.Precision.HIGHEST

GRID_W = 64
EPS = 1e-6
HY_ORDER = 2
HY_BANDS = 16
HY_FAST_DECAY = 0.3
HY_SLOW_DECAY = 1.5
HY_TARGET = 1e-2
GLA_HEADS = 4
GLA_RANK = 16
GLA_TAU = 16.0
N_EXPERTS = 64
N_GROUPS = 8
TOPK_GROUPS = 4
TOP_K = 8
ROUTED_SCALE = 2.5

LANES = 128
V7X_VMEM_BYTES = 64 * 1024 * 1024
VMEM_CAP_BYTES = 56 * 1024 * 1024

GLA_CHUNK = 256
GLA_BLOCK = 256
FFT_N2 = 64
STRIDE_PAD = 8
STAGE_UNROLL = 8
ROW_UNROLL = 2
EXP_PER_STEP = 4


def _params(sem, vmem_bytes):
    limit = int(min(VMEM_CAP_BYTES, max(16 * 1024 * 1024, vmem_bytes * 5 // 4 + (2 << 20))))
    return pltpu.CompilerParams(dimension_semantics=sem, vmem_limit_bytes=limit)


def _nt(a, b, **kw):
    return lax.dot_general(a, b, (((1,), (1,)), ((), ())), preferred_element_type=F32, **kw)


def _tn(a, b):
    return lax.dot_general(a, b, (((0,), (0,)), ((), ())), preferred_element_type=F32)


def _dot(a, b, **kw):
    return jnp.dot(a, b, preferred_element_type=F32, **kw)


def _silu(x):
    return x * jax.nn.sigmoid(x)


def _mods_body(c_ref, w_ref, b_ref, o_ref):
    o_ref[...] = _dot(_silu(c_ref[...]), w_ref[...], precision=HIGHEST) + b_ref[...]


def _mods(cc, w, b):
    rows, d = cc.shape
    n = w.shape[1]
    tn = n // 4
    return pl.pallas_call(
        _mods_body,
        grid=(n // tn,),
        in_specs=[pl.BlockSpec((rows, d), lambda j: (0, 0)),
                  pl.BlockSpec((d, tn), lambda j: (0, j)),
                  pl.BlockSpec((1, tn), lambda j: (0, j))],
        out_specs=pl.BlockSpec((rows, tn), lambda j: (0, j)),
        out_shape=jax.ShapeDtypeStruct((rows, n), F32),
        compiler_params=_params(("arbitrary",), 2 * d * tn * 4),
        name="mods",
    )(cc, w, b)


def _norm_mod(x, w, shift, scale):
    ms = jnp.mean(x * x, axis=-1, keepdims=True)
    return (x * lax.rsqrt(ms + EPS) * w) * (1.0 + scale) + shift


def _inproj_body(x_ref, nw_ref, sh_ref, sc_ref, w_ref, o_ref, xn_ref):
    @pl.when(pl.program_id(2) == 0)
    def _():
        xn_ref[...] = _norm_mod(x_ref[0], nw_ref[...], sh_ref[0], sc_ref[0]).astype(BF16)

    o_ref[0] = _dot(xn_ref[...], w_ref[...]).astype(BF16)


def _inproj(x, nw, shift, scale, w, tm, tn):
    b, l, d = x.shape
    n = w.shape[1]
    vm = 2 * tm * d * 4 + 2 * d * tn * 2 + 2 * tm * tn * 2 + tm * d * 2
    return pl.pallas_call(
        _inproj_body,
        grid=(b, l // tm, n // tn),
        in_specs=[pl.BlockSpec((1, tm, d), lambda bi, i, j: (bi, i, 0)),
                  pl.BlockSpec((1, d), lambda bi, i, j: (0, 0)),
                  pl.BlockSpec((1, 1, d), lambda bi, i, j: (bi, 0, 0)),
                  pl.BlockSpec((1, 1, d), lambda bi, i, j: (bi, 0, 0)),
                  pl.BlockSpec((d, tn), lambda bi, i, j: (0, j))],
        out_specs=pl.BlockSpec((1, tm, tn), lambda bi, i, j: (bi, i, j)),
        out_shape=jax.ShapeDtypeStruct((b, l, n), BF16),
        scratch_shapes=[pltpu.VMEM((tm, d), BF16)],
        compiler_params=_params(("arbitrary", "arbitrary", "arbitrary"), vm),
        name="inproj",
    )(x, nw, shift, scale, w)


def _gla_tables(c, inclusive, flip):
    idx = np.arange(c)
    i = idx[:, None]
    x = idx[None, :]
    blocks = [(x <= i) if inclusive else (x < i), x > i]
    masks = []
    h = c // 2
    while h >= 1:
        mid = (idx // (2 * h)) * (2 * h) + h
        mi = mid[:, None]
        hi = i if inclusive else i - 1
        blocks.append(((i >= mi) & (x >= mi) & (x <= hi)) | ((i < mi) & (x > i) & (x <= mi - 1)))
        same = (idx[:, None] // (2 * h)) == (idx[None, :] // (2 * h))
        masks.append(same & (idx[:, None] >= mi) & (idx[None, :] < mid[None, :]))
        h //= 2
    masks.append(np.eye(c, dtype=bool))
    if flip:
        blocks = [b[::-1, ::-1] for b in blocks]
        masks = [m[::-1, ::-1] for m in masks]
    lall = np.concatenate(blocks + [np.ones((8, c), bool)], axis=0)
    return lall.astype(np.float32), np.stack(masks).astype(np.float32)


def _gla_chunk(q, k, v, a, wa, ba, lall, masks_ref, st_ref, inclusive, q_scale):
    c, dk = k.shape
    n_levels = int(math.log2(c))
    xg = _dot(a, wa) + ba
    g = (jnp.minimum(xg, 0.0) - jnp.log(1.0 + jnp.exp(-jnp.abs(xg)))) * (1.0 / GLA_TAU)
    g_hi = g.astype(BF16)
    g_lo = (g - g_hi.astype(F32)).astype(BF16)
    e2 = _dot(lall, jnp.concatenate([g_hi, g_lo], axis=1))
    ex = jnp.exp(jnp.minimum(e2[:, :dk] + e2[:, dk:], 0.0))

    kf = k.astype(F32)
    st = st_ref[...]
    k1 = (kf * ex[c:2 * c]).astype(BF16)
    st_ref[...] = st * ex[(2 + n_levels) * c:(2 + n_levels) * c + 1] + _tn(v, k1)
    if q is None:
        return None
    qf = q.astype(F32) * q_scale
    o = _nt((qf * ex[0:c]).astype(BF16), st.astype(BF16))
    attn = jnp.zeros((c, c), F32)
    for lv in range(n_levels):
        ex_l = ex[(2 + lv) * c:(3 + lv) * c]
        attn = attn + _nt((qf * ex_l).astype(BF16), (kf * ex_l).astype(BF16)) * masks_ref[lv]
    if inclusive:
        attn = attn + _nt(qf.astype(BF16), k) * masks_ref[n_levels]
    return o + _dot(attn.astype(BF16), v)


def _gla_body(*refs, with_q, n_sub, chunk, q_scale):
    if with_q:
        (kf_ref, vf_ref, af_ref, qf_ref, kb_ref, vb_ref, ab_ref, qb_ref, waf_ref, wab_ref, baf_ref, bab_ref,
         lf_ref, lb_ref, mf_ref, mb_ref, s0f_ref, s0b_ref, of_ref, ob_ref, sf_ref, sb_ref, stf_ref, stb_ref) = refs
    else:
        (kf_ref, vf_ref, af_ref, kb_ref, vb_ref, ab_ref, waf_ref, wab_ref, baf_ref, bab_ref,
         lf_ref, lb_ref, mf_ref, mb_ref, s0f_ref, s0b_ref, sf_ref, sb_ref, stf_ref, stb_ref) = refs
        qf_ref = qb_ref = of_ref = ob_ref = None

    @pl.when(pl.program_id(2) == 0)
    def _():
        stf_ref[...] = s0f_ref[0, 0]
        stb_ref[...] = s0b_ref[0, 0]

    for s in range(n_sub):
        sl = slice(s * chunk, (s + 1) * chunk)
        o = _gla_chunk(None if qf_ref is None else qf_ref[0, sl, :], kf_ref[0, sl, :], vf_ref[0, sl, :],
                       af_ref[0, sl, :], waf_ref[0], baf_ref[0], lf_ref[...], mf_ref, stf_ref, True, q_scale)
        if with_q:
            of_ref[0, sl, :] = o.astype(BF16)
    for s in reversed(range(n_sub)):
        sl = slice(s * chunk, (s + 1) * chunk)
        o = _gla_chunk(None if qb_ref is None else qb_ref[0, sl, :], kb_ref[0, sl, :], vb_ref[0, sl, :],
                       ab_ref[0, sl, :], wab_ref[0], bab_ref[0], lb_ref[...], mb_ref, stb_ref, False, q_scale)
        if with_q:
            ob_ref[0, sl, :] = o.astype(BF16)
    sf_ref[0, 0] = stf_ref[...]
    sb_ref[0, 0] = stb_ref[...]


def _gla(u, cols, waf, wab, baf, bab, s0f, s0b, dk, dv, with_q, tb):
    b, l, _ = u.shape
    h = GLA_HEADS
    nb = l // tb
    n_sub = tb // GLA_CHUNK
    lf, mf = _gla_tables(GLA_CHUNK, True, False)
    lb, mb = _gla_tables(GLA_CHUNK, False, True)
    lf, lb = jnp.asarray(lf, BF16), jnp.asarray(lb, BF16)
    mf, mb = jnp.asarray(mf), jnp.asarray(mb)
    kc, vc, ac, qc = cols

    def seq_specs(rev):
        def blk(i):
            return (nb - 1 - i) if rev else i
        specs = [pl.BlockSpec((1, tb, dk), lambda bi, hi, i: (bi, blk(i), kc // dk + hi)),
                 pl.BlockSpec((1, tb, dv), lambda bi, hi, i: (bi, blk(i), vc // dv + hi)),
                 pl.BlockSpec((1, tb, LANES), lambda bi, hi, i: (bi, blk(i), ac // LANES))]
        if with_q:
            specs.append(pl.BlockSpec((1, tb, dk), lambda bi, hi, i: (bi, blk(i), qc // dk + hi)))
        return specs

    def const_spec(shape):
        nd = len(shape)
        return pl.BlockSpec(shape, lambda bi, hi, i: (0,) * nd)

    head_w = pl.BlockSpec((1, LANES, dk), lambda bi, hi, i: (hi, 0, 0))
    head_b = pl.BlockSpec((1, 1, dk), lambda bi, hi, i: (hi, 0, 0))
    st_spec = pl.BlockSpec((1, 1, dv, dk), lambda bi, hi, i: (bi, hi, 0, 0))
    in_specs = (seq_specs(False) + seq_specs(True) + [head_w, head_w, head_b, head_b,
                const_spec(lf.shape), const_spec(lb.shape), const_spec(mf.shape), const_spec(mb.shape),
                st_spec, st_spec])
    st_shape = jax.ShapeDtypeStruct((b, h, dv, dk), F32)
    if with_q:
        o_shape = jax.ShapeDtypeStruct((b, l, h * dv), BF16)
        out_shape = (o_shape, o_shape, st_shape, st_shape)
        out_specs = (pl.BlockSpec((1, tb, dv), lambda bi, hi, i: (bi, i, hi)),
                     pl.BlockSpec((1, tb, dv), lambda bi, hi, i: (bi, nb - 1 - i, hi)),
                     st_spec, st_spec)
        args = (u,) * 8
    else:
        out_shape = (st_shape, st_shape)
        out_specs = (st_spec, st_spec)
        args = (u,) * 6
    vm = 4 * tb * (2 * dk + dv + LANES) * 2 * 2 + 8 * dv * dk * 4 + 4 * tb * dv * 2 + (4 << 20)
    outs = pl.pallas_call(
        functools.partial(_gla_body, with_q=with_q, n_sub=n_sub, chunk=GLA_CHUNK, q_scale=dk ** -0.5),
        grid=(b, h, nb),
        in_specs=in_specs,
        out_specs=out_specs,
        out_shape=out_shape,
        scratch_shapes=[pltpu.VMEM((dv, dk), F32), pltpu.VMEM((dv, dk), F32)],
        compiler_params=_params(("arbitrary", "arbitrary", "arbitrary"), vm),
        name="gla" if with_q else "gla_ctx",
    )(*args, waf, wab, baf, bab, lf, lb, mf, mb, s0f, s0b)
    if with_q:
        return outs
    return None, None, outs[0], outs[1]


def _hy_tables(l):
    t = np.linspace(0.0, 1.0, l, dtype=np.float32).astype(np.float64)[:, None]
    w = 2.0 * math.pi * np.arange(l, dtype=np.float64)[:, None] / l
    f = np.linspace(1e-4, HY_BANDS - 1, HY_BANDS, dtype=np.float32).astype(np.float64)[None, :]
    z = np.concatenate([t, np.cos(f * w), -np.sin(f * w)], axis=-1)
    rev = (l - np.arange(l)) % l

    def pad(a):
        out = np.zeros((l, LANES), np.float32)
        out[:, :a.shape[1]] = a
        return out

    tt = np.broadcast_to(t, (l, LANES)).astype(np.float32)
    return pad(z), pad(z[rev]), tt, np.ascontiguousarray(tt[rev])


def _hyfilt_body(z1_ref, z2_ref, t1_ref, t2_ref, w1_ref, b1_ref, f1_ref, w2_ref, b2_ref, f2_ref,
                 w3f_ref, w3b_ref, dl_ref, o_ref, ha_ref, hb_ref):
    l = z1_ref.shape[0]

    @pl.when((pl.program_id(0) == 0) & (pl.program_id(1) == 0))
    def _():
        for z_ref, h_ref in ((z1_ref, ha_ref), (z2_ref, hb_ref)):
            h = jnp.sin(f1_ref[...] * (_dot(z_ref[...], w1_ref[...], precision=HIGHEST) + b1_ref[...]))
            h_ref[...] = jnp.sin(f2_ref[...] * (_dot(h, w2_ref[...], precision=HIGHEST) + b2_ref[...]))

    dl = dl_ref[...]
    hf = _dot(ha_ref[...], w3f_ref[...], precision=HIGHEST) * jnp.exp(-t1_ref[...] * dl)
    hb = _dot(hb_ref[...], w3b_ref[...], precision=HIGHEST) * jnp.exp(-t2_ref[...] * dl)
    row = lax.broadcasted_iota(jnp.int32, hb.shape, 0)
    hb = jnp.where(row == 0, 0.0, hb)
    ss = jnp.sum(hf * hf, axis=0, keepdims=True) + jnp.sum(hb * hb, axis=0, keepdims=True)
    scale = lax.rsqrt(ss)
    o_ref[0, 0:l, :] = hf * scale
    o_ref[0, l:2 * l, :] = hb * scale


def _hyfilt(l, c, w1, b1, f1, w2, b2, f2, w3, deltas):
    z1, z2, t1, t2 = (jnp.asarray(a) for a in _hy_tables(l))
    ncb = c // LANES
    tab = pl.BlockSpec((l, LANES), lambda o, j: (0, 0))
    sq = pl.BlockSpec((LANES, LANES), lambda o, j: (0, 0))
    row = pl.BlockSpec((1, LANES), lambda o, j: (0, 0))
    return pl.pallas_call(
        _hyfilt_body,
        grid=(HY_ORDER, ncb),
        in_specs=[tab, tab, tab, tab, sq, row, row, sq, row, row,
                  pl.BlockSpec((LANES, LANES), lambda o, j: (0, 2 * o * ncb + j)),
                  pl.BlockSpec((LANES, LANES), lambda o, j: (0, (2 * o + 1) * ncb + j)),
                  pl.BlockSpec((1, LANES), lambda o, j: (0, j))],
        out_specs=pl.BlockSpec((1, 2 * l, LANES), lambda o, j: (o, 0, j)),
        out_shape=jax.ShapeDtypeStruct((HY_ORDER, 2 * l, c), F32),
        scratch_shapes=[pltpu.VMEM((l, LANES), F32), pltpu.VMEM((l, LANES), F32)],
        compiler_params=_params(("arbitrary", "arbitrary"), 24 * l * LANES * 4),
        name="hyfilt",
    )(z1, z2, t1, t2, w1, b1, f1, w2, b2, f2, w3, w3, deltas)


def _fft_tables(n, n2):
    n1 = n // n2
    h = n1 // 2
    k1 = np.arange(n1)[:, None]
    a = 2.0 * math.pi * k1 * np.arange(h)[None, :] / n1
    c, s = np.cos(a), np.sin(a)
    f1c = np.block([[c, s], [-s, c]])
    a = 2.0 * math.pi * k1 * np.arange(n1)[None, :] / n1
    f1r = np.concatenate([np.cos(a), -np.sin(a)], axis=0)
    kk = np.arange(n1)[:, None, None] + n1 * np.arange(n2)[None, :, None]
    a = 2.0 * math.pi * kk * np.arange(n2)[None, None, :] / n
    c, s = np.cos(a), np.sin(a)
    gf = np.concatenate([np.concatenate([c, s], axis=2), np.concatenate([-s, c], axis=2)], axis=1)
    a = 2.0 * math.pi * np.arange(h)[:, None] * np.arange(n1)[None, :] / n1
    c, s = np.cos(a), np.sin(a)
    if1 = np.block([[c, -s], [s, c]])
    return tuple(jnp.asarray(m, BF16) for m in (f1c, f1r, gf, if1))


def _rows8(start, size):
    return pl.ds(pl.multiple_of(start, 8), size)


def _hyspec_body(k_ref, f1_ref, gf_ref, o_ref, as_ref, *, n, n2, ap):
    n1 = n // n2

    def stage1(j, carry):
        r = k_ref[0, pl.ds(j, n1, stride=n2), :].astype(BF16)
        as_ref[_rows8(j * ap, 2 * n1), :] = _dot(f1_ref[...], r)
        return carry

    lax.fori_loop(0, n2, stage1, 0, unroll=STAGE_UNROLL)

    def stage2(k1, carry):
        r = jnp.concatenate([as_ref[pl.ds(k1, n2, stride=ap), :],
                             as_ref[pl.ds(n1 + k1, n2, stride=ap), :]], axis=0).astype(BF16)
        o_ref[0, k1] = (_dot(gf_ref[k1], r) * (1.0 / n)).astype(BF16)
        return carry

    lax.fori_loop(0, n1, stage2, 0, unroll=STAGE_UNROLL)


def _hyspec(kern, f1r, gf, n2):
    order, n, c = kern.shape
    n1 = n // n2
    ap = 2 * n1 + STRIDE_PAD
    return pl.pallas_call(
        functools.partial(_hyspec_body, n=n, n2=n2, ap=ap),
        grid=(order, c // LANES),
        in_specs=[pl.BlockSpec((1, n, LANES), lambda o, j: (o, 0, j)),
                  pl.BlockSpec(f1r.shape, lambda o, j: (0, 0)),
                  pl.BlockSpec(gf.shape, lambda o, j: (0, 0, 0))],
        out_specs=pl.BlockSpec((1, n1, 2 * n2, LANES), lambda o, j: (o, 0, 0, j)),
        out_shape=jax.ShapeDtypeStruct((order, n1, 2 * n2, c), BF16),
        scratch_shapes=[pltpu.VMEM((n2 * ap, LANES), F32)],
        compiler_params=_params(("arbitrary", "arbitrary"),
                                2 * n * LANES * 4 + 2 * gf.size * 2 + n2 * ap * LANES * 4 + 2 * n * LANES * 2),
        name="hyspec",
    )(kern, f1r, gf)


def _short_conv(u, w_ref, b_ref):
    r = u.shape[0]
    row = lax.broadcasted_iota(jnp.int32, u.shape, 0)
    up = jnp.where(row == 0, 0.0, pltpu.roll(u, 1, axis=0))
    dn = jnp.where(row == r - 1, 0.0, pltpu.roll(u, r - 1, axis=0))
    return up * w_ref[0:1, :] + u * w_ref[1:2, :] + dn * w_ref[2:3, :] + b_ref[...]


def _hyconv_body(z_ref, g_ref, zw_ref, zb_ref, gw_ref, gb_ref, sp_ref, hb_ref, f1_ref, gf_ref, if1_ref,
                 o_ref, x_ref, as_ref, bs_ref, y_ref, *, conv_z, n2, xp, ap):
    n1h = z_ref.shape[1] // n2
    n1 = 2 * n1h
    half = n1h * xp

    def load_z(p, i):
        u = z_ref[p, pl.ds(pl.multiple_of(i * n2, n2), n2), :].astype(F32)
        return _short_conv(u, zw_ref, zb_ref) if conv_z else u

    def fill(i, carry):
        for p in range(2):
            x_ref[_rows8(p * half + i * xp, n2), :] = load_z(p, i)
        return carry

    lax.fori_loop(0, n1h, fill, 0, unroll=ROW_UNROLL)

    def stage1(j, carry):
        r = jnp.concatenate([x_ref[pl.ds(j, n1h, stride=xp), :],
                             x_ref[pl.ds(half + j, n1h, stride=xp), :]], axis=0).astype(BF16)
        as_ref[_rows8(j * ap, 2 * n1), :] = _dot(f1_ref[...], r)
        return carry

    lax.fori_loop(0, n2, stage1, 0, unroll=STAGE_UNROLL)

    def stage2(k1, carry):
        r = jnp.concatenate([as_ref[pl.ds(k1, n2, stride=ap), :],
                             as_ref[pl.ds(n1 + k1, n2, stride=ap), :]], axis=0).astype(BF16)
        gk = gf_ref[k1]
        xk = _dot(gk, r)
        xr, xi = xk[0:n2], xk[n2:2 * n2]
        sp = sp_ref[0, k1].astype(F32)
        sr, si = sp[0:n2], sp[n2:2 * n2]
        yk = jnp.concatenate([xr * sr - xi * si, xr * si + xi * sr], axis=0).astype(BF16)
        bk = _tn(gk, yk)
        bs_ref[pl.ds(k1, n2, stride=ap), :] = bk[0:n2]
        bs_ref[pl.ds(n1 + k1, n2, stride=ap), :] = bk[n2:2 * n2]
        return carry

    lax.fori_loop(0, n1, stage2, 0, unroll=2 * STAGE_UNROLL)

    def stage3(j, carry):
        yn = _dot(if1_ref[...], bs_ref[_rows8(j * ap, 2 * n1), :].astype(BF16))
        y_ref[pl.ds(j, n1h, stride=xp), :] = yn[0:n1h]
        y_ref[pl.ds(half + j, n1h, stride=xp), :] = yn[n1h:n1]
        return carry

    lax.fori_loop(0, n2, stage3, 0, unroll=STAGE_UNROLL)

    def finish(i, carry):
        rows = pl.ds(pl.multiple_of(i * n2, n2), n2)
        for p in range(2):
            gate = _short_conv(g_ref[p, rows, :].astype(F32), gw_ref, gb_ref)
            z = x_ref[_rows8(p * half + i * xp, n2), :]
            y = y_ref[_rows8(p * half + i * xp, n2), :]
            o_ref[p, rows, :] = (gate * (y + z * hb_ref[...])).astype(BF16)
        return carry

    lax.fori_loop(0, n1h, finish, 0, unroll=ROW_UNROLL)


def _hyconv(z, z_col, g, g_col, conv_w, conv_b, zw_col, gw_col, spec, order, hy_bias, tabs, conv_z):
    b, l, _ = z.shape
    f1c, _, gf, if1 = tabs
    n1, n2x2 = gf.shape[0], gf.shape[1]
    n2 = n2x2 // 2
    n1h = n1 // 2
    c = spec.shape[-1]
    ncb = c // LANES
    xp = n2 + STRIDE_PAD
    ap = 2 * n1 + STRIDE_PAD
    vm = (2 * 2 * 2 * l * LANES * 2 + 2 * 2 * l * LANES * 2 + 2 * n1 * n2x2 * LANES * 2
          + 2 * gf.size * 2 + 2 * 2 * n1h * xp * LANES * 4 + 2 * n2 * ap * LANES * 4)
    return pl.pallas_call(
        functools.partial(_hyconv_body, conv_z=conv_z, n2=n2, xp=xp, ap=ap),
        grid=(ncb, b // 2),
        in_specs=[pl.BlockSpec((2, l, LANES), lambda j, p: (p, 0, z_col // LANES + j)),
                  pl.BlockSpec((2, l, LANES), lambda j, p: (p, 0, g_col // LANES + j)),
                  pl.BlockSpec((3, LANES), lambda j, p: (0, zw_col // LANES + j)),
                  pl.BlockSpec((1, LANES), lambda j, p: (0, zw_col // LANES + j)),
                  pl.BlockSpec((3, LANES), lambda j, p: (0, gw_col // LANES + j)),
                  pl.BlockSpec((1, LANES), lambda j, p: (0, gw_col // LANES + j)),
                  pl.BlockSpec((1, n1, n2x2, LANES), lambda j, p: (order, 0, 0, j)),
                  pl.BlockSpec((1, LANES), lambda j, p: (0, j)),
                  pl.BlockSpec(f1c.shape, lambda j, p: (0, 0)),
                  pl.BlockSpec(gf.shape, lambda j, p: (0, 0, 0)),
                  pl.BlockSpec(if1.shape, lambda j, p: (0, 0))],
        out_specs=pl.BlockSpec((2, l, LANES), lambda j, p: (p, 0, j)),
        out_shape=jax.ShapeDtypeStruct((b, l, c), BF16),
        scratch_shapes=[pltpu.VMEM((2 * n1h * xp, LANES), F32),
                        pltpu.VMEM((n2 * ap, LANES), F32),
                        pltpu.VMEM((n2 * ap, LANES), F32),
                        pltpu.VMEM((2 * n1h * xp, LANES), F32)],
        compiler_params=_params(("arbitrary", "arbitrary"), vm),
        name="hyconv%d" % order,
    )(z, g, conv_w, conv_b, conv_w, conv_b, spec, hy_bias, f1c, gf, if1)


def _mix_body(x_ref, of_ref, ob_ref, ug_ref, ugate_ref, yhy_ref, gnw_ref, phy_ref, pgla_ref, wout_ref,
              g1_ref, n2w_ref, sh2_ref, sc2_ref, rwt_ref, x1_ref, xn2_ref, lg_ref, *, heads):
    d = x_ref.shape[2]
    o = of_ref[0].astype(F32) + ob_ref[0].astype(F32)
    dv = o.shape[1] // heads
    parts = []
    for h in range(heads):
        seg = o[:, h * dv:(h + 1) * dv]
        parts.append(seg * lax.rsqrt(jnp.mean(seg * seg, axis=-1, keepdims=True) + EPS))
    y_gla = jnp.concatenate(parts, axis=1) * gnw_ref[...] * _silu(ug_ref[0].astype(F32))
    gates = jax.nn.sigmoid(ugate_ref[0].astype(F32))
    merged = (gates[:, :d] * _dot(yhy_ref[0], phy_ref[...])
              + gates[:, d:] * _dot(y_gla.astype(BF16), pgla_ref[...]))
    x1 = x_ref[0] + g1_ref[0] * _dot(merged.astype(BF16), wout_ref[...])
    x1_ref[0] = x1
    xn2 = _norm_mod(x1, n2w_ref[...], sh2_ref[0], sc2_ref[0])
    xn2_ref[0] = xn2.astype(BF16)
    lg_ref[0] = _nt(rwt_ref[...], xn2, precision=HIGHEST)


def _mix(x, o_f, o_b, u, g_col, gate_col, y_hy, gnw, phy, pgla, wout, g1, n2w, sh2, sc2, rwt, tm):
    b, l, d = x.shape
    vw = o_f.shape[2]
    ne = rwt.shape[0]
    tok = lambda w: pl.BlockSpec((1, tm, w), lambda bi, i: (bi, i, 0))
    per_b = pl.BlockSpec((1, 1, d), lambda bi, i: (bi, 0, 0))
    const = lambda shape: pl.BlockSpec(shape, lambda bi, i: (0, 0))
    vm = (2 * tm * d * 4 * 2 + 2 * tm * (3 * vw + 2 * d + 2 * d) * 2 + 2 * 3 * d * d * 2 + 12 * tm * d * 4)
    return pl.pallas_call(
        functools.partial(_mix_body, heads=GLA_HEADS),
        grid=(b, l // tm),
        in_specs=[tok(d), tok(vw), tok(vw),
                  pl.BlockSpec((1, tm, vw), lambda bi, i: (bi, i, g_col // vw)),
                  pl.BlockSpec((1, tm, 2 * d), lambda bi, i: (bi, i, gate_col // (2 * d))),
                  tok(d), const((1, vw)), const((d, d)), const((vw, d)), const((d, d)),
                  per_b, const((1, d)), per_b, per_b, const((ne, d))],
        out_specs=(tok(d), tok(d), pl.BlockSpec((1, ne, tm), lambda bi, i: (bi, 0, i))),
        out_shape=(jax.ShapeDtypeStruct((b, l, d), F32), jax.ShapeDtypeStruct((b, l, d), BF16),
                   jax.ShapeDtypeStruct((b, ne, l), F32)),
        compiler_params=_params(("arbitrary", "arbitrary"), vm),
        name="mix",
    )(x, o_f, o_b, u, u, y_hy, gnw, phy, pgla, wout, g1, n2w, sh2, sc2, rwt)


def _select_body(lg_ref, bias_ref, o_ref):
    ne, tn = lg_ref.shape[1], lg_ref.shape[2]
    ng = N_GROUPS
    pg = ne // ng
    scores = jax.nn.sigmoid(lg_ref[0]).reshape(ng, pg, tn)
    sel = scores + bias_ref[...]
    ie = lax.broadcasted_iota(jnp.int32, sel.shape, 1)
    m1 = jnp.max(sel, axis=1, keepdims=True)
    i1 = jnp.min(jnp.where(sel == m1, ie, pg), axis=1, keepdims=True)
    m2 = jnp.max(jnp.where(ie == i1, -jnp.inf, sel), axis=1, keepdims=True)
    grp = m1 + m2
    ig = lax.broadcasted_iota(jnp.int32, grp.shape, 0)
    rank = jnp.zeros(grp.shape, jnp.int32)
    for g in range(ng):
        other = grp[g:g + 1]
        rank = rank + jnp.where((other > grp) | ((other == grp) & (g < ig)), 1, 0)
    cand = jnp.where(rank < TOPK_GROUPS, sel, -jnp.inf)
    flat = ig * pg + ie
    rank = jnp.zeros(sel.shape, jnp.int32)
    for g in range(ng):
        for e in range(pg):
            other = cand[g:g + 1, e:e + 1, :]
            rank = rank + jnp.where((other > cand) | ((other == cand) & (g * pg + e < flat)), 1, 0)
    w = jnp.where(rank < TOP_K, scores, 0.0)
    tot = jnp.sum(jnp.sum(w, axis=1, keepdims=True), axis=0, keepdims=True)
    comb = (w / tot * ROUTED_SCALE).reshape(ne, tn)
    o_ref[...] = jnp.concatenate([comb, jnp.zeros((LANES - ne, tn), F32)], axis=0).T


def _select(logits_t, bias, tn):
    b, ne, l = logits_t.shape
    nb = l // tn
    return pl.pallas_call(
        _select_body,
        grid=(b, nb),
        in_specs=[pl.BlockSpec((1, ne, tn), lambda bi, i: (bi, 0, i)),
                  pl.BlockSpec(bias.shape, lambda bi, i: (0, 0, 0))],
        out_specs=pl.BlockSpec((tn, LANES), lambda bi, i: (bi * nb + i, 0)),
        out_shape=jax.ShapeDtypeStruct((b * l, LANES), F32),
        compiler_params=_params(("arbitrary", "arbitrary"), 64 * ne * tn * 4),
        name="select",
    )(logits_t, bias)


def _moe_body(x_ref, comb_ref, pick_ref, w1_ref, w3_ref, w2_ref, sw1_ref, sw3_ref, sw2_ref,
              x1_ref, g2_ref, fnw_ref, o_ref, acc_ref):
    g = pl.program_id(1)
    x = x_ref[...]
    per = w1_ref.shape[0]

    @pl.when(g == 0)
    def _():
        hs = _silu(_dot(x, sw1_ref[...])) * _dot(x, sw3_ref[...])
        acc_ref[...] = _dot(hs.astype(BF16), sw2_ref[...])

    comb = comb_ref[...]
    c_hi = comb.astype(BF16)
    c_lo = (comb - c_hi.astype(F32)).astype(BF16)
    cg = _dot(c_hi, pick_ref[0]) + _dot(c_lo, pick_ref[0])
    acts = []
    for e in range(per):
        a = _silu(_dot(x, w1_ref[e])) * _dot(x, w3_ref[e])
        acts.append((a * cg[:, e:e + 1]).astype(BF16))
    w2 = w2_ref[...]
    acc_ref[...] += _dot(jnp.concatenate(acts, axis=1), w2.reshape(per * w2.shape[1], w2.shape[2]))

    @pl.when(g == pl.num_programs(1) - 1)
    def _():
        y = x1_ref[...] + g2_ref[0] * acc_ref[...]
        ms = jnp.mean(y * y, axis=-1, keepdims=True)
        o_ref[...] = y * lax.rsqrt(ms + EPS) * fnw_ref[...]


def _moe(xn2, comb, w1, w3, w2, sw1, sw3, sw2, x1, g2, fnw, tokens_per_batch, tm):
    t, d = xn2.shape
    ne, _, f = w1.shape
    per = EXP_PER_STEP
    ng = ne // per
    pick = np.zeros((ng, LANES, LANES), np.float32)
    for gi in range(ng):
        for e in range(per):
            pick[gi, gi * per + e, e] = 1.0
    pick = jnp.asarray(pick, BF16)
    bpb = tokens_per_batch // tm
    tok = lambda w: pl.BlockSpec((tm, w), lambda i, g: (i, 0))
    const = lambda shape: pl.BlockSpec(shape, lambda i, g: (0,) * len(shape))
    vm = (2 * tm * d * 2 + 2 * tm * LANES * 4 + 2 * 3 * per * d * f * 2 + 2 * 3 * d * f * 2
          + 2 * tm * d * 4 * 2 + tm * d * 4 + 6 * tm * per * f * 4)
    return pl.pallas_call(
        _moe_body,
        grid=(t // tm, ng),
        in_specs=[tok(d), tok(LANES),
                  pl.BlockSpec((1, LANES, LANES), lambda i, g: (g, 0, 0)),
                  pl.BlockSpec((per, d, f), lambda i, g: (g, 0, 0)),
                  pl.BlockSpec((per, d, f), lambda i, g: (g, 0, 0)),
                  pl.BlockSpec((per, f, d), lambda i, g: (g, 0, 0)),
                  const(sw1.shape), const(sw3.shape), const(sw2.shape),
                  tok(d),
                  pl.BlockSpec((1, 1, d), lambda i, g: (i // bpb, 0, 0)),
                  const((1, d))],
        out_specs=tok(d),
        out_shape=jax.ShapeDtypeStruct((t, d), F32),
        scratch_shapes=[pltpu.VMEM((tm, d), F32)],
        compiler_params=_params(("arbitrary", "arbitrary"), vm),
        name="moe",
    )(xn2, comb, pick, w1, w3, w2, sw1, sw3, sw2, x1, g2, fnw)


def _pad_to(a, rows, cols):
    return jnp.pad(a, ((0, rows - a.shape[0]), (0, cols - a.shape[1])))


def kernel(x, c, ctx, c_ctx, ada_w, ada_b, norm1_w, norm2_w, w_in, hy_conv_w, hy_conv_b, hy_w1, hy_b1, hy_freq, hy_w2, hy_b2, hy_w3, hy_bias, gla_a_w2, gla_a_b, gla_norm_w, proj_hy, proj_gla, w_out, router_w, router_bias, exp_w1, exp_w3, exp_w2, sh_w1, sh_w3, sh_w2, final_norm_w):
    b, l, d = x.shape
    assert ada_w.shape[0] == 1, "single-layer block"
    assert l // GRID_W * GRID_W == l and FFT_N2 == GRID_W
    heads = GLA_HEADS
    qk_w = d // 2
    dk = qk_w // heads
    v_w = d
    dv = v_w // heads
    a_w = 2 * GLA_RANK
    hy_w = d
    hy_cols = (HY_ORDER + 1) * hy_w

    rows = -(-(b + 1) // 8) * 8
    cc = jnp.zeros((rows, d), F32).at[:b].set(c).at[b].set(c_ctx)
    mods = _mods(cc, ada_w[0], ada_b[0][None])
    sh1, sc1, g1, sh2, sc2, g2 = [m[:b, None, :] for m in jnp.split(mods, 6, axis=-1)]
    csh1, csc1 = [jnp.broadcast_to(m[b][None, None, :], (b, 1, d)) for m in jnp.split(mods, 6, axis=-1)[:2]]

    w = w_in[0]
    o_a = qk_w + v_w
    o_q = o_a + a_w
    o_g = o_q + qk_w
    o_hy = o_g + v_w
    o_gate = o_hy + hy_cols
    w_k, w_v = w[:, :qk_w], w[:, qk_w:o_a]
    w_a = jnp.pad(w[:, o_a:o_q], ((0, 0), (0, LANES - a_w)))
    wp = jnp.concatenate([w[:, o_gate:], w[:, o_g:o_hy], w_v, w_k, w[:, o_q:o_g], w[:, o_hy:o_gate], w_a],
                         axis=1).astype(BF16)
    p_gate = 0
    p_g = 2 * d
    p_v = p_g + v_w
    p_k = p_v + v_w
    p_q = p_k + qk_w
    p_hy = p_q + qk_w
    p_a = p_hy + hy_cols
    n_all = p_a + LANES
    w_ctx = jnp.concatenate([w_k, w_v, w_a], axis=1).astype(BF16)
    cols_ctx = (0, qk_w, qk_w + v_w, None)
    cols = (p_k, p_v, p_a, p_q)

    nw1 = norm1_w[0][None]
    u_ctx = _inproj(ctx, nw1, csh1, csc1, w_ctx, ctx.shape[1], w_ctx.shape[1])
    u = _inproj(x, nw1, sh1, sc1, wp, 1024, n_all // 5)

    wa = gla_a_w2[0].reshape(2, GLA_RANK, heads, dk).transpose(0, 2, 1, 3)
    waf = jnp.pad(wa[0], ((0, 0), (0, LANES - GLA_RANK), (0, 0))).astype(BF16)
    wab = jnp.pad(wa[1], ((0, 0), (GLA_RANK, LANES - 2 * GLA_RANK), (0, 0))).astype(BF16)
    ba = gla_a_b[0].reshape(2, heads, 1, dk)
    zeros_state = jnp.zeros((b, heads, dv, dk), F32)
    _, _, s_f, s_b = _gla(u_ctx, cols_ctx, waf, wab, ba[0], ba[1], zeros_state, zeros_state, dk, dv, False,
                          ctx.shape[1])
    o_f, o_b, _, _ = _gla(u, cols, waf, wab, ba[0], ba[1], s_f, s_b, dk, dv, True, GLA_BLOCK)

    n = 2 * l
    max_decay = math.log(HY_TARGET) / HY_FAST_DECAY
    min_decay = math.log(HY_TARGET) / HY_SLOW_DECAY
    deltas = jnp.asarray(np.abs(np.linspace(min_decay, max_decay, hy_w, dtype=np.float32))[None])
    ffn = hy_w1.shape[2]
    kern = _hyfilt(l, hy_w,
                   _pad_to(hy_w1[0], LANES, LANES), _pad_to(hy_b1[0][None], 1, LANES),
                   _pad_to(hy_freq[0, 0][None], 1, LANES),
                   _pad_to(hy_w2[0], LANES, LANES), _pad_to(hy_b2[0][None], 1, LANES),
                   _pad_to(hy_freq[0, 1][None], 1, LANES),
                   jnp.pad(hy_w3[0], ((0, LANES - ffn), (0, 0))), deltas)
    tabs = _fft_tables(n, FFT_N2)
    spec = _hyspec(kern, tabs[1], tabs[2], FFT_N2)
    cw, cb = hy_conv_w[0], hy_conv_b[0][None]
    z1 = _hyconv(u, p_hy, u, p_hy + hy_w, cw, cb, 0, hy_w, spec, 0, hy_bias[0, 0][None], tabs, True)
    y_hy = _hyconv(z1, 0, u, p_hy + 2 * hy_w, cw, cb, 0, 2 * hy_w, spec, 1, hy_bias[0, 1][None], tabs, False)

    x1, xn2, logits_t = _mix(x, o_f, o_b, u, p_g, p_gate, y_hy, gla_norm_w[0][None],
                             proj_hy[0].astype(BF16), proj_gla[0].astype(BF16), w_out[0].astype(BF16),
                             g1, norm2_w[0][None], sh2, sc2, router_w[0].T, 512)
    comb = _select(logits_t, router_bias[0].reshape(N_GROUPS, N_EXPERTS // N_GROUPS, 1), 512)
    out = _moe(xn2.reshape(b * l, d), comb, exp_w1[0].astype(BF16), exp_w3[0].astype(BF16),
               exp_w2[0].astype(BF16), sh_w1[0].astype(BF16), sh_w3[0].astype(BF16), sh_w2[0].astype(BF16),
               x1.reshape(b * l, d), g2, final_norm_w[None], l, 1024)
    return out.reshape(b, l, d)
```

```python
import functools
import math

import jax
import jax.numpy as jnp
import numpy as np
from jax import lax
from jax.experimental import pallas as pl
from jax.experimental.pallas import tpu as pltpu

F32 = jnp.float32
BF16 = jnp.bfloat16
HIGHEST = lax.Precision.HIGHEST

GRID_W = 64
EPS = 1e-6
HY_ORDER = 2
HY_BANDS = 16
HY_FAST_DECAY = 0.3
HY_SLOW_DECAY = 1.5
HY_TARGET = 1e-2
GLA_HEADS = 4
GLA_RANK = 16
GLA_TAU = 16.0
N_EXPERTS = 64
N_GROUPS = 8
TOPK_GROUPS = 4
TOP_K = 8
ROUTED_SCALE = 2.5

LANES = 128
V7X_VMEM_BYTES = 64 * 1024 * 1024
VMEM_CAP_BYTES = 56 * 1024 * 1024

GLA_CHUNK = 256
GLA_BLOCK = 256
FFT_N2 = 64
STRIDE_PAD = 8
STAGE_UNROLL = 8
ROW_UNROLL = 2
EXP_PER_STEP = 4
MOE_SUB = 256
MOE_CAP = 64


def _params(sem, vmem_bytes):
    limit = int(min(VMEM_CAP_BYTES, max(16 * 1024 * 1024, vmem_bytes * 5 // 4 + (2 << 20))))
    return pltpu.CompilerParams(dimension_semantics=sem, vmem_limit_bytes=limit)


def _nt(a, b, **kw):
    return lax.dot_general(a, b, (((1,), (1,)), ((), ())), preferred_element_type=F32, **kw)


def _tn(a, b):
    return lax.dot_general(a, b, (((0,), (0,)), ((), ())), preferred_element_type=F32)


def _dot(a, b, **kw):
    return jnp.dot(a, b, preferred_element_type=F32, **kw)


def _silu(x):
    return x * jax.nn.sigmoid(x)


def _mods_body(c_ref, w_ref, b_ref, o_ref):
    o_ref[...] = _dot(_silu(c_ref[...]), w_ref[...], precision=HIGHEST) + b_ref[...]


def _mods(cc, w, b):
    rows, d = cc.shape
    n = w.shape[1]
    tn = n // 4
    return pl.pallas_call(
        _mods_body,
        grid=(n // tn,),
        in_specs=[pl.BlockSpec((rows, d), lambda j: (0, 0)),
                  pl.BlockSpec((d, tn), lambda j: (0, j)),
                  pl.BlockSpec((1, tn), lambda j: (0, j))],
        out_specs=pl.BlockSpec((rows, tn), lambda j: (0, j)),
        out_shape=jax.ShapeDtypeStruct((rows, n), F32),
        compiler_params=_params(("arbitrary",), 2 * d * tn * 4),
        name="mods",
    )(cc, w, b)


def _norm_mod(x, w, shift, scale):
    ms = jnp.mean(x * x, axis=-1, keepdims=True)
    return (x * lax.rsqrt(ms + EPS) * w) * (1.0 + scale) + shift


def _inproj_body(x_ref, nw_ref, sh_ref, sc_ref, w_ref, o_ref, xn_ref):
    @pl.when(pl.program_id(2) == 0)
    def _():
        xn_ref[...] = _norm_mod(x_ref[0], nw_ref[...], sh_ref[0], sc_ref[0]).astype(BF16)

    o_ref[0] = _dot(xn_ref[...], w_ref[...]).astype(BF16)


def _inproj(x, nw, shift, scale, w, tm, tn):
    b, l, d = x.shape
    n = w.shape[1]
    vm = 2 * tm * d * 4 + 2 * d * tn * 2 + 2 * tm * tn * 2 + tm * d * 2
    return pl.pallas_call(
        _inproj_body,
        grid=(b, l // tm, n // tn),
        in_specs=[pl.BlockSpec((1, tm, d), lambda bi, i, j: (bi, i, 0)),
                  pl.BlockSpec((1, d), lambda bi, i, j: (0, 0)),
                  pl.BlockSpec((1, 1, d), lambda bi, i, j: (bi, 0, 0)),
                  pl.BlockSpec((1, 1, d), lambda bi, i, j: (bi, 0, 0)),
                  pl.BlockSpec((d, tn), lambda bi, i, j: (0, j))],
        out_specs=pl.BlockSpec((1, tm, tn), lambda bi, i, j: (bi, i, j)),
        out_shape=jax.ShapeDtypeStruct((b, l, n), BF16),
        scratch_shapes=[pltpu.VMEM((tm, d), BF16)],
        compiler_params=_params(("arbitrary", "arbitrary", "arbitrary"), vm),
        name="inproj",
    )(x, nw, shift, scale, w)


def _gla_tables(c, inclusive, flip):
    idx = np.arange(c)
    i = idx[:, None]
    x = idx[None, :]
    blocks = [(x <= i) if inclusive else (x < i), x > i]
    masks = []
    h = c // 2
    while h >= 1:
        mid = (idx // (2 * h)) * (2 * h) + h
        mi = mid[:, None]
        hi = i if inclusive else i - 1
        blocks.append(((i >= mi) & (x >= mi) & (x <= hi)) | ((i < mi) & (x > i) & (x <= mi - 1)))
        same = (idx[:, None] // (2 * h)) == (idx[None, :] // (2 * h))
        masks.append(same & (idx[:, None] >= mi) & (idx[None, :] < mid[None, :]))
        h //= 2
    masks.append(np.eye(c, dtype=bool))
    if flip:
        blocks = [b[::-1, ::-1] for b in blocks]
        masks = [m[::-1, ::-1] for m in masks]
    lall = np.concatenate(blocks + [np.ones((8, c), bool)], axis=0)
    return lall.astype(np.float32), np.stack(masks).astype(np.float32)


def _gla_chunk(q, k, v, a, wa, ba, lall, masks_ref, st_ref, inclusive, q_scale):
    c, dk = k.shape
    n_levels = int(math.log2(c))
    xg = _dot(a, wa) + ba
    g = (jnp.minimum(xg, 0.0) - jnp.log(1.0 + jnp.exp(-jnp.abs(xg)))) * (1.0 / GLA_TAU)
    g_hi = g.astype(BF16)
    g_lo = (g - g_hi.astype(F32)).astype(BF16)
    e2 = _dot(lall, jnp.concatenate([g_hi, g_lo], axis=1))
    ex = jnp.exp(jnp.minimum(e2[:, :dk] + e2[:, dk:], 0.0))

    kf = k.astype(F32)
    st = st_ref[...]
    k1 = (kf * ex[c:2 * c]).astype(BF16)
    st_ref[...] = st * ex[(2 + n_levels) * c:(2 + n_levels) * c + 1] + _tn(v, k1)
    if q is None:
        return None
    qf = q.astype(F32) * q_scale
    o = _nt((qf * ex[0:c]).astype(BF16), st.astype(BF16))
    attn = jnp.zeros((c, c), F32)
    for lv in range(n_levels):
        ex_l = ex[(2 + lv) * c:(3 + lv) * c]
        attn = attn + _nt((qf * ex_l).astype(BF16), (kf * ex_l).astype(BF16)) * masks_ref[lv]
    if inclusive:
        attn = attn + _nt(qf.astype(BF16), k) * masks_ref[n_levels]
    return o + _dot(attn.astype(BF16), v)


def _gla_body(*refs, with_q, n_sub, chunk, q_scale):
    if with_q:
        (kf_ref, vf_ref, af_ref, qf_ref, kb_ref, vb_ref, ab_ref, qb_ref, waf_ref, wab_ref, baf_ref, bab_ref,
         lf_ref, lb_ref, mf_ref, mb_ref, s0f_ref, s0b_ref, of_ref, ob_ref, sf_ref, sb_ref, stf_ref, stb_ref) = refs
    else:
        (kf_ref, vf_ref, af_ref, kb_ref, vb_ref, ab_ref, waf_ref, wab_ref, baf_ref, bab_ref,
         lf_ref, lb_ref, mf_ref, mb_ref, s0f_ref, s0b_ref, sf_ref, sb_ref, stf_ref, stb_ref) = refs
        qf_ref = qb_ref = of_ref = ob_ref = None

    @pl.when(pl.program_id(2) == 0)
    def _():
        stf_ref[...] = s0f_ref[0, 0]
        stb_ref[...] = s0b_ref[0, 0]

    for s in range(n_sub):
        sl = slice(s * chunk, (s + 1) * chunk)
        o = _gla_chunk(None if qf_ref is None else qf_ref[0, sl, :], kf_ref[0, sl, :], vf_ref[0, sl, :],
                       af_ref[0, sl, :], waf_ref[0], baf_ref[0], lf_ref[...], mf_ref, stf_ref, True, q_scale)
        if with_q:
            of_ref[0, sl, :] = o.astype(BF16)
    for s in reversed(range(n_sub)):
        sl = slice(s * chunk, (s + 1) * chunk)
        o = _gla_chunk(None if qb_ref is None else qb_ref[0, sl, :], kb_ref[0, sl, :], vb_ref[0, sl, :],
                       ab_ref[0, sl, :], wab_ref[0], bab_ref[0], lb_ref[...], mb_ref, stb_ref, False, q_scale)
        if with_q:
            ob_ref[0, sl, :] = o.astype(BF16)
    sf_ref[0, 0] = stf_ref[...]
    sb_ref[0, 0] = stb_ref[...]


def _gla(u, cols, waf, wab, baf, bab, s0f, s0b, dk, dv, with_q, tb):
    b, l, _ = u.shape
    h = GLA_HEADS
    nb = l // tb
    n_sub = tb // GLA_CHUNK
    lf, mf = _gla_tables(GLA_CHUNK, True, False)
    lb, mb = _gla_tables(GLA_CHUNK, False, True)
    lf, lb = jnp.asarray(lf, BF16), jnp.asarray(lb, BF16)
    mf, mb = jnp.asarray(mf), jnp.asarray(mb)
    kc, vc, ac, qc = cols

    def seq_specs(rev):
        def blk(i):
            return (nb - 1 - i) if rev else i
        specs = [pl.BlockSpec((1, tb, dk), lambda bi, hi, i: (bi, blk(i), kc // dk + hi)),
                 pl.BlockSpec((1, tb, dv), lambda bi, hi, i: (bi, blk(i), vc // dv + hi)),
                 pl.BlockSpec((1, tb, LANES), lambda bi, hi, i: (bi, blk(i), ac // LANES))]
        if with_q:
            specs.append(pl.BlockSpec((1, tb, dk), lambda bi, hi, i: (bi, blk(i), qc // dk + hi)))
        return specs

    def const_spec(shape):
        nd = len(shape)
        return pl.BlockSpec(shape, lambda bi, hi, i: (0,) * nd)

    head_w = pl.BlockSpec((1, LANES, dk), lambda bi, hi, i: (hi, 0, 0))
    head_b = pl.BlockSpec((1, 1, dk), lambda bi, hi, i: (hi, 0, 0))
    st_spec = pl.BlockSpec((1, 1, dv, dk), lambda bi, hi, i: (bi, hi, 0, 0))
    in_specs = (seq_specs(False) + seq_specs(True) + [head_w, head_w, head_b, head_b,
                const_spec(lf.shape), const_spec(lb.shape), const_spec(mf.shape), const_spec(mb.shape),
                st_spec, st_spec])
    st_shape = jax.ShapeDtypeStruct((b, h, dv, dk), F32)
    if with_q:
        o_shape = jax.ShapeDtypeStruct((b, l, h * dv), BF16)
        out_shape = (o_shape, o_shape, st_shape, st_shape)
        out_specs = (pl.BlockSpec((1, tb, dv), lambda bi, hi, i: (bi, i, hi)),
                     pl.BlockSpec((1, tb, dv), lambda bi, hi, i: (bi, nb - 1 - i, hi)),
                     st_spec, st_spec)
        args = (u,) * 8
    else:
        out_shape = (st_shape, st_shape)
        out_specs = (st_spec, st_spec)
        args = (u,) * 6
    vm = 4 * tb * (2 * dk + dv + LANES) * 2 * 2 + 8 * dv * dk * 4 + 4 * tb * dv * 2 + (4 << 20)
    outs = pl.pallas_call(
        functools.partial(_gla_body, with_q=with_q, n_sub=n_sub, chunk=GLA_CHUNK, q_scale=dk ** -0.5),
        grid=(b, h, nb),
        in_specs=in_specs,
        out_specs=out_specs,
        out_shape=out_shape,
        scratch_shapes=[pltpu.VMEM((dv, dk), F32), pltpu.VMEM((dv, dk), F32)],
        compiler_params=_params(("arbitrary", "arbitrary", "arbitrary"), vm),
        name="gla" if with_q else "gla_ctx",
    )(*args, waf, wab, baf, bab, lf, lb, mf, mb, s0f, s0b)
    if with_q:
        return outs
    return None, None, outs[0], outs[1]


def _hy_tables(l):
    t = np.linspace(0.0, 1.0, l, dtype=np.float32).astype(np.float64)[:, None]
    w = 2.0 * math.pi * np.arange(l, dtype=np.float64)[:, None] / l
    f = np.linspace(1e-4, HY_BANDS - 1, HY_BANDS, dtype=np.float32).astype(np.float64)[None, :]
    z = np.concatenate([t, np.cos(f * w), -np.sin(f * w)], axis=-1)
    rev = (l - np.arange(l)) % l

    def pad(a):
        out = np.zeros((l, LANES), np.float32)
        out[:, :a.shape[1]] = a
        return out

    tt = np.broadcast_to(t, (l, LANES)).astype(np.float32)
    return pad(z), pad(z[rev]), tt, np.ascontiguousarray(tt[rev])


def _hyfilt_body(z1_ref, z2_ref, t1_ref, t2_ref, w1_ref, b1_ref, f1_ref, w2_ref, b2_ref, f2_ref,
                 w3f_ref, w3b_ref, dl_ref, o_ref, ha_ref, hb_ref):
    l = z1_ref.shape[0]

    @pl.when((pl.program_id(0) == 0) & (pl.program_id(1) == 0))
    def _():
        for z_ref, h_ref in ((z1_ref, ha_ref), (z2_ref, hb_ref)):
            h = jnp.sin(f1_ref[...] * (_dot(z_ref[...], w1_ref[...], precision=HIGHEST) + b1_ref[...]))
            h_ref[...] = jnp.sin(f2_ref[...] * (_dot(h, w2_ref[...], precision=HIGHEST) + b2_ref[...]))

    dl = dl_ref[...]
    hf = _dot(ha_ref[...], w3f_ref[...], precision=HIGHEST) * jnp.exp(-t1_ref[...] * dl)
    hb = _dot(hb_ref[...], w3b_ref[...], precision=HIGHEST) * jnp.exp(-t2_ref[...] * dl)
    row = lax.broadcasted_iota(jnp.int32, hb.shape, 0)
    hb = jnp.where(row == 0, 0.0, hb)
    ss = jnp.sum(hf * hf, axis=0, keepdims=True) + jnp.sum(hb * hb, axis=0, keepdims=True)
    scale = lax.rsqrt(ss)
    o_ref[0, 0:l, :] = hf * scale
    o_ref[0, l:2 * l, :] = hb * scale


def _hyfilt(l, c, w1, b1, f1, w2, b2, f2, w3, deltas):
    z1, z2, t1, t2 = (jnp.asarray(a) for a in _hy_tables(l))
    ncb = c // LANES
    tab = pl.BlockSpec((l, LANES), lambda o, j: (0, 0))
    sq = pl.BlockSpec((LANES, LANES), lambda o, j: (0, 0))
    row = pl.BlockSpec((1, LANES), lambda o, j: (0, 0))
    return pl.pallas_call(
        _hyfilt_body,
        grid=(HY_ORDER, ncb),
        in_specs=[tab, tab, tab, tab, sq, row, row, sq, row, row,
                  pl.BlockSpec((LANES, LANES), lambda o, j: (0, 2 * o * ncb + j)),
                  pl.BlockSpec((LANES, LANES), lambda o, j: (0, (2 * o + 1) * ncb + j)),
                  pl.BlockSpec((1, LANES), lambda o, j: (0, j))],
        out_specs=pl.BlockSpec((1, 2 * l, LANES), lambda o, j: (o, 0, j)),
        out_shape=jax.ShapeDtypeStruct((HY_ORDER, 2 * l, c), F32),
        scratch_shapes=[pltpu.VMEM((l, LANES), F32), pltpu.VMEM((l, LANES), F32)],
        compiler_params=_params(("arbitrary", "arbitrary"), 24 * l * LANES * 4),
        name="hyfilt",
    )(z1, z2, t1, t2, w1, b1, f1, w2, b2, f2, w3, w3, deltas)


def _fft_tables(n, n2):
    n1 = n // n2
    h = n1 // 2
    k1 = np.arange(n1)[:, None]
    a = 2.0 * math.pi * k1 * np.arange(h)[None, :] / n1
    c, s = np.cos(a), np.sin(a)
    f1c = np.block([[c, s], [-s, c]])
    a = 2.0 * math.pi * k1 * np.arange(n1)[None, :] / n1
    f1r = np.concatenate([np.cos(a), -np.sin(a)], axis=0)
    kk = np.arange(n1)[:, None, None] + n1 * np.arange(n2)[None, :, None]
    a = 2.0 * math.pi * kk * np.arange(n2)[None, None, :] / n
    c, s = np.cos(a), np.sin(a)
    gf = np.concatenate([np.concatenate([c, s], axis=2), np.concatenate([-s, c], axis=2)], axis=1)
    a = 2.0 * math.pi * np.arange(h)[:, None] * np.arange(n1)[None, :] / n1
    c, s = np.cos(a), np.sin(a)
    if1 = np.block([[c, -s], [s, c]])
    return tuple(jnp.asarray(m, BF16) for m in (f1c, f1r, gf, if1))


def _rows8(start, size):
    return pl.ds(pl.multiple_of(start, 8), size)


def _hyspec_body(k_ref, f1_ref, gf_ref, o_ref, as_ref, *, n, n2, ap):
    n1 = n // n2

    def stage1(j, carry):
        r = k_ref[0, pl.ds(j, n1, stride=n2), :].astype(BF16)
        as_ref[_rows8(j * ap, 2 * n1), :] = _dot(f1_ref[...], r)
        return carry

    lax.fori_loop(0, n2, stage1, 0, unroll=STAGE_UNROLL)

    def stage2(k1, carry):
        r = jnp.concatenate([as_ref[pl.ds(k1, n2, stride=ap), :],
                             as_ref[pl.ds(n1 + k1, n2, stride=ap), :]], axis=0).astype(BF16)
        o_ref[0, k1] = (_dot(gf_ref[k1], r) * (1.0 / n)).astype(BF16)
        return carry

    lax.fori_loop(0, n1, stage2, 0, unroll=STAGE_UNROLL)


def _hyspec(kern, f1r, gf, n2):
    order, n, c = kern.shape
    n1 = n // n2
    ap = 2 * n1 + STRIDE_PAD
    return pl.pallas_call(
        functools.partial(_hyspec_body, n=n, n2=n2, ap=ap),
        grid=(order, c // LANES),
        in_specs=[pl.BlockSpec((1, n, LANES), lambda o, j: (o, 0, j)),
                  pl.BlockSpec(f1r.shape, lambda o, j: (0, 0)),
                  pl.BlockSpec(gf.shape, lambda o, j: (0, 0, 0))],
        out_specs=pl.BlockSpec((1, n1, 2 * n2, LANES), lambda o, j: (o, 0, 0, j)),
        out_shape=jax.ShapeDtypeStruct((order, n1, 2 * n2, c), BF16),
        scratch_shapes=[pltpu.VMEM((n2 * ap, LANES), F32)],
        compiler_params=_params(("arbitrary", "arbitrary"),
                                2 * n * LANES * 4 + 2 * gf.size * 2 + n2 * ap * LANES * 4 + 2 * n * LANES * 2),
        name="hyspec",
    )(kern, f1r, gf)


def _short_conv(u, w_ref, b_ref):
    r = u.shape[0]
    row = lax.broadcasted_iota(jnp.int32, u.shape, 0)
    up = jnp.where(row == 0, 0.0, pltpu.roll(u, 1, axis=0))
    dn = jnp.where(row == r - 1, 0.0, pltpu.roll(u, r - 1, axis=0))
    return up * w_ref[0:1, :] + u * w_ref[1:2, :] + dn * w_ref[2:3, :] + b_ref[...]


def _hyconv_body(z_ref, g_ref, zw_ref, zb_ref, gw_ref, gb_ref, sp_ref, hb_ref, f1_ref, gf_ref, if1_ref,
                 o_ref, x_ref, as_ref, bs_ref, y_ref, *, conv_z, n2, xp, ap):
    n1h = z_ref.shape[1] // n2
    n1 = 2 * n1h
    half = n1h * xp

    def load_z(p, i):
        u = z_ref[p, pl.ds(pl.multiple_of(i * n2, n2), n2), :].astype(F32)
        return _short_conv(u, zw_ref, zb_ref) if conv_z else u

    def fill(i, carry):
        for p in range(2):
            x_ref[_rows8(p * half + i * xp, n2), :] = load_z(p, i)
        return carry

    lax.fori_loop(0, n1h, fill, 0, unroll=ROW_UNROLL)

    def stage1(j, carry):
        r = jnp.concatenate([x_ref[pl.ds(j, n1h, stride=xp), :],
                             x_ref[pl.ds(half + j, n1h, stride=xp), :]], axis=0).astype(BF16)
        as_ref[_rows8(j * ap, 2 * n1), :] = _dot(f1_ref[...], r)
        return carry

    lax.fori_loop(0, n2, stage1, 0, unroll=STAGE_UNROLL)

    def stage2(k1, carry):
        r = jnp.concatenate([as_ref[pl.ds(k1, n2, stride=ap), :],
                             as_ref[pl.ds(n1 + k1, n2, stride=ap), :]], axis=0).astype(BF16)
        gk = gf_ref[k1]
        xk = _dot(gk, r)
        xr, xi = xk[0:n2], xk[n2:2 * n2]
        sp = sp_ref[0, k1].astype(F32)
        sr, si = sp[0:n2], sp[n2:2 * n2]
        yk = jnp.concatenate([xr * sr - xi * si, xr * si + xi * sr], axis=0).astype(BF16)
        bk = _tn(gk, yk)
        bs_ref[pl.ds(k1, n2, stride=ap), :] = bk[0:n2]
        bs_ref[pl.ds(n1 + k1, n2, stride=ap), :] = bk[n2:2 * n2]
        return carry

    lax.fori_loop(0, n1, stage2, 0, unroll=2 * STAGE_UNROLL)

    def stage3(j, carry):
        yn = _dot(if1_ref[...], bs_ref[_rows8(j * ap, 2 * n1), :].astype(BF16))
        y_ref[pl.ds(j, n1h, stride=xp), :] = yn[0:n1h]
        y_ref[pl.ds(half + j, n1h, stride=xp), :] = yn[n1h:n1]
        return carry

    lax.fori_loop(0, n2, stage3, 0, unroll=STAGE_UNROLL)

    def finish(i, carry):
        rows = pl.ds(pl.multiple_of(i * n2, n2), n2)
        for p in range(2):
            gate = _short_conv(g_ref[p, rows, :].astype(F32), gw_ref, gb_ref)
            z = x_ref[_rows8(p * half + i * xp, n2), :]
            y = y_ref[_rows8(p * half + i * xp, n2), :]
            o_ref[p, rows, :] = (gate * (y + z * hb_ref[...])).astype(BF16)
        return carry

    lax.fori_loop(0, n1h, finish, 0, unroll=ROW_UNROLL)


def _hyconv(z, z_col, g, g_col, conv_w, conv_b, zw_col, gw_col, spec, order, hy_bias, tabs, conv_z):
    b, l, _ = z.shape
    f1c, _, gf, if1 = tabs
    n1, n2x2 = gf.shape[0], gf.shape[1]
    n2 = n2x2 // 2
    n1h = n1 // 2
    c = spec.shape[-1]
    ncb = c // LANES
    xp = n2 + STRIDE_PAD
    ap = 2 * n1 + STRIDE_PAD
    vm = (2 * 2 * 2 * l * LANES * 2 + 2 * 2 * l * LANES * 2 + 2 * n1 * n2x2 * LANES * 2
          + 2 * gf.size * 2 + 2 * 2 * n1h * xp * LANES * 4 + 2 * n2 * ap * LANES * 4)
    return pl.pallas_call(
        functools.partial(_hyconv_body, conv_z=conv_z, n2=n2, xp=xp, ap=ap),
        grid=(ncb, b // 2),
        in_specs=[pl.BlockSpec((2, l, LANES), lambda j, p: (p, 0, z_col // LANES + j)),
                  pl.BlockSpec((2, l, LANES), lambda j, p: (p, 0, g_col // LANES + j)),
                  pl.BlockSpec((3, LANES), lambda j, p: (0, zw_col // LANES + j)),
                  pl.BlockSpec((1, LANES), lambda j, p: (0, zw_col // LANES + j)),
                  pl.BlockSpec((3, LANES), lambda j, p: (0, gw_col // LANES + j)),
                  pl.BlockSpec((1, LANES), lambda j, p: (0, gw_col // LANES + j)),
                  pl.BlockSpec((1, n1, n2x2, LANES), lambda j, p: (order, 0, 0, j)),
                  pl.BlockSpec((1, LANES), lambda j, p: (0, j)),
                  pl.BlockSpec(f1c.shape, lambda j, p: (0, 0)),
                  pl.BlockSpec(gf.shape, lambda j, p: (0, 0, 0)),
                  pl.BlockSpec(if1.shape, lambda j, p: (0, 0))],
        out_specs=pl.BlockSpec((2, l, LANES), lambda j, p: (p, 0, j)),
        out_shape=jax.ShapeDtypeStruct((b, l, c), BF16),
        scratch_shapes=[pltpu.VMEM((2 * n1h * xp, LANES), F32),
                        pltpu.VMEM((n2 * ap, LANES), F32),
                        pltpu.VMEM((n2 * ap, LANES), F32),
                        pltpu.VMEM((2 * n1h * xp, LANES), F32)],
        compiler_params=_params(("arbitrary", "arbitrary"), vm),
        name="hyconv%d" % order,
    )(z, g, conv_w, conv_b, conv_w, conv_b, spec, hy_bias, f1c, gf, if1)


def _mix_body(x_ref, of_ref, ob_ref, ug_ref, ugate_ref, yhy_ref, gnw_ref, phy_ref, pgla_ref, wout_ref,
              g1_ref, n2w_ref, sh2_ref, sc2_ref, rwt_ref, x1_ref, xn2_ref, lg_ref, *, heads):
    d = x_ref.shape[2]
    o = of_ref[0].astype(F32) + ob_ref[0].astype(F32)
    dv = o.shape[1] // heads
    parts = []
    for h in range(heads):
        seg = o[:, h * dv:(h + 1) * dv]
        parts.append(seg * lax.rsqrt(jnp.mean(seg * seg, axis=-1, keepdims=True) + EPS))
    y_gla = jnp.concatenate(parts, axis=1) * gnw_ref[...] * _silu(ug_ref[0].astype(F32))
    gates = jax.nn.sigmoid(ugate_ref[0].astype(F32))
    merged = (gates[:, :d] * _dot(yhy_ref[0], phy_ref[...])
              + gates[:, d:] * _dot(y_gla.astype(BF16), pgla_ref[...]))
    x1 = x_ref[0] + g1_ref[0] * _dot(merged.astype(BF16), wout_ref[...])
    x1_ref[0] = x1
    xn2 = _norm_mod(x1, n2w_ref[...], sh2_ref[0], sc2_ref[0])
    xn2_ref[0] = xn2.astype(BF16)
    lg_ref[0] = _nt(rwt_ref[...], xn2, precision=HIGHEST)


def _mix(x, o_f, o_b, u, g_col, gate_col, y_hy, gnw, phy, pgla, wout, g1, n2w, sh2, sc2, rwt, tm):
    b, l, d = x.shape
    vw = o_f.shape[2]
    ne = rwt.shape[0]
    tok = lambda w: pl.BlockSpec((1, tm, w), lambda bi, i: (bi, i, 0))
    per_b = pl.BlockSpec((1, 1, d), lambda bi, i: (bi, 0, 0))
    const = lambda shape: pl.BlockSpec(shape, lambda bi, i: (0, 0))
    vm = (2 * tm * d * 4 * 2 + 2 * tm * (3 * vw + 2 * d + 2 * d) * 2 + 2 * 3 * d * d * 2 + 12 * tm * d * 4)
    return pl.pallas_call(
        functools.partial(_mix_body, heads=GLA_HEADS),
        grid=(b, l // tm),
        in_specs=[tok(d), tok(vw), tok(vw),
                  pl.BlockSpec((1, tm, vw), lambda bi, i: (bi, i, g_col // vw)),
                  pl.BlockSpec((1, tm, 2 * d), lambda bi, i: (bi, i, gate_col // (2 * d))),
                  tok(d), const((1, vw)), const((d, d)), const((vw, d)), const((d, d)),
                  per_b, const((1, d)), per_b, per_b, const((ne, d))],
        out_specs=(tok(d), tok(d), pl.BlockSpec((1, ne, tm), lambda bi, i: (bi, 0, i))),
        out_shape=(jax.ShapeDtypeStruct((b, l, d), F32), jax.ShapeDtypeStruct((b, l, d), BF16),
                   jax.ShapeDtypeStruct((b, ne, l), F32)),
        compiler_params=_params(("arbitrary", "arbitrary"), vm),
        name="mix",
    )(x, o_f, o_b, u, u, y_hy, gnw, phy, pgla, wout, g1, n2w, sh2, sc2, rwt)


def _select_body(lg_ref, bias_ref, o_ref):
    ne, tn = lg_ref.shape[1], lg_ref.shape[2]
    ng = N_GROUPS
    pg = ne // ng
    scores = jax.nn.sigmoid(lg_ref[0]).reshape(ng, pg, tn)
    sel = scores + bias_ref[...]
    ie = lax.broadcasted_iota(jnp.int32, sel.shape, 1)
    m1 = jnp.max(sel, axis=1, keepdims=True)
    i1 = jnp.min(jnp.where(sel == m1, ie, pg), axis=1, keepdims=True)
    m2 = jnp.max(jnp.where(ie == i1, -jnp.inf, sel), axis=1, keepdims=True)
    grp = m1 + m2
    ig = lax.broadcasted_iota(jnp.int32, grp.shape, 0)
    rank = jnp.zeros(grp.shape, jnp.int32)
    for g in range(ng):
        other = grp[g:g + 1]
        rank = rank + jnp.where((other > grp) | ((other == grp) & (g < ig)), 1, 0)
    cand = jnp.where(rank < TOPK_GROUPS, sel, -jnp.inf)
    flat = ig * pg + ie
    rank = jnp.zeros(sel.shape, jnp.int32)
    for g in range(ng):
        for e in range(pg):
            other = cand[g:g + 1, e:e + 1, :]
            rank = rank + jnp.where((other > cand) | ((other == cand) & (g * pg + e < flat)), 1, 0)
    w = jnp.where(rank < TOP_K, scores, 0.0)
    tot = jnp.sum(jnp.sum(w, axis=1, keepdims=True), axis=0, keepdims=True)
    o_ref[0] = (w / tot * ROUTED_SCALE).reshape(ne, tn)


def _select(logits_t, bias, tn):
    b, ne, l = logits_t.shape
    nb = l // tn
    return pl.pallas_call(
        _select_body,
        grid=(b, nb),
        in_specs=[pl.BlockSpec((1, ne, tn), lambda bi, i: (bi, 0, i)),
                  pl.BlockSpec(bias.shape, lambda bi, i: (0, 0, 0))],
        out_specs=pl.BlockSpec((1, ne, tn), lambda bi, i: (bi, 0, i)),
        out_shape=jax.ShapeDtypeStruct((b, ne, l), F32),
        compiler_params=_params(("arbitrary", "arbitrary"), 64 * ne * tn * 4),
        name="select",
    )(logits_t, bias)


def _moe_body(x_ref, ct_ref, tri_ref, w1_ref, w3_ref, w2_ref, sw1_ref, sw3_ref, sw2_ref,
              x1_ref, g2_ref, fnw_ref, o_ref, acc_ref, rank_ref, *, sub, cap):
    g = pl.program_id(1)
    per = w1_ref.shape[0]
    tm = x_ref.shape[0]
    ns = tm // sub

    @pl.when(g == 0)
    def _():
        x = x_ref[...]
        hs = _silu(_dot(x, sw1_ref[...])) * _dot(x, sw3_ref[...])
        acc_ref[...] = _dot(hs.astype(BF16), sw2_ref[...])
        for s in range(ns):
            chosen = ct_ref[0, :, s * sub:(s + 1) * sub] > 0.0
            before = _dot(jnp.where(chosen, 1.0, 0.0).astype(BF16), tri_ref[...])
            rank_ref[s] = jnp.where(chosen, before, -1.0)

    e0 = g * per
    ranks = [[rank_ref[s, pl.ds(e0 + i, 1), :] for i in range(per)] for s in range(ns)]
    wts = [[ct_ref[0, pl.ds(e0 + i, 1), s * sub:(s + 1) * sub] for i in range(per)] for s in range(ns)]
    top = ranks[0][0]
    for s in range(ns):
        for i in range(per):
            top = jnp.maximum(top, ranks[s][i])
    n_rounds = (jnp.max(top).astype(jnp.int32) + cap) // cap
    slot = lax.broadcasted_iota(jnp.int32, (cap, sub), 0).astype(F32)

    def one_round(r, carry):
        base = slot + (r * cap).astype(F32)
        packed, spread = [], []
        for s in range(ns):
            hits = [base == ranks[s][i] for i in range(per)]
            pack = jnp.concatenate([jnp.where(h, 1.0, 0.0).astype(BF16) for h in hits], axis=0)
            spread.append(jnp.concatenate([jnp.where(h, wts[s][i], 0.0).astype(BF16)
                                           for i, h in enumerate(hits)], axis=0))
            packed.append(_dot(pack, x_ref[s * sub:(s + 1) * sub, :]).astype(BF16))
        outs = []
        for i in range(per):
            ze = jnp.concatenate([packed[s][i * cap:(i + 1) * cap] for s in range(ns)], axis=0)
            h = _silu(_dot(ze, w1_ref[i])) * _dot(ze, w3_ref[i])
            outs.append(_dot(h.astype(BF16), w2_ref[i]).astype(BF16))
        for s in range(ns):
            ys = jnp.concatenate([outs[i][s * cap:(s + 1) * cap] for i in range(per)], axis=0)
            acc_ref[s * sub:(s + 1) * sub, :] += _tn(spread[s], ys)
        return carry

    lax.fori_loop(0, n_rounds, one_round, 0)

    @pl.when(g == pl.num_programs(1) - 1)
    def _():
        y = x1_ref[...] + g2_ref[0] * acc_ref[...]
        ms = jnp.mean(y * y, axis=-1, keepdims=True)
        o_ref[...] = y * lax.rsqrt(ms + EPS) * fnw_ref[...]


def _moe(xn2, comb_t, w1, w3, w2, sw1, sw3, sw2, x1, g2, fnw, tm):
    t, d = xn2.shape
    ne, _, f = w1.shape
    l = comb_t.shape[2]
    per = EXP_PER_STEP
    gpb = l // tm
    sub, cap = MOE_SUB, MOE_CAP
    tri = jnp.asarray(np.triu(np.ones((sub, sub), np.float32), 1), BF16)
    tok = lambda w: pl.BlockSpec((tm, w), lambda i, g: (i, 0))
    const = lambda shape: pl.BlockSpec(shape, lambda i, g: (0,) * len(shape))
    vm = (2 * tm * d * 2 + 2 * ne * tm * 4 + 2 * 3 * per * d * f * 2 + 2 * 3 * d * f * 2
          + 2 * tm * d * 4 * 2 + tm * d * 4 + 16 * per * cap * (tm // sub) * d)
    return pl.pallas_call(
        functools.partial(_moe_body, sub=sub, cap=cap),
        grid=(t // tm, ne // per),
        in_specs=[tok(d),
                  pl.BlockSpec((1, ne, tm), lambda i, g: (i // gpb, 0, i % gpb)),
                  const(tri.shape),
                  pl.BlockSpec((per, d, f), lambda i, g: (g, 0, 0)),
                  pl.BlockSpec((per, d, f), lambda i, g: (g, 0, 0)),
                  pl.BlockSpec((per, f, d), lambda i, g: (g, 0, 0)),
                  const(sw1.shape), const(sw3.shape), const(sw2.shape),
                  tok(d),
                  pl.BlockSpec((1, 1, d), lambda i, g: (i // gpb, 0, 0)),
                  const((1, d))],
        out_specs=tok(d),
        out_shape=jax.ShapeDtypeStruct((t, d), F32),
        scratch_shapes=[pltpu.VMEM((tm, d), F32), pltpu.VMEM((tm // sub, ne, sub), F32)],
        compiler_params=_params(("arbitrary", "arbitrary"), vm),
        name="moe",
    )(xn2, comb_t, tri, w1, w3, w2, sw1, sw3, sw2, x1, g2, fnw)


def _pad_to(a, rows, cols):
    return jnp.pad(a, ((0, rows - a.shape[0]), (0, cols - a.shape[1])))


def kernel(x, c, ctx, c_ctx, ada_w, ada_b, norm1_w, norm2_w, w_in, hy_conv_w, hy_conv_b, hy_w1, hy_b1, hy_freq, hy_w2, hy_b2, hy_w3, hy_bias, gla_a_w2, gla_a_b, gla_norm_w, proj_hy, proj_gla, w_out, router_w, router_bias, exp_w1, exp_w3, exp_w2, sh_w1, sh_w3, sh_w2, final_norm_w):
    b, l, d = x.shape
    assert ada_w.shape[0] == 1, "single-layer block"
    assert l // GRID_W * GRID_W == l and FFT_N2 == GRID_W
    heads = GLA_HEADS
    qk_w = d // 2
    dk = qk_w // heads
    v_w = d
    dv = v_w // heads
    a_w = 2 * GLA_RANK
    hy_w = d
    hy_cols = (HY_ORDER + 1) * hy_w

    rows = -(-(b + 1) // 8) * 8
    cc = jnp.zeros((rows, d), F32).at[:b].set(c).at[b].set(c_ctx)
    mods = _mods(cc, ada_w[0], ada_b[0][None])
    sh1, sc1, g1, sh2, sc2, g2 = [m[:b, None, :] for m in jnp.split(mods, 6, axis=-1)]
    csh1, csc1 = [jnp.broadcast_to(m[b][None, None, :], (b, 1, d)) for m in jnp.split(mods, 6, axis=-1)[:2]]

    w = w_in[0]
    o_a = qk_w + v_w
    o_q = o_a + a_w
    o_g = o_q + qk_w
    o_hy = o_g + v_w
    o_gate = o_hy + hy_cols
    w_k, w_v = w[:, :qk_w], w[:, qk_w:o_a]
    w_a = jnp.pad(w[:, o_a:o_q], ((0, 0), (0, LANES - a_w)))
    wp = jnp.concatenate([w[:, o_gate:], w[:, o_g:o_hy], w_v, w_k, w[:, o_q:o_g], w[:, o_hy:o_gate], w_a],
                         axis=1).astype(BF16)
    p_gate = 0
    p_g = 2 * d
    p_v = p_g + v_w
    p_k = p_v + v_w
    p_q = p_k + qk_w
    p_hy = p_q + qk_w
    p_a = p_hy + hy_cols
    n_all = p_a + LANES
    w_ctx = jnp.concatenate([w_k, w_v, w_a], axis=1).astype(BF16)
    cols_ctx = (0, qk_w, qk_w + v_w, None)
    cols = (p_k, p_v, p_a, p_q)

    nw1 = norm1_w[0][None]
    u_ctx = _inproj(ctx, nw1, csh1, csc1, w_ctx, ctx.shape[1], w_ctx.shape[1])
    u = _inproj(x, nw1, sh1, sc1, wp, 1024, n_all // 5)

    wa = gla_a_w2[0].reshape(2, GLA_RANK, heads, dk).transpose(0, 2, 1, 3)
    waf = jnp.pad(wa[0], ((0, 0), (0, LANES - GLA_RANK), (0, 0))).astype(BF16)
    wab = jnp.pad(wa[1], ((0, 0), (GLA_RANK, LANES - 2 * GLA_RANK), (0, 0))).astype(BF16)
    ba = gla_a_b[0].reshape(2, heads, 1, dk)
    zeros_state = jnp.zeros((b, heads, dv, dk), F32)
    _, _, s_f, s_b = _gla(u_ctx, cols_ctx, waf, wab, ba[0], ba[1], zeros_state, zeros_state, dk, dv, False,
                          ctx.shape[1])
    o_f, o_b, _, _ = _gla(u, cols, waf, wab, ba[0], ba[1], s_f, s_b, dk, dv, True, GLA_BLOCK)

    n = 2 * l
    max_decay = math.log(HY_TARGET) / HY_FAST_DECAY
    min_decay = math.log(HY_TARGET) / HY_SLOW_DECAY
    deltas = jnp.asarray(np.abs(np.linspace(min_decay, max_decay, hy_w, dtype=np.float32))[None])
    ffn = hy_w1.shape[2]
    kern = _hyfilt(l, hy_w,
                   _pad_to(hy_w1[0], LANES, LANES), _pad_to(hy_b1[0][None], 1, LANES),
                   _pad_to(hy_freq[0, 0][None], 1, LANES),
                   _pad_to(hy_w2[0], LANES, LANES), _pad_to(hy_b2[0][None], 1, LANES),
                   _pad_to(hy_freq[0, 1][None], 1, LANES),
                   jnp.pad(hy_w3[0], ((0, LANES - ffn), (0, 0))), deltas)
    tabs = _fft_tables(n, FFT_N2)
    spec = _hyspec(kern, tabs[1], tabs[2], FFT_N2)
    cw, cb = hy_conv_w[0], hy_conv_b[0][None]
    z1 = _hyconv(u, p_hy, u, p_hy + hy_w, cw, cb, 0, hy_w, spec, 0, hy_bias[0, 0][None], tabs, True)
    y_hy = _hyconv(z1, 0, u, p_hy + 2 * hy_w, cw, cb, 0, 2 * hy_w, spec, 1, hy_bias[0, 1][None], tabs, False)

    x1, xn2, logits_t = _mix(x, o_f, o_b, u, p_g, p_gate, y_hy, gla_norm_w[0][None],
                             proj_hy[0].astype(BF16), proj_gla[0].astype(BF16), w_out[0].astype(BF16),
                             g1, norm2_w[0][None], sh2, sc2, router_w[0].T, 512)
    comb = _select(logits_t, router_bias[0].reshape(N_GROUPS, N_EXPERTS // N_GROUPS, 1), 512)
    out = _moe(xn2.reshape(b * l, d), comb, exp_w1[0].astype(BF16), exp_w3[0].astype(BF16),
               exp_w2[0].astype(BF16), sh_w1[0].astype(BF16), sh_w3[0].astype(BF16), sh_w2[0].astype(BF16),
               x1.reshape(b * l, d), g2, final_norm_w[None], 1024)
    return out.reshape(b, l, d)
```

```python
import functools
import math

import jax
import jax.numpy as jnp
import numpy as np
from jax import lax
from jax.experimental import pallas as pl
from jax.experimental.pallas import tpu as pltpu

F32 = jnp.float32
BF16 = jnp.bfloat16
HIGHEST = lax.Precision.HIGHEST

GRID_W = 64
EPS = 1e-6
HY_ORDER = 2
HY_BANDS = 16
HY_FAST_DECAY = 0.3
HY_SLOW_DECAY = 1.5
HY_TARGET = 1e-2
GLA_HEADS = 4
GLA_RANK = 16
GLA_TAU = 16.0
N_EXPERTS = 64
N_GROUPS = 8
TOPK_GROUPS = 4
TOP_K = 8
ROUTED_SCALE = 2.5

LANES = 128
V7X_VMEM_BYTES = 64 * 1024 * 1024
VMEM_CAP_BYTES = 56 * 1024 * 1024

GLA_CHUNK = 256
GLA_BLOCK = 512
FFT_N2 = 64
STRIDE_PAD = 8
STAGE_UNROLL = 8
ROW_UNROLL = 2
EXP_PER_STEP = 4
MOE_SUB = 256
MOE_CAP = 64


def _params(sem, vmem_bytes):
    limit = int(min(VMEM_CAP_BYTES, max(16 * 1024 * 1024, vmem_bytes * 5 // 4 + (2 << 20))))
    return pltpu.CompilerParams(dimension_semantics=sem, vmem_limit_bytes=limit)


def _nt(a, b, **kw):
    return lax.dot_general(a, b, (((1,), (1,)), ((), ())), preferred_element_type=F32, **kw)


def _tn(a, b):
    return lax.dot_general(a, b, (((0,), (0,)), ((), ())), preferred_element_type=F32)


def _dot(a, b, **kw):
    return jnp.dot(a, b, preferred_element_type=F32, **kw)


def _silu(x):
    return x * jax.nn.sigmoid(x)


def _mods_body(c_ref, w_ref, b_ref, o_ref):
    o_ref[...] = _dot(_silu(c_ref[...]), w_ref[...], precision=HIGHEST) + b_ref[...]


def _mods(cc, w, b):
    rows, d = cc.shape
    n = w.shape[1]
    tn = n // 4
    return pl.pallas_call(
        _mods_body,
        grid=(n // tn,),
        in_specs=[pl.BlockSpec((rows, d), lambda j: (0, 0)),
                  pl.BlockSpec((d, tn), lambda j: (0, j)),
                  pl.BlockSpec((1, tn), lambda j: (0, j))],
        out_specs=pl.BlockSpec((rows, tn), lambda j: (0, j)),
        out_shape=jax.ShapeDtypeStruct((rows, n), F32),
        compiler_params=_params(("arbitrary",), 2 * d * tn * 4),
        name="mods",
    )(cc, w, b)


def _norm_mod(x, w, shift, scale):
    ms = jnp.mean(x * x, axis=-1, keepdims=True)
    return (x * lax.rsqrt(ms + EPS) * w) * (1.0 + scale) + shift


def _inproj_body(x_ref, nw_ref, sh_ref, sc_ref, w_ref, o_ref, xn_ref):
    @pl.when(pl.program_id(2) == 0)
    def _():
        xn_ref[...] = _norm_mod(x_ref[0], nw_ref[...], sh_ref[0], sc_ref[0]).astype(BF16)

    o_ref[0] = _dot(xn_ref[...], w_ref[...]).astype(BF16)


def _inproj(x, nw, shift, scale, w, tm, tn):
    b, l, d = x.shape
    n = w.shape[1]
    vm = 2 * tm * d * 4 + 2 * d * tn * 2 + 2 * tm * tn * 2 + tm * d * 2
    return pl.pallas_call(
        _inproj_body,
        grid=(b, l // tm, n // tn),
        in_specs=[pl.BlockSpec((1, tm, d), lambda bi, i, j: (bi, i, 0)),
                  pl.BlockSpec((1, d), lambda bi, i, j: (0, 0)),
                  pl.BlockSpec((1, 1, d), lambda bi, i, j: (bi, 0, 0)),
                  pl.BlockSpec((1, 1, d), lambda bi, i, j: (bi, 0, 0)),
                  pl.BlockSpec((d, tn), lambda bi, i, j: (0, j))],
        out_specs=pl.BlockSpec((1, tm, tn), lambda bi, i, j: (bi, i, j)),
        out_shape=jax.ShapeDtypeStruct((b, l, n), BF16),
        scratch_shapes=[pltpu.VMEM((tm, d), BF16)],
        compiler_params=_params(("arbitrary", "arbitrary", "arbitrary"), vm),
        name="inproj",
    )(x, nw, shift, scale, w)


def _gla_tables(c, inclusive, flip):
    idx = np.arange(c)
    i = idx[:, None]
    x = idx[None, :]
    blocks = [(x <= i) if inclusive else (x < i), x > i]
    masks = []
    h = c // 2
    while h >= 1:
        mid = (idx // (2 * h)) * (2 * h) + h
        mi = mid[:, None]
        hi = i if inclusive else i - 1
        blocks.append(((i >= mi) & (x >= mi) & (x <= hi)) | ((i < mi) & (x > i) & (x <= mi - 1)))
        same = (idx[:, None] // (2 * h)) == (idx[None, :] // (2 * h))
        masks.append(same & (idx[:, None] >= mi) & (idx[None, :] < mid[None, :]))
        h //= 2
    masks.append(np.eye(c, dtype=bool))
    if flip:
        blocks = [b[::-1, ::-1] for b in blocks]
        masks = [m[::-1, ::-1] for m in masks]
    lall = np.concatenate(blocks + [np.ones((8, c), bool)], axis=0)
    return lall.astype(np.float32), np.stack(masks).astype(np.float32)


def _gla_chunk(q, k, v, a, wa, ba, lall, masks_ref, st_ref, inclusive, q_scale):
    c, dk = k.shape
    n_levels = int(math.log2(c))
    xg = _dot(a, wa) + ba
    g = (jnp.minimum(xg, 0.0) - jnp.log(1.0 + jnp.exp(-jnp.abs(xg)))) * (1.0 / GLA_TAU)
    g_hi = g.astype(BF16)
    g_lo = (g - g_hi.astype(F32)).astype(BF16)
    e2 = _dot(lall, jnp.concatenate([g_hi, g_lo], axis=1))
    ex = jnp.exp(jnp.minimum(e2[:, :dk] + e2[:, dk:], 0.0))

    kf = k.astype(F32)
    st = st_ref[...]
    k1 = (kf * ex[c:2 * c]).astype(BF16)
    st_ref[...] = st * ex[(2 + n_levels) * c:(2 + n_levels) * c + 1] + _tn(v, k1)
    if q is None:
        return None
    qf = q.astype(F32) * q_scale
    o = _nt((qf * ex[0:c]).astype(BF16), st.astype(BF16))
    attn = jnp.zeros((c, c), F32)
    for lv in range(n_levels):
        ex_l = ex[(2 + lv) * c:(3 + lv) * c]
        attn = attn + _nt((qf * ex_l).astype(BF16), (kf * ex_l).astype(BF16)) * masks_ref[lv]
    if inclusive:
        attn = attn + _nt(qf.astype(BF16), k) * masks_ref[n_levels]
    return o + _dot(attn.astype(BF16), v)


def _gla_body(*refs, with_q, n_sub, chunk, q_scale):
    if with_q:
        (kf_ref, vf_ref, af_ref, qf_ref, kb_ref, vb_ref, ab_ref, qb_ref, waf_ref, wab_ref, baf_ref, bab_ref,
         lf_ref, lb_ref, mf_ref, mb_ref, s0f_ref, s0b_ref, of_ref, ob_ref, sf_ref, sb_ref, stf_ref, stb_ref) = refs
    else:
        (kf_ref, vf_ref, af_ref, kb_ref, vb_ref, ab_ref, waf_ref, wab_ref, baf_ref, bab_ref,
         lf_ref, lb_ref, mf_ref, mb_ref, s0f_ref, s0b_ref, sf_ref, sb_ref, stf_ref, stb_ref) = refs
        qf_ref = qb_ref = of_ref = ob_ref = None

    @pl.when(pl.program_id(2) == 0)
    def _():
        stf_ref[...] = s0f_ref[0, 0]
        stb_ref[...] = s0b_ref[0, 0]

    for s in range(n_sub):
        sl = slice(s * chunk, (s + 1) * chunk)
        o = _gla_chunk(None if qf_ref is None else qf_ref[0, sl, :], kf_ref[0, sl, :], vf_ref[0, sl, :],
                       af_ref[0, sl, :], waf_ref[0], baf_ref[0], lf_ref[...], mf_ref, stf_ref, True, q_scale)
        if with_q:
            of_ref[0, sl, :] = o.astype(BF16)
    for s in reversed(range(n_sub)):
        sl = slice(s * chunk, (s + 1) * chunk)
        o = _gla_chunk(None if qb_ref is None else qb_ref[0, sl, :], kb_ref[0, sl, :], vb_ref[0, sl, :],
                       ab_ref[0, sl, :], wab_ref[0], bab_ref[0], lb_ref[...], mb_ref, stb_ref, False, q_scale)
        if with_q:
            ob_ref[0, sl, :] = o.astype(BF16)
    sf_ref[0, 0] = stf_ref[...]
    sb_ref[0, 0] = stb_ref[...]


def _gla(u, cols, waf, wab, baf, bab, s0f, s0b, dk, dv, with_q, tb):
    b, l, _ = u.shape
    h = GLA_HEADS
    nb = l // tb
    n_sub = tb // GLA_CHUNK
    lf, mf = _gla_tables(GLA_CHUNK, True, False)
    lb, mb = _gla_tables(GLA_CHUNK, False, True)
    lf, lb = jnp.asarray(lf, BF16), jnp.asarray(lb, BF16)
    mf, mb = jnp.asarray(mf), jnp.asarray(mb)
    kc, vc, ac, qc = cols

    def seq_specs(rev):
        def blk(i):
            return (nb - 1 - i) if rev else i
        specs = [pl.BlockSpec((1, tb, dk), lambda bi, hi, i: (bi, blk(i), kc // dk + hi)),
                 pl.BlockSpec((1, tb, dv), lambda bi, hi, i: (bi, blk(i), vc // dv + hi)),
                 pl.BlockSpec((1, tb, LANES), lambda bi, hi, i: (bi, blk(i), ac // LANES))]
        if with_q:
            specs.append(pl.BlockSpec((1, tb, dk), lambda bi, hi, i: (bi, blk(i), qc // dk + hi)))
        return specs

    def const_spec(shape):
        nd = len(shape)
        return pl.BlockSpec(shape, lambda bi, hi, i: (0,) * nd)

    head_w = pl.BlockSpec((1, LANES, dk), lambda bi, hi, i: (hi, 0, 0))
    head_b = pl.BlockSpec((1, 1, dk), lambda bi, hi, i: (hi, 0, 0))
    st_spec = pl.BlockSpec((1, 1, dv, dk), lambda bi, hi, i: (bi, hi, 0, 0))
    in_specs = (seq_specs(False) + seq_specs(True) + [head_w, head_w, head_b, head_b,
                const_spec(lf.shape), const_spec(lb.shape), const_spec(mf.shape), const_spec(mb.shape),
                st_spec, st_spec])
    st_shape = jax.ShapeDtypeStruct((b, h, dv, dk), F32)
    if with_q:
        o_shape = jax.ShapeDtypeStruct((b, l, h * dv), BF16)
        out_shape = (o_shape, o_shape, st_shape, st_shape)
        out_specs = (pl.BlockSpec((1, tb, dv), lambda bi, hi, i: (bi, i, hi)),
                     pl.BlockSpec((1, tb, dv), lambda bi, hi, i: (bi, nb - 1 - i, hi)),
                     st_spec, st_spec)
        args = (u,) * 8
    else:
        out_shape = (st_shape, st_shape)
        out_specs = (st_spec, st_spec)
        args = (u,) * 6
    vm = 4 * tb * (2 * dk + dv + LANES) * 2 * 2 + 8 * dv * dk * 4 + 4 * tb * dv * 2 + (4 << 20)
    outs = pl.pallas_call(
        functools.partial(_gla_body, with_q=with_q, n_sub=n_sub, chunk=GLA_CHUNK, q_scale=dk ** -0.5),
        grid=(b, h, nb),
        in_specs=in_specs,
        out_specs=out_specs,
        out_shape=out_shape,
        scratch_shapes=[pltpu.VMEM((dv, dk), F32), pltpu.VMEM((dv, dk), F32)],
        compiler_params=_params(("arbitrary", "arbitrary", "arbitrary"), vm),
        name="gla" if with_q else "gla_ctx",
    )(*args, waf, wab, baf, bab, lf, lb, mf, mb, s0f, s0b)
    if with_q:
        return outs
    return None, None, outs[0], outs[1]


def _hy_tables(l):
    t = np.linspace(0.0, 1.0, l, dtype=np.float32).astype(np.float64)[:, None]
    w = 2.0 * math.pi * np.arange(l, dtype=np.float64)[:, None] / l
    f = np.linspace(1e-4, HY_BANDS - 1, HY_BANDS, dtype=np.float32).astype(np.float64)[None, :]
    z = np.concatenate([t, np.cos(f * w), -np.sin(f * w)], axis=-1)
    rev = (l - np.arange(l)) % l

    def pad(a):
        out = np.zeros((l, LANES), np.float32)
        out[:, :a.shape[1]] = a
        return out

    tt = np.broadcast_to(t, (l, LANES)).astype(np.float32)
    return pad(z), pad(z[rev]), tt, np.ascontiguousarray(tt[rev])


def _hyfilt_body(z1_ref, z2_ref, t1_ref, t2_ref, w1_ref, b1_ref, f1_ref, w2_ref, b2_ref, f2_ref,
                 w3f_ref, w3b_ref, dl_ref, o_ref, ha_ref, hb_ref):
    l = z1_ref.shape[0]

    @pl.when((pl.program_id(0) == 0) & (pl.program_id(1) == 0))
    def _():
        for z_ref, h_ref in ((z1_ref, ha_ref), (z2_ref, hb_ref)):
            h = jnp.sin(f1_ref[...] * (_dot(z_ref[...], w1_ref[...], precision=HIGHEST) + b1_ref[...]))
            h_ref[...] = jnp.sin(f2_ref[...] * (_dot(h, w2_ref[...], precision=HIGHEST) + b2_ref[...]))

    dl = dl_ref[...]
    hf = _dot(ha_ref[...], w3f_ref[...], precision=HIGHEST) * jnp.exp(-t1_ref[...] * dl)
    hb = _dot(hb_ref[...], w3b_ref[...], precision=HIGHEST) * jnp.exp(-t2_ref[...] * dl)
    row = lax.broadcasted_iota(jnp.int32, hb.shape, 0)
    hb = jnp.where(row == 0, 0.0, hb)
    ss = jnp.sum(hf * hf, axis=0, keepdims=True) + jnp.sum(hb * hb, axis=0, keepdims=True)
    scale = lax.rsqrt(ss)
    o_ref[0, 0:l, :] = hf * scale
    o_ref[0, l:2 * l, :] = hb * scale


def _hyfilt(l, c, w1, b1, f1, w2, b2, f2, w3, deltas):
    z1, z2, t1, t2 = (jnp.asarray(a) for a in _hy_tables(l))
    ncb = c // LANES
    tab = pl.BlockSpec((l, LANES), lambda o, j: (0, 0))
    sq = pl.BlockSpec((LANES, LANES), lambda o, j: (0, 0))
    row = pl.BlockSpec((1, LANES), lambda o, j: (0, 0))
    return pl.pallas_call(
        _hyfilt_body,
        grid=(HY_ORDER, ncb),
        in_specs=[tab, tab, tab, tab, sq, row, row, sq, row, row,
                  pl.BlockSpec((LANES, LANES), lambda o, j: (0, 2 * o * ncb + j)),
                  pl.BlockSpec((LANES, LANES), lambda o, j: (0, (2 * o + 1) * ncb + j)),
                  pl.BlockSpec((1, LANES), lambda o, j: (0, j))],
        out_specs=pl.BlockSpec((1, 2 * l, LANES), lambda o, j: (o, 0, j)),
        out_shape=jax.ShapeDtypeStruct((HY_ORDER, 2 * l, c), F32),
        scratch_shapes=[pltpu.VMEM((l, LANES), F32), pltpu.VMEM((l, LANES), F32)],
        compiler_params=_params(("arbitrary", "arbitrary"), 24 * l * LANES * 4),
        name="hyfilt",
    )(z1, z2, t1, t2, w1, b1, f1, w2, b2, f2, w3, w3, deltas)


def _fft_tables(n, n2):
    n1 = n // n2
    h = n1 // 2
    k1 = np.arange(n1)[:, None]
    a = 2.0 * math.pi * k1 * np.arange(h)[None, :] / n1
    c, s = np.cos(a), np.sin(a)
    f1c = np.block([[c, s], [-s, c]])
    a = 2.0 * math.pi * k1 * np.arange(n1)[None, :] / n1
    f1r = np.concatenate([np.cos(a), -np.sin(a)], axis=0)
    kk = np.arange(n1)[:, None, None] + n1 * np.arange(n2)[None, :, None]
    a = 2.0 * math.pi * kk * np.arange(n2)[None, None, :] / n
    c, s = np.cos(a), np.sin(a)
    gf = np.concatenate([np.concatenate([c, s], axis=2), np.concatenate([-s, c], axis=2)], axis=1)
    a = 2.0 * math.pi * np.arange(h)[:, None] * np.arange(n1)[None, :] / n1
    c, s = np.cos(a), np.sin(a)
    if1 = np.block([[c, -s], [s, c]])
    return tuple(jnp.asarray(m, BF16) for m in (f1c, f1r, gf, if1))


def _rows8(start, size):
    return pl.ds(pl.multiple_of(start, 8), size)


def _hyspec_body(k_ref, f1_ref, gf_ref, o_ref, as_ref, *, n, n2, ap):
    n1 = n // n2

    def stage1(j, carry):
        r = k_ref[0, pl.ds(j, n1, stride=n2), :].astype(BF16)
        as_ref[_rows8(j * ap, 2 * n1), :] = _dot(f1_ref[...], r)
        return carry

    lax.fori_loop(0, n2, stage1, 0, unroll=STAGE_UNROLL)

    def stage2(k1, carry):
        r = jnp.concatenate([as_ref[pl.ds(k1, n2, stride=ap), :],
                             as_ref[pl.ds(n1 + k1, n2, stride=ap), :]], axis=0).astype(BF16)
        o_ref[0, k1] = (_dot(gf_ref[k1], r) * (1.0 / n)).astype(BF16)
        return carry

    lax.fori_loop(0, n1, stage2, 0, unroll=STAGE_UNROLL)


def _hyspec(kern, f1r, gf, n2):
    order, n, c = kern.shape
    n1 = n // n2
    ap = 2 * n1 + STRIDE_PAD
    return pl.pallas_call(
        functools.partial(_hyspec_body, n=n, n2=n2, ap=ap),
        grid=(order, c // LANES),
        in_specs=[pl.BlockSpec((1, n, LANES), lambda o, j: (o, 0, j)),
                  pl.BlockSpec(f1r.shape, lambda o, j: (0, 0)),
                  pl.BlockSpec(gf.shape, lambda o, j: (0, 0, 0))],
        out_specs=pl.BlockSpec((1, n1, 2 * n2, LANES), lambda o, j: (o, 0, 0, j)),
        out_shape=jax.ShapeDtypeStruct((order, n1, 2 * n2, c), BF16),
        scratch_shapes=[pltpu.VMEM((n2 * ap, LANES), F32)],
        compiler_params=_params(("arbitrary", "arbitrary"),
                                2 * n * LANES * 4 + 2 * gf.size * 2 + n2 * ap * LANES * 4 + 2 * n * LANES * 2),
        name="hyspec",
    )(kern, f1r, gf)


def _short_conv(u, w_ref, b_ref):
    r = u.shape[0]
    row = lax.broadcasted_iota(jnp.int32, u.shape, 0)
    up = jnp.where(row == 0, 0.0, pltpu.roll(u, 1, axis=0))
    dn = jnp.where(row == r - 1, 0.0, pltpu.roll(u, r - 1, axis=0))
    return up * w_ref[0:1, :] + u * w_ref[1:2, :] + dn * w_ref[2:3, :] + b_ref[...]


def _hyconv_body(z_ref, g_ref, zw_ref, zb_ref, gw_ref, gb_ref, sp_ref, hb_ref, f1_ref, gf_ref, if1_ref,
                 o_ref, x_ref, as_ref, bs_ref, y_ref, *, conv_z, n2, xp, ap):
    n1h = z_ref.shape[1] // n2
    n1 = 2 * n1h
    half = n1h * xp

    def fill(i, carry):
        for p in range(2):
            u = z_ref[p, pl.ds(pl.multiple_of(i * n2, n2), n2), :].astype(F32)
            x_ref[_rows8(p * half + i * xp, n2), :] = _short_conv(u, zw_ref, zb_ref) if conv_z else u
        return carry

    lax.fori_loop(0, n1h, fill, 0, unroll=ROW_UNROLL)

    def stage1(j, carry):
        r = jnp.concatenate([x_ref[pl.ds(j, n1h, stride=xp), :],
                             x_ref[pl.ds(half + j, n1h, stride=xp), :]], axis=0).astype(BF16)
        as_ref[_rows8(j * ap, 2 * n1), :] = _dot(f1_ref[...], r)
        return carry

    lax.fori_loop(0, n2, stage1, 0, unroll=STAGE_UNROLL)

    def stage2(k1, carry):
        r = jnp.concatenate([as_ref[pl.ds(k1, n2, stride=ap), :],
                             as_ref[pl.ds(n1 + k1, n2, stride=ap), :]], axis=0).astype(BF16)
        gk = gf_ref[k1]
        xk = _dot(gk, r)
        xr, xi = xk[0:n2], xk[n2:2 * n2]
        sp = sp_ref[0, k1].astype(F32)
        sr, si = sp[0:n2], sp[n2:2 * n2]
        yk = jnp.concatenate([xr * sr - xi * si, xr * si + xi * sr], axis=0).astype(BF16)
        bk = _tn(gk, yk)
        bs_ref[pl.ds(k1, n2, stride=ap), :] = bk[0:n2]
        bs_ref[pl.ds(n1 + k1, n2, stride=ap), :] = bk[n2:2 * n2]
        return carry

    lax.fori_loop(0, n1, stage2, 0, unroll=2 * STAGE_UNROLL)

    def stage3(j, carry):
        yn = _dot(if1_ref[...], bs_ref[_rows8(j * ap, 2 * n1), :].astype(BF16))
        y_ref[pl.ds(j, n1h, stride=xp), :] = yn[0:n1h]
        y_ref[pl.ds(half + j, n1h, stride=xp), :] = yn[n1h:n1]
        return carry

    lax.fori_loop(0, n2, stage3, 0, unroll=STAGE_UNROLL)

    def finish(i, carry):
        rows = pl.ds(pl.multiple_of(i * n2, n2), n2)
        for p in range(2):
            gate = _short_conv(g_ref[p, rows, :].astype(F32), gw_ref, gb_ref)
            z = x_ref[_rows8(p * half + i * xp, n2), :]
            y = y_ref[_rows8(p * half + i * xp, n2), :]
            o_ref[p, rows, :] = (gate * (y + z * hb_ref[...])).astype(BF16)
        return carry

    lax.fori_loop(0, n1h, finish, 0, unroll=ROW_UNROLL)


def _hyconv(z, z_col, g, g_col, conv_w, conv_b, zw_col, gw_col, spec, order, hy_bias, tabs, conv_z):
    b, l, _ = z.shape
    f1c, _, gf, if1 = tabs
    n1, n2x2 = gf.shape[0], gf.shape[1]
    n2 = n2x2 // 2
    n1h = n1 // 2
    c = spec.shape[-1]
    ncb = c // LANES
    xp = n2 + STRIDE_PAD
    ap = 2 * n1 + STRIDE_PAD
    vm = (2 * 2 * 2 * l * LANES * 2 + 2 * 2 * l * LANES * 2 + 2 * n1 * n2x2 * LANES * 2
          + 2 * gf.size * 2 + 2 * 2 * n1h * xp * LANES * 4 + 2 * n2 * ap * LANES * 4)
    return pl.pallas_call(
        functools.partial(_hyconv_body, conv_z=conv_z, n2=n2, xp=xp, ap=ap),
        grid=(ncb, b // 2),
        in_specs=[pl.BlockSpec((2, l, LANES), lambda j, p: (p, 0, z_col // LANES + j)),
                  pl.BlockSpec((2, l, LANES), lambda j, p: (p, 0, g_col // LANES + j)),
                  pl.BlockSpec((3, LANES), lambda j, p: (0, zw_col // LANES + j)),
                  pl.BlockSpec((1, LANES), lambda j, p: (0, zw_col // LANES + j)),
                  pl.BlockSpec((3, LANES), lambda j, p: (0, gw_col // LANES + j)),
                  pl.BlockSpec((1, LANES), lambda j, p: (0, gw_col // LANES + j)),
                  pl.BlockSpec((1, n1, n2x2, LANES), lambda j, p: (order, 0, 0, j)),
                  pl.BlockSpec((1, LANES), lambda j, p: (0, j)),
                  pl.BlockSpec(f1c.shape, lambda j, p: (0, 0)),
                  pl.BlockSpec(gf.shape, lambda j, p: (0, 0, 0)),
                  pl.BlockSpec(if1.shape, lambda j, p: (0, 0))],
        out_specs=pl.BlockSpec((2, l, LANES), lambda j, p: (p, 0, j)),
        out_shape=jax.ShapeDtypeStruct((b, l, c), BF16),
        scratch_shapes=[pltpu.VMEM((2 * n1h * xp, LANES), F32),
                        pltpu.VMEM((n2 * ap, LANES), F32),
                        pltpu.VMEM((n2 * ap, LANES), F32),
                        pltpu.VMEM((2 * n1h * xp, LANES), F32)],
        compiler_params=_params(("arbitrary", "arbitrary"), vm),
        name="hyconv%d" % order,
    )(z, g, conv_w, conv_b, conv_w, conv_b, spec, hy_bias, f1c, gf, if1)


def _mix_body(x_ref, of_ref, ob_ref, ug_ref, ugate_ref, yhy_ref, gnw_ref, phy_ref, pgla_ref, wout_ref,
              g1_ref, n2w_ref, sh2_ref, sc2_ref, rwt_ref, x1_ref, xn2_ref, lg_ref, *, heads):
    d = x_ref.shape[2]
    o = of_ref[0].astype(F32) + ob_ref[0].astype(F32)
    dv = o.shape[1] // heads
    parts = []
    for h in range(heads):
        seg = o[:, h * dv:(h + 1) * dv]
        parts.append(seg * lax.rsqrt(jnp.mean(seg * seg, axis=-1, keepdims=True) + EPS))
    y_gla = jnp.concatenate(parts, axis=1) * gnw_ref[...] * _silu(ug_ref[0].astype(F32))
    gates = jax.nn.sigmoid(ugate_ref[0].astype(F32))
    merged = (gates[:, :d] * _dot(yhy_ref[0], phy_ref[...])
              + gates[:, d:] * _dot(y_gla.astype(BF16), pgla_ref[...]))
    x1 = x_ref[0] + g1_ref[0] * _dot(merged.astype(BF16), wout_ref[...])
    x1_ref[0] = x1
    xn2 = _norm_mod(x1, n2w_ref[...], sh2_ref[0], sc2_ref[0])
    xn2_ref[0] = xn2.astype(BF16)
    lg_ref[0] = _nt(rwt_ref[...], xn2, precision=HIGHEST)


def _mix(x, o_f, o_b, u, g_col, gate_col, y_hy, gnw, phy, pgla, wout, g1, n2w, sh2, sc2, rwt, tm):
    b, l, d = x.shape
    vw = o_f.shape[2]
    ne = rwt.shape[0]
    tok = lambda w: pl.BlockSpec((1, tm, w), lambda bi, i: (bi, i, 0))
    per_b = pl.BlockSpec((1, 1, d), lambda bi, i: (bi, 0, 0))
    const = lambda shape: pl.BlockSpec(shape, lambda bi, i: (0, 0))
    vm = (2 * tm * d * 4 * 2 + 2 * tm * (3 * vw + 2 * d + 2 * d) * 2 + 2 * 3 * d * d * 2 + 12 * tm * d * 4)
    return pl.pallas_call(
        functools.partial(_mix_body, heads=GLA_HEADS),
        grid=(b, l // tm),
        in_specs=[tok(d), tok(vw), tok(vw),
                  pl.BlockSpec((1, tm, vw), lambda bi, i: (bi, i, g_col // vw)),
                  pl.BlockSpec((1, tm, 2 * d), lambda bi, i: (bi, i, gate_col // (2 * d))),
                  tok(d), const((1, vw)), const((d, d)), const((vw, d)), const((d, d)),
                  per_b, const((1, d)), per_b, per_b, const((ne, d))],
        out_specs=(tok(d), tok(d), pl.BlockSpec((1, ne, tm), lambda bi, i: (bi, 0, i))),
        out_shape=(jax.ShapeDtypeStruct((b, l, d), F32), jax.ShapeDtypeStruct((b, l, d), BF16),
                   jax.ShapeDtypeStruct((b, ne, l), F32)),
        compiler_params=_params(("arbitrary", "arbitrary"), vm),
        name="mix",
    )(x, o_f, o_b, u, u, y_hy, gnw, phy, pgla, wout, g1, n2w, sh2, sc2, rwt)


def _select_body(lg_ref, bias_ref, o_ref):
    ne, tn = lg_ref.shape[1], lg_ref.shape[2]
    ng = N_GROUPS
    pg = ne // ng
    scores = jax.nn.sigmoid(lg_ref[0]).reshape(ng, pg, tn)
    sel = scores + bias_ref[...]
    ie = lax.broadcasted_iota(jnp.int32, sel.shape, 1)
    m1 = jnp.max(sel, axis=1, keepdims=True)
    i1 = jnp.min(jnp.where(sel == m1, ie, pg), axis=1, keepdims=True)
    m2 = jnp.max(jnp.where(ie == i1, -jnp.inf, sel), axis=1, keepdims=True)
    grp = m1 + m2
    ig = lax.broadcasted_iota(jnp.int32, grp.shape, 0)
    rank = jnp.zeros(grp.shape, jnp.int32)
    for g in range(ng):
        other = grp[g:g + 1]
        rank = rank + jnp.where((other > grp) | ((other == grp) & (g < ig)), 1, 0)
    cand = jnp.where(rank < TOPK_GROUPS, sel, -jnp.inf)
    flat = ig * pg + ie
    rank = jnp.zeros(sel.shape, jnp.int32)
    for g in range(ng):
        for e in range(pg):
            other = cand[g:g + 1, e:e + 1, :]
            rank = rank + jnp.where((other > cand) | ((other == cand) & (g * pg + e < flat)), 1, 0)
    w = jnp.where(rank < TOP_K, scores, 0.0)
    tot = jnp.sum(jnp.sum(w, axis=1, keepdims=True), axis=0, keepdims=True)
    o_ref[0] = (w / tot * ROUTED_SCALE).reshape(ne, tn)


def _select(logits_t, bias, tn):
    b, ne, l = logits_t.shape
    nb = l // tn
    return pl.pallas_call(
        _select_body,
        grid=(b, nb),
        in_specs=[pl.BlockSpec((1, ne, tn), lambda bi, i: (bi, 0, i)),
                  pl.BlockSpec(bias.shape, lambda bi, i: (0, 0, 0))],
        out_specs=pl.BlockSpec((1, ne, tn), lambda bi, i: (bi, 0, i)),
        out_shape=jax.ShapeDtypeStruct((b, ne, l), F32),
        compiler_params=_params(("arbitrary", "arbitrary"), 64 * ne * tn * 4),
        name="select",
    )(logits_t, bias)


def _moe_body(order_ref, x_ref, ct_ref, tri_ref, *refs, per, sub, cap):
    w1_refs, w3_refs, w2_refs = refs[:per], refs[per:2 * per], refs[2 * per:3 * per]
    sw1_ref, sw3_ref, sw2_ref, x1_ref, g2_ref, fnw_ref, o_ref, acc_ref, rank_ref = refs[3 * per:]
    g = pl.program_id(1)
    tm = x_ref.shape[0]
    ns = tm // sub

    @pl.when(g == 0)
    def _():
        x = x_ref[...]
        hs = _silu(_dot(x, sw1_ref[...])) * _dot(x, sw3_ref[...])
        acc_ref[...] = _dot(hs.astype(BF16), sw2_ref[...])
        for s in range(ns):
            chosen = ct_ref[0, :, s * sub:(s + 1) * sub] > 0.0
            before = _dot(jnp.where(chosen, 1.0, 0.0).astype(BF16), tri_ref[...])
            rank_ref[s] = jnp.where(chosen, before, -1.0)

    ids = [order_ref[pl.program_id(0), g * per + i] for i in range(per)]
    ranks = [[rank_ref[s, pl.ds(ids[i], 1), :] for i in range(per)] for s in range(ns)]
    wts = [[ct_ref[0, pl.ds(ids[i], 1), s * sub:(s + 1) * sub] for i in range(per)] for s in range(ns)]
    top = ranks[0][0]
    for s in range(ns):
        for i in range(per):
            top = jnp.maximum(top, ranks[s][i])
    n_rounds = (jnp.max(top).astype(jnp.int32) + cap) // cap
    slot = lax.broadcasted_iota(jnp.int32, (cap, sub), 0).astype(F32)

    def one_round(r, carry):
        base = slot + (r * cap).astype(F32)
        packed, spread = [], []
        for s in range(ns):
            hits = [base == ranks[s][i] for i in range(per)]
            pack = jnp.concatenate([jnp.where(h, 1.0, 0.0).astype(BF16) for h in hits], axis=0)
            spread.append(jnp.concatenate([jnp.where(h, wts[s][i], 0.0).astype(BF16)
                                           for i, h in enumerate(hits)], axis=0))
            packed.append(_dot(pack, x_ref[s * sub:(s + 1) * sub, :]).astype(BF16))
        outs = []
        for i in range(per):
            ze = jnp.concatenate([packed[s][i * cap:(i + 1) * cap] for s in range(ns)], axis=0)
            h = _silu(_dot(ze, w1_refs[i][0])) * _dot(ze, w3_refs[i][0])
            outs.append(_dot(h.astype(BF16), w2_refs[i][0]).astype(BF16))
        for s in range(ns):
            ys = jnp.concatenate([outs[i][s * cap:(s + 1) * cap] for i in range(per)], axis=0)
            acc_ref[s * sub:(s + 1) * sub, :] += _tn(spread[s], ys)
        return carry

    lax.fori_loop(0, n_rounds, one_round, 0)

    @pl.when(g == pl.num_programs(1) - 1)
    def _():
        y = x1_ref[...] + g2_ref[0] * acc_ref[...]
        ms = jnp.mean(y * y, axis=-1, keepdims=True)
        o_ref[...] = y * lax.rsqrt(ms + EPS) * fnw_ref[...]


def _moe(xn2, comb_t, w1, w3, w2, sw1, sw3, sw2, x1, g2, fnw, tm):
    t, d = xn2.shape
    ne, _, f = w1.shape
    l = comb_t.shape[2]
    per = EXP_PER_STEP
    gpb = l // tm
    sub, cap = MOE_SUB, MOE_CAP
    tri = jnp.asarray(np.triu(np.ones((sub, sub), np.float32), 1), BF16)
    b = comb_t.shape[0]
    load = jnp.max(jnp.sum((comb_t > 0.0).reshape(b, ne, gpb, tm // sub, sub), axis=-1, dtype=jnp.int32), axis=-1)
    order = jnp.argsort(load.transpose(0, 2, 1).reshape(b * gpb, ne), axis=-1).astype(jnp.int32)
    tok = lambda w: pl.BlockSpec((tm, w), lambda i, g, o: (i, 0))
    const = lambda shape: pl.BlockSpec(shape, lambda i, g, o: (0,) * len(shape))

    def expert(shape):
        return [pl.BlockSpec((1,) + shape, lambda i, g, o, k=k: (o[i, g * per + k], 0, 0)) for k in range(per)]

    vm = (2 * tm * d * 2 + 2 * ne * tm * 4 + 2 * 3 * per * d * f * 2 + 2 * 3 * d * f * 2
          + 2 * tm * d * 4 * 2 + tm * d * 4 + 16 * per * cap * (tm // sub) * d)
    grid_spec = pltpu.PrefetchScalarGridSpec(
        num_scalar_prefetch=1,
        grid=(t // tm, ne // per),
        in_specs=([tok(d),
                   pl.BlockSpec((1, ne, tm), lambda i, g, o: (i // gpb, 0, i % gpb)),
                   const(tri.shape)]
                  + expert((d, f)) + expert((d, f)) + expert((f, d))
                  + [const(sw1.shape), const(sw3.shape), const(sw2.shape),
                     tok(d),
                     pl.BlockSpec((1, 1, d), lambda i, g, o: (i // gpb, 0, 0)),
                     const((1, d))]),
        out_specs=tok(d),
        scratch_shapes=[pltpu.VMEM((tm, d), F32), pltpu.VMEM((tm // sub, ne, sub), F32)])
    return pl.pallas_call(
        functools.partial(_moe_body, per=per, sub=sub, cap=cap),
        grid_spec=grid_spec,
        out_shape=jax.ShapeDtypeStruct((t, d), F32),
        compiler_params=_params(("arbitrary", "arbitrary"), vm),
        name="moe",
    )(order, xn2, comb_t, tri, *([w1] * per), *([w3] * per), *([w2] * per), sw1, sw3, sw2, x1, g2, fnw)


def _pad_to(a, rows, cols):
    return jnp.pad(a, ((0, rows - a.shape[0]), (0, cols - a.shape[1])))


def kernel(x, c, ctx, c_ctx, ada_w, ada_b, norm1_w, norm2_w, w_in, hy_conv_w, hy_conv_b, hy_w1, hy_b1, hy_freq, hy_w2, hy_b2, hy_w3, hy_bias, gla_a_w2, gla_a_b, gla_norm_w, proj_hy, proj_gla, w_out, router_w, router_bias, exp_w1, exp_w3, exp_w2, sh_w1, sh_w3, sh_w2, final_norm_w):
    b, l, d = x.shape
    assert ada_w.shape[0] == 1, "single-layer block"
    assert l // GRID_W * GRID_W == l and FFT_N2 == GRID_W
    heads = GLA_HEADS
    qk_w = d // 2
    dk = qk_w // heads
    v_w = d
    dv = v_w // heads
    a_w = 2 * GLA_RANK
    hy_w = d
    hy_cols = (HY_ORDER + 1) * hy_w

    rows = -(-(b + 1) // 8) * 8
    cc = jnp.zeros((rows, d), F32).at[:b].set(c).at[b].set(c_ctx)
    mods = _mods(cc, ada_w[0], ada_b[0][None])
    sh1, sc1, g1, sh2, sc2, g2 = [m[:b, None, :] for m in jnp.split(mods, 6, axis=-1)]
    csh1, csc1 = [jnp.broadcast_to(m[b][None, None, :], (b, 1, d)) for m in jnp.split(mods, 6, axis=-1)[:2]]

    w = w_in[0]
    o_a = qk_w + v_w
    o_q = o_a + a_w
    o_g = o_q + qk_w
    o_hy = o_g + v_w
    o_gate = o_hy + hy_cols
    w_k, w_v = w[:, :qk_w], w[:, qk_w:o_a]
    w_a = jnp.pad(w[:, o_a:o_q], ((0, 0), (0, LANES - a_w)))
    wp = jnp.concatenate([w[:, o_gate:], w[:, o_g:o_hy], w_v, w_k, w[:, o_q:o_g], w[:, o_hy:o_gate], w_a],
                         axis=1).astype(BF16)
    p_gate = 0
    p_g = 2 * d
    p_v = p_g + v_w
    p_k = p_v + v_w
    p_q = p_k + qk_w
    p_hy = p_q + qk_w
    p_a = p_hy + hy_cols
    n_all = p_a + LANES
    w_ctx = jnp.concatenate([w_k, w_v, w_a], axis=1).astype(BF16)
    cols_ctx = (0, qk_w, qk_w + v_w, None)
    cols = (p_k, p_v, p_a, p_q)

    nw1 = norm1_w[0][None]
    u_ctx = _inproj(ctx, nw1, csh1, csc1, w_ctx, ctx.shape[1], w_ctx.shape[1])
    u = _inproj(x, nw1, sh1, sc1, wp, 1024, n_all // 5)

    wa = gla_a_w2[0].reshape(2, GLA_RANK, heads, dk).transpose(0, 2, 1, 3)
    waf = jnp.pad(wa[0], ((0, 0), (0, LANES - GLA_RANK), (0, 0))).astype(BF16)
    wab = jnp.pad(wa[1], ((0, 0), (GLA_RANK, LANES - 2 * GLA_RANK), (0, 0))).astype(BF16)
    ba = gla_a_b[0].reshape(2, heads, 1, dk)
    zeros_state = jnp.zeros((b, heads, dv, dk), F32)
    _, _, s_f, s_b = _gla(u_ctx, cols_ctx, waf, wab, ba[0], ba[1], zeros_state, zeros_state, dk, dv, False,
                          ctx.shape[1])
    o_f, o_b, _, _ = _gla(u, cols, waf, wab, ba[0], ba[1], s_f, s_b, dk, dv, True, GLA_BLOCK)

    n = 2 * l
    max_decay = math.log(HY_TARGET) / HY_FAST_DECAY
    min_decay = math.log(HY_TARGET) / HY_SLOW_DECAY
    deltas = jnp.asarray(np.abs(np.linspace(min_decay, max_decay, hy_w, dtype=np.float32))[None])
    ffn = hy_w1.shape[2]
    kern = _hyfilt(l, hy_w,
                   _pad_to(hy_w1[0], LANES, LANES), _pad_to(hy_b1[0][None], 1, LANES),
                   _pad_to(hy_freq[0, 0][None], 1, LANES),
                   _pad_to(hy_w2[0], LANES, LANES), _pad_to(hy_b2[0][None], 1, LANES),
                   _pad_to(hy_freq[0, 1][None], 1, LANES),
                   jnp.pad(hy_w3[0], ((0, LANES - ffn), (0, 0))), deltas)
    tabs = _fft_tables(n, FFT_N2)
    spec = _hyspec(kern, tabs[1], tabs[2], FFT_N2)
    cw, cb = hy_conv_w[0], hy_conv_b[0][None]
    z1 = _hyconv(u, p_hy, u, p_hy + hy_w, cw, cb, 0, hy_w, spec, 0, hy_bias[0, 0][None], tabs, True)
    y_hy = _hyconv(z1, 0, u, p_hy + 2 * hy_w, cw, cb, 0, 2 * hy_w, spec, 1, hy_bias[0, 1][None], tabs, False)

    x1, xn2, logits_t = _mix(x, o_f, o_b, u, p_g, p_gate, y_hy, gla_norm_w[0][None],
                             proj_hy[0].astype(BF16), proj_gla[0].astype(BF16), w_out[0].astype(BF16),
                             g1, norm2_w[0][None], sh2, sc2, router_w[0].T, 512)
    comb = _select(logits_t, router_bias[0].reshape(N_GROUPS, N_EXPERTS // N_GROUPS, 1), 512)
    out = _moe(xn2.reshape(b * l, d), comb, exp_w1[0].astype(BF16), exp_w3[0].astype(BF16),
               exp_w2[0].astype(BF16), sh_w1[0].astype(BF16), sh_w3[0].astype(BF16), sh_w2[0].astype(BF16),
               x1.reshape(b * l, d), g2, final_norm_w[None], 1024)
    return out.reshape(b, l, d)
```

```python
import functools
import math

import jax
import jax.numpy as jnp
import numpy as np
from jax import lax
from jax.experimental import pallas as pl
from jax.experimental.pallas import tpu as pltpu

F32 = jnp.float32
BF16 = jnp.bfloat16
HIGHEST = lax.Precision.HIGHEST

GRID_W = 64
EPS = 1e-6
HY_ORDER = 2
HY_BANDS = 16
HY_FAST_DECAY = 0.3
HY_SLOW_DECAY = 1.5
HY_TARGET = 1e-2
GLA_HEADS = 4
GLA_RANK = 16
GLA_TAU = 16.0
N_EXPERTS = 64
N_GROUPS = 8
TOPK_GROUPS = 4
TOP_K = 8
ROUTED_SCALE = 2.5

LANES = 128
V7X_VMEM_BYTES = 64 * 1024 * 1024
VMEM_CAP_BYTES = 56 * 1024 * 1024

GLA_CHUNK = 256
GLA_BLOCK = 512
GLA_VPU_MIN_HALF = 4
FFT_N2 = 64
STRIDE_PAD = 8
STAGE_UNROLL = 8
ROW_UNROLL = 2
EXP_PER_STEP = 4
MOE_SUB = 256
MOE_CAP = 64


def _params(sem, vmem_bytes):
    limit = int(min(VMEM_CAP_BYTES, max(16 * 1024 * 1024, vmem_bytes * 5 // 4 + (2 << 20))))
    return pltpu.CompilerParams(dimension_semantics=sem, vmem_limit_bytes=limit)


def _nt(a, b, **kw):
    return lax.dot_general(a, b, (((1,), (1,)), ((), ())), preferred_element_type=F32, **kw)


def _tn(a, b):
    return lax.dot_general(a, b, (((0,), (0,)), ((), ())), preferred_element_type=F32)


def _dot(a, b, **kw):
    return jnp.dot(a, b, preferred_element_type=F32, **kw)


def _silu(x):
    return x * jax.nn.sigmoid(x)


def _mods_body(c_ref, w_ref, b_ref, o_ref):
    o_ref[...] = _dot(_silu(c_ref[...]), w_ref[...], precision=HIGHEST) + b_ref[...]


def _mods(cc, w, b):
    rows, d = cc.shape
    n = w.shape[1]
    tn = n // 4
    return pl.pallas_call(
        _mods_body,
        grid=(n // tn,),
        in_specs=[pl.BlockSpec((rows, d), lambda j: (0, 0)),
                  pl.BlockSpec((d, tn), lambda j: (0, j)),
                  pl.BlockSpec((1, tn), lambda j: (0, j))],
        out_specs=pl.BlockSpec((rows, tn), lambda j: (0, j)),
        out_shape=jax.ShapeDtypeStruct((rows, n), F32),
        compiler_params=_params(("arbitrary",), 2 * d * tn * 4),
        name="mods",
    )(cc, w, b)


def _norm_mod(x, w, shift, scale):
    ms = jnp.mean(x * x, axis=-1, keepdims=True)
    return (x * lax.rsqrt(ms + EPS) * w) * (1.0 + scale) + shift


def _inproj_body(x_ref, nw_ref, sh_ref, sc_ref, w_ref, o_ref, xn_ref):
    @pl.when(pl.program_id(2) == 0)
    def _():
        xn_ref[...] = _norm_mod(x_ref[0], nw_ref[...], sh_ref[0], sc_ref[0]).astype(BF16)

    o_ref[0] = _dot(xn_ref[...], w_ref[...]).astype(BF16)


def _inproj(x, nw, shift, scale, w, tm, tn):
    b, l, d = x.shape
    n = w.shape[1]
    vm = 2 * tm * d * 4 + 2 * d * tn * 2 + 2 * tm * tn * 2 + tm * d * 2
    return pl.pallas_call(
        _inproj_body,
        grid=(b, l // tm, n // tn),
        in_specs=[pl.BlockSpec((1, tm, d), lambda bi, i, j: (bi, i, 0)),
                  pl.BlockSpec((1, d), lambda bi, i, j: (0, 0)),
                  pl.BlockSpec((1, 1, d), lambda bi, i, j: (bi, 0, 0)),
                  pl.BlockSpec((1, 1, d), lambda bi, i, j: (bi, 0, 0)),
                  pl.BlockSpec((d, tn), lambda bi, i, j: (0, j))],
        out_specs=pl.BlockSpec((1, tm, tn), lambda bi, i, j: (bi, i, j)),
        out_shape=jax.ShapeDtypeStruct((b, l, n), BF16),
        scratch_shapes=[pltpu.VMEM((tm, d), BF16)],
        compiler_params=_params(("arbitrary", "arbitrary", "arbitrary"), vm),
        name="inproj",
    )(x, nw, shift, scale, w)


def _gla_tables(c, inclusive, flip):
    idx = np.arange(c)
    i = idx[:, None]
    x = idx[None, :]
    blocks = [x <= i]
    masks = []
    h = c // 2
    while h >= 1:
        mid = (idx // (2 * h)) * (2 * h) + h
        mi = mid[:, None]
        hi = i if inclusive else i - 1
        if h < GLA_VPU_MIN_HALF:
            blocks.append(((i >= mi) & (x >= mi) & (x <= hi)) | ((i < mi) & (x > i) & (x <= mi - 1)))
        same = (idx[:, None] // (2 * h)) == (idx[None, :] // (2 * h))
        masks.append(same & (idx[:, None] >= mi) & (idx[None, :] < mid[None, :]))
        h //= 2
    masks.append(np.eye(c, dtype=bool))
    if flip:
        blocks = [b[::-1, ::-1] for b in blocks]
        masks = [m[::-1, ::-1] for m in masks]
    lall = np.concatenate(blocks + [np.ones((8, c), bool)], axis=0)
    return lall.astype(np.float32), np.stack(masks).astype(np.float32)


def _gla_chunk(q, k, v, a, wa, ba, lall, masks_ref, st_ref, inclusive, flip, q_scale):
    c, dk = k.shape
    n_levels = int(math.log2(c))
    xg = _dot(a, wa) + ba
    g = (jnp.minimum(xg, 0.0) - jnp.log(1.0 + jnp.exp(-jnp.abs(xg)))) * (1.0 / GLA_TAU)
    g_hi = g.astype(BF16)
    g_lo = (g - g_hi.astype(F32)).astype(BF16)
    e2 = _dot(lall, jnp.concatenate([g_hi, g_lo], axis=1))
    e = e2[:, :dk] + e2[:, dk:]
    run = e[0:c]
    tot = e[e.shape[0] - 8:e.shape[0] - 7]
    upto = run if inclusive else run - g

    def decay(t):
        return jnp.exp(jnp.minimum(t, 0.0))

    row = lax.broadcasted_iota(jnp.int32, (c, dk), 0)
    ex_levels = []
    h = c // 2
    while h >= 1:
        if h >= GLA_VPU_MIN_HALF:
            piv = jnp.concatenate([jnp.broadcast_to(run[m + (h if flip else h - 1):m + (h if flip else h - 1) + 1, :],
                                                    (2 * h, dk)) for m in range(0, c, 2 * h)], axis=0)
            first_half = (row & (2 * h - 1)) < h
            ex_levels.append(decay(jnp.where(first_half, piv - run, upto - piv) if not flip
                                   else jnp.where(first_half, upto - piv, piv - run)))
        else:
            lo = (1 + len(ex_levels) - (n_levels - int(math.log2(GLA_VPU_MIN_HALF)))) * c
            ex_levels.append(decay(e[lo:lo + c]))
        h //= 2

    kf = k.astype(F32)
    st = st_ref[...]
    k1 = (kf * decay(tot - run)).astype(BF16)
    st_ref[...] = st * decay(tot) + _tn(v, k1)
    if q is None:
        return None
    qf = q.astype(F32) * q_scale
    o = _nt((qf * decay(upto)).astype(BF16), st.astype(BF16))
    attn = jnp.zeros((c, c), F32)
    for lv in range(n_levels):
        ex_l = ex_levels[lv]
        attn = attn + _nt((qf * ex_l).astype(BF16), (kf * ex_l).astype(BF16)) * masks_ref[lv]
    if inclusive:
        attn = attn + _nt(qf.astype(BF16), k) * masks_ref[n_levels]
    return o + _dot(attn.astype(BF16), v)


def _gla_body(*refs, with_q, n_sub, chunk, q_scale):
    if with_q:
        (kf_ref, vf_ref, af_ref, qf_ref, kb_ref, vb_ref, ab_ref, qb_ref, waf_ref, wab_ref, baf_ref, bab_ref,
         lf_ref, lb_ref, mf_ref, mb_ref, s0f_ref, s0b_ref, of_ref, ob_ref, sf_ref, sb_ref, stf_ref, stb_ref) = refs
    else:
        (kf_ref, vf_ref, af_ref, kb_ref, vb_ref, ab_ref, waf_ref, wab_ref, baf_ref, bab_ref,
         lf_ref, lb_ref, mf_ref, mb_ref, s0f_ref, s0b_ref, sf_ref, sb_ref, stf_ref, stb_ref) = refs
        qf_ref = qb_ref = of_ref = ob_ref = None

    @pl.when(pl.program_id(2) == 0)
    def _():
        stf_ref[...] = s0f_ref[0, 0]
        stb_ref[...] = s0b_ref[0, 0]

    for s in range(n_sub):
        sl = slice(s * chunk, (s + 1) * chunk)
        o = _gla_chunk(None if qf_ref is None else qf_ref[0, sl, :], kf_ref[0, sl, :], vf_ref[0, sl, :],
                       af_ref[0, sl, :], waf_ref[0], baf_ref[0], lf_ref[...], mf_ref, stf_ref, True, False, q_scale)
        if with_q:
            of_ref[0, sl, :] = o.astype(BF16)
    for s in reversed(range(n_sub)):
        sl = slice(s * chunk, (s + 1) * chunk)
        o = _gla_chunk(None if qb_ref is None else qb_ref[0, sl, :], kb_ref[0, sl, :], vb_ref[0, sl, :],
                       ab_ref[0, sl, :], wab_ref[0], bab_ref[0], lb_ref[...], mb_ref, stb_ref, False, True, q_scale)
        if with_q:
            ob_ref[0, sl, :] = o.astype(BF16)
    sf_ref[0, 0] = stf_ref[...]
    sb_ref[0, 0] = stb_ref[...]


def _gla(u, cols, waf, wab, baf, bab, s0f, s0b, dk, dv, with_q, tb):
    b, l, _ = u.shape
    h = GLA_HEADS
    nb = l // tb
    n_sub = tb // GLA_CHUNK
    lf, mf = _gla_tables(GLA_CHUNK, True, False)
    lb, mb = _gla_tables(GLA_CHUNK, False, True)
    lf, lb = jnp.asarray(lf, BF16), jnp.asarray(lb, BF16)
    mf, mb = jnp.asarray(mf), jnp.asarray(mb)
    kc, vc, ac, qc = cols

    def seq_specs(rev):
        def blk(i):
            return (nb - 1 - i) if rev else i
        specs = [pl.BlockSpec((1, tb, dk), lambda bi, hi, i: (bi, blk(i), kc // dk + hi)),
                 pl.BlockSpec((1, tb, dv), lambda bi, hi, i: (bi, blk(i), vc // dv + hi)),
                 pl.BlockSpec((1, tb, LANES), lambda bi, hi, i: (bi, blk(i), ac // LANES))]
        if with_q:
            specs.append(pl.BlockSpec((1, tb, dk), lambda bi, hi, i: (bi, blk(i), qc // dk + hi)))
        return specs

    def const_spec(shape):
        nd = len(shape)
        return pl.BlockSpec(shape, lambda bi, hi, i: (0,) * nd)

    head_w = pl.BlockSpec((1, LANES, dk), lambda bi, hi, i: (hi, 0, 0))
    head_b = pl.BlockSpec((1, 1, dk), lambda bi, hi, i: (hi, 0, 0))
    st_spec = pl.BlockSpec((1, 1, dv, dk), lambda bi, hi, i: (bi, hi, 0, 0))
    in_specs = (seq_specs(False) + seq_specs(True) + [head_w, head_w, head_b, head_b,
                const_spec(lf.shape), const_spec(lb.shape), const_spec(mf.shape), const_spec(mb.shape),
                st_spec, st_spec])
    st_shape = jax.ShapeDtypeStruct((b, h, dv, dk), F32)
    if with_q:
        o_shape = jax.ShapeDtypeStruct((b, l, h * dv), BF16)
        out_shape = (o_shape, o_shape, st_shape, st_shape)
        out_specs = (pl.BlockSpec((1, tb, dv), lambda bi, hi, i: (bi, i, hi)),
                     pl.BlockSpec((1, tb, dv), lambda bi, hi, i: (bi, nb - 1 - i, hi)),
                     st_spec, st_spec)
        args = (u,) * 8
    else:
        out_shape = (st_shape, st_shape)
        out_specs = (st_spec, st_spec)
        args = (u,) * 6
    vm = 4 * tb * (2 * dk + dv + LANES) * 2 * 2 + 8 * dv * dk * 4 + 4 * tb * dv * 2 + (4 << 20)
    outs = pl.pallas_call(
        functools.partial(_gla_body, with_q=with_q, n_sub=n_sub, chunk=GLA_CHUNK, q_scale=dk ** -0.5),
        grid=(b, h, nb),
        in_specs=in_specs,
        out_specs=out_specs,
        out_shape=out_shape,
        scratch_shapes=[pltpu.VMEM((dv, dk), F32), pltpu.VMEM((dv, dk), F32)],
        compiler_params=_params(("arbitrary", "arbitrary", "arbitrary"), vm),
        name="gla" if with_q else "gla_ctx",
    )(*args, waf, wab, baf, bab, lf, lb, mf, mb, s0f, s0b)
    if with_q:
        return outs
    return None, None, outs[0], outs[1]


def _hy_tables(l):
    t = np.linspace(0.0, 1.0, l, dtype=np.float32).astype(np.float64)[:, None]
    w = 2.0 * math.pi * np.arange(l, dtype=np.float64)[:, None] / l
    f = np.linspace(1e-4, HY_BANDS - 1, HY_BANDS, dtype=np.float32).astype(np.float64)[None, :]
    z = np.concatenate([t, np.cos(f * w), -np.sin(f * w)], axis=-1)
    rev = (l - np.arange(l)) % l

    def pad(a):
        out = np.zeros((l, LANES), np.float32)
        out[:, :a.shape[1]] = a
        return out

    tt = np.broadcast_to(t, (l, LANES)).astype(np.float32)
    return pad(z), pad(z[rev]), tt, np.ascontiguousarray(tt[rev])


def _hyfilt_body(z1_ref, z2_ref, t1_ref, t2_ref, w1_ref, b1_ref, f1_ref, w2_ref, b2_ref, f2_ref,
                 w3f_ref, w3b_ref, dl_ref, o_ref, ha_ref, hb_ref):
    l = z1_ref.shape[0]

    @pl.when((pl.program_id(0) == 0) & (pl.program_id(1) == 0))
    def _():
        for z_ref, h_ref in ((z1_ref, ha_ref), (z2_ref, hb_ref)):
            h = jnp.sin(f1_ref[...] * (_dot(z_ref[...], w1_ref[...], precision=HIGHEST) + b1_ref[...]))
            h_ref[...] = jnp.sin(f2_ref[...] * (_dot(h, w2_ref[...], precision=HIGHEST) + b2_ref[...]))

    dl = dl_ref[...]
    hf = _dot(ha_ref[...], w3f_ref[...], precision=HIGHEST) * jnp.exp(-t1_ref[...] * dl)
    hb = _dot(hb_ref[...], w3b_ref[...], precision=HIGHEST) * jnp.exp(-t2_ref[...] * dl)
    row = lax.broadcasted_iota(jnp.int32, hb.shape, 0)
    hb = jnp.where(row == 0, 0.0, hb)
    ss = jnp.sum(hf * hf, axis=0, keepdims=True) + jnp.sum(hb * hb, axis=0, keepdims=True)
    scale = lax.rsqrt(ss)
    o_ref[0, 0:l, :] = hf * scale
    o_ref[0, l:2 * l, :] = hb * scale


def _hyfilt(l, c, w1, b1, f1, w2, b2, f2, w3, deltas):
    z1, z2, t1, t2 = (jnp.asarray(a) for a in _hy_tables(l))
    ncb = c // LANES
    tab = pl.BlockSpec((l, LANES), lambda o, j: (0, 0))
    sq = pl.BlockSpec((LANES, LANES), lambda o, j: (0, 0))
    row = pl.BlockSpec((1, LANES), lambda o, j: (0, 0))
    return pl.pallas_call(
        _hyfilt_body,
        grid=(HY_ORDER, ncb),
        in_specs=[tab, tab, tab, tab, sq, row, row, sq, row, row,
                  pl.BlockSpec((LANES, LANES), lambda o, j: (0, 2 * o * ncb + j)),
                  pl.BlockSpec((LANES, LANES), lambda o, j: (0, (2 * o + 1) * ncb + j)),
                  pl.BlockSpec((1, LANES), lambda o, j: (0, j))],
        out_specs=pl.BlockSpec((1, 2 * l, LANES), lambda o, j: (o, 0, j)),
        out_shape=jax.ShapeDtypeStruct((HY_ORDER, 2 * l, c), F32),
        scratch_shapes=[pltpu.VMEM((l, LANES), F32), pltpu.VMEM((l, LANES), F32)],
        compiler_params=_params(("arbitrary", "arbitrary"), 24 * l * LANES * 4),
        name="hyfilt",
    )(z1, z2, t1, t2, w1, b1, f1, w2, b2, f2, w3, w3, deltas)


def _fft_tables(n, n2):
    n1 = n // n2
    h = n1 // 2
    k1 = np.arange(n1)[:, None]
    a = 2.0 * math.pi * k1 * np.arange(h)[None, :] / n1
    c, s = np.cos(a), np.sin(a)
    f1c = np.block([[c, s], [-s, c]])
    a = 2.0 * math.pi * k1 * np.arange(n1)[None, :] / n1
    f1r = np.concatenate([np.cos(a), -np.sin(a)], axis=0)
    kk = np.arange(n1)[:, None, None] + n1 * np.arange(n2)[None, :, None]
    a = 2.0 * math.pi * kk * np.arange(n2)[None, None, :] / n
    c, s = np.cos(a), np.sin(a)
    gf = np.concatenate([np.concatenate([c, s], axis=2), np.concatenate([-s, c], axis=2)], axis=1)
    a = 2.0 * math.pi * np.arange(h)[:, None] * np.arange(n1)[None, :] / n1
    c, s = np.cos(a), np.sin(a)
    if1 = np.block([[c, -s], [s, c]])
    return tuple(jnp.asarray(m, BF16) for m in (f1c, f1r, gf, if1))


def _rows8(start, size):
    return pl.ds(pl.multiple_of(start, 8), size)


def _hyspec_body(k_ref, f1_ref, gf_ref, o_ref, as_ref, *, n, n2, ap):
    n1 = n // n2

    def stage1(j, carry):
        r = k_ref[0, pl.ds(j, n1, stride=n2), :].astype(BF16)
        as_ref[_rows8(j * ap, 2 * n1), :] = _dot(f1_ref[...], r)
        return carry

    lax.fori_loop(0, n2, stage1, 0, unroll=STAGE_UNROLL)

    def stage2(k1, carry):
        r = jnp.concatenate([as_ref[pl.ds(k1, n2, stride=ap), :],
                             as_ref[pl.ds(n1 + k1, n2, stride=ap), :]], axis=0).astype(BF16)
        o_ref[0, k1] = (_dot(gf_ref[k1], r) * (1.0 / n)).astype(BF16)
        return carry

    lax.fori_loop(0, n1, stage2, 0, unroll=STAGE_UNROLL)


def _hyspec(kern, f1r, gf, n2):
    order, n, c = kern.shape
    n1 = n // n2
    ap = 2 * n1 + STRIDE_PAD
    return pl.pallas_call(
        functools.partial(_hyspec_body, n=n, n2=n2, ap=ap),
        grid=(order, c // LANES),
        in_specs=[pl.BlockSpec((1, n, LANES), lambda o, j: (o, 0, j)),
                  pl.BlockSpec(f1r.shape, lambda o, j: (0, 0)),
                  pl.BlockSpec(gf.shape, lambda o, j: (0, 0, 0))],
        out_specs=pl.BlockSpec((1, n1, 2 * n2, LANES), lambda o, j: (o, 0, 0, j)),
        out_shape=jax.ShapeDtypeStruct((order, n1, 2 * n2, c), BF16),
        scratch_shapes=[pltpu.VMEM((n2 * ap, LANES), F32)],
        compiler_params=_params(("arbitrary", "arbitrary"),
                                2 * n * LANES * 4 + 2 * gf.size * 2 + n2 * ap * LANES * 4 + 2 * n * LANES * 2),
        name="hyspec",
    )(kern, f1r, gf)


def _short_conv(u, w_ref, b_ref):
    r = u.shape[0]
    row = lax.broadcasted_iota(jnp.int32, u.shape, 0)
    up = jnp.where(row == 0, 0.0, pltpu.roll(u, 1, axis=0))
    dn = jnp.where(row == r - 1, 0.0, pltpu.roll(u, r - 1, axis=0))
    return up * w_ref[0:1, :] + u * w_ref[1:2, :] + dn * w_ref[2:3, :] + b_ref[...]


def _hyconv_body(z_ref, g_ref, zw_ref, zb_ref, gw_ref, gb_ref, sp_ref, hb_ref, f1_ref, gf_ref, if1_ref,
                 o_ref, x_ref, as_ref, bs_ref, y_ref, *, conv_z, n2, xp, ap):
    n1h = z_ref.shape[1] // n2
    n1 = 2 * n1h
    half = n1h * xp

    def fill(i, carry):
        for p in range(2):
            u = z_ref[p, pl.ds(pl.multiple_of(i * n2, n2), n2), :].astype(F32)
            x_ref[_rows8(p * half + i * xp, n2), :] = _short_conv(u, zw_ref, zb_ref) if conv_z else u
        return carry

    lax.fori_loop(0, n1h, fill, 0, unroll=ROW_UNROLL)

    def stage1(j, carry):
        r = jnp.concatenate([x_ref[pl.ds(j, n1h, stride=xp), :],
                             x_ref[pl.ds(half + j, n1h, stride=xp), :]], axis=0).astype(BF16)
        as_ref[_rows8(j * ap, 2 * n1), :] = _dot(f1_ref[...], r)
        return carry

    lax.fori_loop(0, n2, stage1, 0, unroll=STAGE_UNROLL)

    def stage2(k1, carry):
        r = jnp.concatenate([as_ref[pl.ds(k1, n2, stride=ap), :],
                             as_ref[pl.ds(n1 + k1, n2, stride=ap), :]], axis=0).astype(BF16)
        gk = gf_ref[k1]
        xk = _dot(gk, r)
        xr, xi = xk[0:n2], xk[n2:2 * n2]
        sp = sp_ref[0, k1].astype(F32)
        sr, si = sp[0:n2], sp[n2:2 * n2]
        yk = jnp.concatenate([xr * sr - xi * si, xr * si + xi * sr], axis=0).astype(BF16)
        bk = _tn(gk, yk)
        bs_ref[pl.ds(k1, n2, stride=ap), :] = bk[0:n2]
        bs_ref[pl.ds(n1 + k1, n2, stride=ap), :] = bk[n2:2 * n2]
        return carry

    lax.fori_loop(0, n1, stage2, 0, unroll=2 * STAGE_UNROLL)

    def stage3(j, carry):
        yn = _dot(if1_ref[...], bs_ref[_rows8(j * ap, 2 * n1), :].astype(BF16))
        y_ref[pl.ds(j, n1h, stride=xp), :] = yn[0:n1h]
        y_ref[pl.ds(half + j, n1h, stride=xp), :] = yn[n1h:n1]
        return carry

    lax.fori_loop(0, n2, stage3, 0, unroll=STAGE_UNROLL)

    def finish(i, carry):
        rows = pl.ds(pl.multiple_of(i * n2, n2), n2)
        for p in range(2):
            gate = _short_conv(g_ref[p, rows, :].astype(F32), gw_ref, gb_ref)
            z = x_ref[_rows8(p * half + i * xp, n2), :]
            y = y_ref[_rows8(p * half + i * xp, n2), :]
            o_ref[p, rows, :] = (gate * (y + z * hb_ref[...])).astype(BF16)
        return carry

    lax.fori_loop(0, n1h, finish, 0, unroll=ROW_UNROLL)


def _hyconv(z, z_col, g, g_col, conv_w, conv_b, zw_col, gw_col, spec, order, hy_bias, tabs, conv_z):
    b, l, _ = z.shape
    f1c, _, gf, if1 = tabs
    n1, n2x2 = gf.shape[0], gf.shape[1]
    n2 = n2x2 // 2
    n1h = n1 // 2
    c = spec.shape[-1]
    ncb = c // LANES
    xp = n2 + STRIDE_PAD
    ap = 2 * n1 + STRIDE_PAD
    vm = (2 * 2 * 2 * l * LANES * 2 + 2 * 2 * l * LANES * 2 + 2 * n1 * n2x2 * LANES * 2
          + 2 * gf.size * 2 + 2 * 2 * n1h * xp * LANES * 4 + 2 * n2 * ap * LANES * 4)
    return pl.pallas_call(
        functools.partial(_hyconv_body, conv_z=conv_z, n2=n2, xp=xp, ap=ap),
        grid=(ncb, b // 2),
        in_specs=[pl.BlockSpec((2, l, LANES), lambda j, p: (p, 0, z_col // LANES + j)),
                  pl.BlockSpec((2, l, LANES), lambda j, p: (p, 0, g_col // LANES + j)),
                  pl.BlockSpec((3, LANES), lambda j, p: (0, zw_col // LANES + j)),
                  pl.BlockSpec((1, LANES), lambda j, p: (0, zw_col // LANES + j)),
                  pl.BlockSpec((3, LANES), lambda j, p: (0, gw_col // LANES + j)),
                  pl.BlockSpec((1, LANES), lambda j, p: (0, gw_col // LANES + j)),
                  pl.BlockSpec((1, n1, n2x2, LANES), lambda j, p: (order, 0, 0, j)),
                  pl.BlockSpec((1, LANES), lambda j, p: (0, j)),
                  pl.BlockSpec(f1c.shape, lambda j, p: (0, 0)),
                  pl.BlockSpec(gf.shape, lambda j, p: (0, 0, 0)),
                  pl.BlockSpec(if1.shape, lambda j, p: (0, 0))],
        out_specs=pl.BlockSpec((2, l, LANES), lambda j, p: (p, 0, j)),
        out_shape=jax.ShapeDtypeStruct((b, l, c), BF16),
        scratch_shapes=[pltpu.VMEM((2 * n1h * xp, LANES), F32),
                        pltpu.VMEM((n2 * ap, LANES), F32),
                        pltpu.VMEM((n2 * ap, LANES), F32),
                        pltpu.VMEM((2 * n1h * xp, LANES), F32)],
        compiler_params=_params(("arbitrary", "arbitrary"), vm),
        name="hyconv%d" % order,
    )(z, g, conv_w, conv_b, conv_w, conv_b, spec, hy_bias, f1c, gf, if1)


def _mix_body(x_ref, of_ref, ob_ref, ug_ref, ugate_ref, yhy_ref, gnw_ref, phy_ref, pgla_ref, wout_ref,
              g1_ref, n2w_ref, sh2_ref, sc2_ref, rwt_ref, x1_ref, xn2_ref, lg_ref, *, heads):
    d = x_ref.shape[2]
    o = of_ref[0].astype(F32) + ob_ref[0].astype(F32)
    dv = o.shape[1] // heads
    parts = []
    for h in range(heads):
        seg = o[:, h * dv:(h + 1) * dv]
        parts.append(seg * lax.rsqrt(jnp.mean(seg * seg, axis=-1, keepdims=True) + EPS))
    y_gla = jnp.concatenate(parts, axis=1) * gnw_ref[...] * _silu(ug_ref[0].astype(F32))
    gates = jax.nn.sigmoid(ugate_ref[0].astype(F32))
    merged = (gates[:, :d] * _dot(yhy_ref[0], phy_ref[...])
              + gates[:, d:] * _dot(y_gla.astype(BF16), pgla_ref[...]))
    x1 = x_ref[0] + g1_ref[0] * _dot(merged.astype(BF16), wout_ref[...])
    x1_ref[0] = x1
    xn2 = _norm_mod(x1, n2w_ref[...], sh2_ref[0], sc2_ref[0])
    xn2_ref[0] = xn2.astype(BF16)
    lg_ref[0] = _nt(rwt_ref[...], xn2, precision=HIGHEST)


def _mix(x, o_f, o_b, u, g_col, gate_col, y_hy, gnw, phy, pgla, wout, g1, n2w, sh2, sc2, rwt, tm):
    b, l, d = x.shape
    vw = o_f.shape[2]
    ne = rwt.shape[0]
    tok = lambda w: pl.BlockSpec((1, tm, w), lambda bi, i: (bi, i, 0))
    per_b = pl.BlockSpec((1, 1, d), lambda bi, i: (bi, 0, 0))
    const = lambda shape: pl.BlockSpec(shape, lambda bi, i: (0, 0))
    vm = (2 * tm * d * 4 * 2 + 2 * tm * (3 * vw + 2 * d + 2 * d) * 2 + 2 * 3 * d * d * 2 + 12 * tm * d * 4)
    return pl.pallas_call(
        functools.partial(_mix_body, heads=GLA_HEADS),
        grid=(b, l // tm),
        in_specs=[tok(d), tok(vw), tok(vw),
                  pl.BlockSpec((1, tm, vw), lambda bi, i: (bi, i, g_col // vw)),
                  pl.BlockSpec((1, tm, 2 * d), lambda bi, i: (bi, i, gate_col // (2 * d))),
                  tok(d), const((1, vw)), const((d, d)), const((vw, d)), const((d, d)),
                  per_b, const((1, d)), per_b, per_b, const((ne, d))],
        out_specs=(tok(d), tok(d), pl.BlockSpec((1, ne, tm), lambda bi, i: (bi, 0, i))),
        out_shape=(jax.ShapeDtypeStruct((b, l, d), F32), jax.ShapeDtypeStruct((b, l, d), BF16),
                   jax.ShapeDtypeStruct((b, ne, l), F32)),
        compiler_params=_params(("arbitrary", "arbitrary"), vm),
        name="mix",
    )(x, o_f, o_b, u, u, y_hy, gnw, phy, pgla, wout, g1, n2w, sh2, sc2, rwt)


def _select_body(lg_ref, bias_ref, o_ref):
    ne, tn = lg_ref.shape[1], lg_ref.shape[2]
    ng = N_GROUPS
    pg = ne // ng
    scores = jax.nn.sigmoid(lg_ref[0]).reshape(ng, pg, tn)
    sel = scores + bias_ref[...]
    ie = lax.broadcasted_iota(jnp.int32, sel.shape, 1)
    m1 = jnp.max(sel, axis=1, keepdims=True)
    i1 = jnp.min(jnp.where(sel == m1, ie, pg), axis=1, keepdims=True)
    m2 = jnp.max(jnp.where(ie == i1, -jnp.inf, sel), axis=1, keepdims=True)
    grp = m1 + m2
    ig = lax.broadcasted_iota(jnp.int32, grp.shape, 0)
    rank = jnp.zeros(grp.shape, jnp.int32)
    for g in range(ng):
        other = grp[g:g + 1]
        rank = rank + jnp.where((other > grp) | ((other == grp) & (g < ig)), 1, 0)
    cand = jnp.where(rank < TOPK_GROUPS, sel, -jnp.inf)
    flat = ig * pg + ie
    rank = jnp.zeros(sel.shape, jnp.int32)
    for g in range(ng):
        for e in range(pg):
            other = cand[g:g + 1, e:e + 1, :]
            rank = rank + jnp.where((other > cand) | ((other == cand) & (g * pg + e < flat)), 1, 0)
    w = jnp.where(rank < TOP_K, scores, 0.0)
    tot = jnp.sum(jnp.sum(w, axis=1, keepdims=True), axis=0, keepdims=True)
    o_ref[0] = (w / tot * ROUTED_SCALE).reshape(ne, tn)


def _select(logits_t, bias, tn):
    b, ne, l = logits_t.shape
    nb = l // tn
    return pl.pallas_call(
        _select_body,
        grid=(b, nb),
        in_specs=[pl.BlockSpec((1, ne, tn), lambda bi, i: (bi, 0, i)),
                  pl.BlockSpec(bias.shape, lambda bi, i: (0, 0, 0))],
        out_specs=pl.BlockSpec((1, ne, tn), lambda bi, i: (bi, 0, i)),
        out_shape=jax.ShapeDtypeStruct((b, ne, l), F32),
        compiler_params=_params(("arbitrary", "arbitrary"), 64 * ne * tn * 4),
        name="select",
    )(logits_t, bias)


def _moe_body(order_ref, x_ref, ct_ref, tri_ref, *refs, per, sub, cap):
    w1_refs, w3_refs, w2_refs = refs[:per], refs[per:2 * per], refs[2 * per:3 * per]
    sw1_ref, sw3_ref, sw2_ref, x1_ref, g2_ref, fnw_ref, o_ref, acc_ref, rank_ref = refs[3 * per:]
    g = pl.program_id(1)
    tm = x_ref.shape[0]
    ns = tm // sub

    @pl.when(g == 0)
    def _():
        x = x_ref[...]
        hs = _silu(_dot(x, sw1_ref[...])) * _dot(x, sw3_ref[...])
        acc_ref[...] = _dot(hs.astype(BF16), sw2_ref[...])
        for s in range(ns):
            chosen = ct_ref[0, :, s * sub:(s + 1) * sub] > 0.0
            before = _dot(jnp.where(chosen, 1.0, 0.0).astype(BF16), tri_ref[...])
            rank_ref[s] = jnp.where(chosen, before, -1.0)

    ids = [order_ref[pl.program_id(0), g * per + i] for i in range(per)]
    ranks = [[rank_ref[s, pl.ds(ids[i], 1), :] for i in range(per)] for s in range(ns)]
    wts = [[ct_ref[0, pl.ds(ids[i], 1), s * sub:(s + 1) * sub] for i in range(per)] for s in range(ns)]
    top = ranks[0][0]
    for s in range(ns):
        for i in range(per):
            top = jnp.maximum(top, ranks[s][i])
    n_rounds = (jnp.max(top).astype(jnp.int32) + cap) // cap
    slot = lax.broadcasted_iota(jnp.int32, (cap, sub), 0).astype(F32)

    def one_round(r, carry):
        base = slot + (r * cap).astype(F32)
        packed, spread = [], []
        for s in range(ns):
            hits = [base == ranks[s][i] for i in range(per)]
            pack = jnp.concatenate([jnp.where(h, 1.0, 0.0).astype(BF16) for h in hits], axis=0)
            spread.append(jnp.concatenate([jnp.where(h, wts[s][i], 0.0).astype(BF16)
                                           for i, h in enumerate(hits)], axis=0))
            packed.append(_dot(pack, x_ref[s * sub:(s + 1) * sub, :]).astype(BF16))
        outs = []
        for i in range(per):
            ze = jnp.concatenate([packed[s][i * cap:(i + 1) * cap] for s in range(ns)], axis=0)
            h = _silu(_dot(ze, w1_refs[i][0])) * _dot(ze, w3_refs[i][0])
            outs.append(_dot(h.astype(BF16), w2_refs[i][0]).astype(BF16))
        for s in range(ns):
            ys = jnp.concatenate([outs[i][s * cap:(s + 1) * cap] for i in range(per)], axis=0)
            acc_ref[s * sub:(s + 1) * sub, :] += _tn(spread[s], ys)
        return carry

    lax.fori_loop(0, n_rounds, one_round, 0)

    @pl.when(g == pl.num_programs(1) - 1)
    def _():
        y = x1_ref[...] + g2_ref[0] * acc_ref[...]
        ms = jnp.mean(y * y, axis=-1, keepdims=True)
        o_ref[...] = y * lax.rsqrt(ms + EPS) * fnw_ref[...]


def _moe(xn2, comb_t, w1, w3, w2, sw1, sw3, sw2, x1, g2, fnw, tm):
    t, d = xn2.shape
    ne, _, f = w1.shape
    l = comb_t.shape[2]
    per = EXP_PER_STEP
    gpb = l // tm
    sub, cap = MOE_SUB, MOE_CAP
    tri = jnp.asarray(np.triu(np.ones((sub, sub), np.float32), 1), BF16)
    b = comb_t.shape[0]
    load = jnp.max(jnp.sum((comb_t > 0.0).reshape(b, ne, gpb, tm // sub, sub), axis=-1, dtype=jnp.int32), axis=-1)
    order = jnp.argsort(load.transpose(0, 2, 1).reshape(b * gpb, ne), axis=-1).astype(jnp.int32)
    tok = lambda w: pl.BlockSpec((tm, w), lambda i, g, o: (i, 0))
    const = lambda shape: pl.BlockSpec(shape, lambda i, g, o: (0,) * len(shape))

    def expert(shape):
        return [pl.BlockSpec((1,) + shape, lambda i, g, o, k=k: (o[i, g * per + k], 0, 0)) for k in range(per)]

    vm = (2 * tm * d * 2 + 2 * ne * tm * 4 + 2 * 3 * per * d * f * 2 + 2 * 3 * d * f * 2
          + 2 * tm * d * 4 * 2 + tm * d * 4 + 16 * per * cap * (tm // sub) * d)
    grid_spec = pltpu.PrefetchScalarGridSpec(
        num_scalar_prefetch=1,
        grid=(t // tm, ne // per),
        in_specs=([tok(d),
                   pl.BlockSpec((1, ne, tm), lambda i, g, o: (i // gpb, 0, i % gpb)),
                   const(tri.shape)]
                  + expert((d, f)) + expert((d, f)) + expert((f, d))
                  + [const(sw1.shape), const(sw3.shape), const(sw2.shape),
                     tok(d),
                     pl.BlockSpec((1, 1, d), lambda i, g, o: (i // gpb, 0, 0)),
                     const((1, d))]),
        out_specs=tok(d),
        scratch_shapes=[pltpu.VMEM((tm, d), F32), pltpu.VMEM((tm // sub, ne, sub), F32)])
    return pl.pallas_call(
        functools.partial(_moe_body, per=per, sub=sub, cap=cap),
        grid_spec=grid_spec,
        out_shape=jax.ShapeDtypeStruct((t, d), F32),
        compiler_params=_params(("arbitrary", "arbitrary"), vm),
        name="moe",
    )(order, xn2, comb_t, tri, *([w1] * per), *([w3] * per), *([w2] * per), sw1, sw3, sw2, x1, g2, fnw)


def _pad_to(a, rows, cols):
    return jnp.pad(a, ((0, rows - a.shape[0]), (0, cols - a.shape[1])))


def kernel(x, c, ctx, c_ctx, ada_w, ada_b, norm1_w, norm2_w, w_in, hy_conv_w, hy_conv_b, hy_w1, hy_b1, hy_freq, hy_w2, hy_b2, hy_w3, hy_bias, gla_a_w2, gla_a_b, gla_norm_w, proj_hy, proj_gla, w_out, router_w, router_bias, exp_w1, exp_w3, exp_w2, sh_w1, sh_w3, sh_w2, final_norm_w):
    b, l, d = x.shape
    assert ada_w.shape[0] == 1, "single-layer block"
    assert l // GRID_W * GRID_W == l and FFT_N2 == GRID_W
    heads = GLA_HEADS
    qk_w = d // 2
    dk = qk_w // heads
    v_w = d
    dv = v_w // heads
    a_w = 2 * GLA_RANK
    hy_w = d
    hy_cols = (HY_ORDER + 1) * hy_w

    rows = -(-(b + 1) // 8) * 8
    cc = jnp.zeros((rows, d), F32).at[:b].set(c).at[b].set(c_ctx)
    mods = _mods(cc, ada_w[0], ada_b[0][None])
    sh1, sc1, g1, sh2, sc2, g2 = [m[:b, None, :] for m in jnp.split(mods, 6, axis=-1)]
    csh1, csc1 = [jnp.broadcast_to(m[b][None, None, :], (b, 1, d)) for m in jnp.split(mods, 6, axis=-1)[:2]]

    w = w_in[0]
    o_a = qk_w + v_w
    o_q = o_a + a_w
    o_g = o_q + qk_w
    o_hy = o_g + v_w
    o_gate = o_hy + hy_cols
    w_k, w_v = w[:, :qk_w], w[:, qk_w:o_a]
    w_a = jnp.pad(w[:, o_a:o_q], ((0, 0), (0, LANES - a_w)))
    wp = jnp.concatenate([w[:, o_gate:], w[:, o_g:o_hy], w_v, w_k, w[:, o_q:o_g], w[:, o_hy:o_gate], w_a],
                         axis=1).astype(BF16)
    p_gate = 0
    p_g = 2 * d
    p_v = p_g + v_w
    p_k = p_v + v_w
    p_q = p_k + qk_w
    p_hy = p_q + qk_w
    p_a = p_hy + hy_cols
    n_all = p_a + LANES
    w_ctx = jnp.concatenate([w_k, w_v, w_a], axis=1).astype(BF16)
    cols_ctx = (0, qk_w, qk_w + v_w, None)
    cols = (p_k, p_v, p_a, p_q)

    nw1 = norm1_w[0][None]
    u_ctx = _inproj(ctx, nw1, csh1, csc1, w_ctx, ctx.shape[1], w_ctx.shape[1])
    u = _inproj(x, nw1, sh1, sc1, wp, 1024, n_all // 5)

    wa = gla_a_w2[0].reshape(2, GLA_RANK, heads, dk).transpose(0, 2, 1, 3)
    waf = jnp.pad(wa[0], ((0, 0), (0, LANES - GLA_RANK), (0, 0))).astype(BF16)
    wab = jnp.pad(wa[1], ((0, 0), (GLA_RANK, LANES - 2 * GLA_RANK), (0, 0))).astype(BF16)
    ba = gla_a_b[0].reshape(2, heads, 1, dk)
    zeros_state = jnp.zeros((b, heads, dv, dk), F32)
    _, _, s_f, s_b = _gla(u_ctx, cols_ctx, waf, wab, ba[0], ba[1], zeros_state, zeros_state, dk, dv, False,
                          ctx.shape[1])
    o_f, o_b, _, _ = _gla(u, cols, waf, wab, ba[0], ba[1], s_f, s_b, dk, dv, True, GLA_BLOCK)

    n = 2 * l
    max_decay = math.log(HY_TARGET) / HY_FAST_DECAY
    min_decay = math.log(HY_TARGET) / HY_SLOW_DECAY
    deltas = jnp.asarray(np.abs(np.linspace(min_decay, max_decay, hy_w, dtype=np.float32))[None])
    ffn = hy_w1.shape[2]
    kern = _hyfilt(l, hy_w,
                   _pad_to(hy_w1[0], LANES, LANES), _pad_to(hy_b1[0][None], 1, LANES),
                   _pad_to(hy_freq[0, 0][None], 1, LANES),
                   _pad_to(hy_w2[0], LANES, LANES), _pad_to(hy_b2[0][None], 1, LANES),
                   _pad_to(hy_freq[0, 1][None], 1, LANES),
                   jnp.pad(hy_w3[0], ((0, LANES - ffn), (0, 0))), deltas)
    tabs = _fft_tables(n, FFT_N2)
    spec = _hyspec(kern, tabs[1], tabs[2], FFT_N2)
    cw, cb = hy_conv_w[0], hy_conv_b[0][None]
    z1 = _hyconv(u, p_hy, u, p_hy + hy_w, cw, cb, 0, hy_w, spec, 0, hy_bias[0, 0][None], tabs, True)
    y_hy = _hyconv(z1, 0, u, p_hy + 2 * hy_w, cw, cb, 0, 2 * hy_w, spec, 1, hy_bias[0, 1][None], tabs, False)

    x1, xn2, logits_t = _mix(x, o_f, o_b, u, p_g, p_gate, y_hy, gla_norm_w[0][None],
                             proj_hy[0].astype(BF16), proj_gla[0].astype(BF16), w_out[0].astype(BF16),
                             g1, norm2_w[0][None], sh2, sc2, router_w[0].T, 512)
    comb = _select(logits_t, router_bias[0].reshape(N_GROUPS, N_EXPERTS // N_GROUPS, 1), 512)
    out = _moe(xn2.reshape(b * l, d), comb, exp_w1[0].astype(BF16), exp_w3[0].astype(BF16),
               exp_w2[0].astype(BF16), sh_w1[0].astype(BF16), sh_w3[0].astype(BF16), sh_w2[0].astype(BF16),
               x1.reshape(b * l, d), g2, final_norm_w[None], 1024)
    return out.reshape(b, l, d)
```

```python
import functools
import math

import jax
import jax.numpy as jnp
import numpy as np
from jax import lax
from jax.experimental import pallas as pl
from jax.experimental.pallas import tpu as pltpu

F32 = jnp.float32
BF16 = jnp.bfloat16
HIGHEST = lax.Precision.HIGHEST

GRID_W = 64
EPS = 1e-6
HY_ORDER = 2
HY_BANDS = 16
HY_FAST_DECAY = 0.3
HY_SLOW_DECAY = 1.5
HY_TARGET = 1e-2
GLA_HEADS = 4
GLA_RANK = 16
GLA_TAU = 16.0
N_EXPERTS = 64
N_GROUPS = 8
TOPK_GROUPS = 4
TOP_K = 8
ROUTED_SCALE = 2.5

LANES = 128
V7X_VMEM_BYTES = 64 * 1024 * 1024
VMEM_CAP_BYTES = 56 * 1024 * 1024

GLA_CHUNK = 256
GLA_BLOCK = 512
GLA_VPU_MIN_HALF = 4
FFT_N2 = 64
STRIDE_PAD = 8
STAGE_UNROLL = 8
ROW_UNROLL = 2
EXP_PER_STEP = 4
MOE_SUB = 256
MOE_CAP = 64


def _params(sem, vmem_bytes):
    limit = int(min(VMEM_CAP_BYTES, max(16 * 1024 * 1024, vmem_bytes * 5 // 4 + (2 << 20))))
    return pltpu.CompilerParams(dimension_semantics=sem, vmem_limit_bytes=limit)


def _nt(a, b, **kw):
    return lax.dot_general(a, b, (((1,), (1,)), ((), ())), preferred_element_type=F32, **kw)


def _tn(a, b):
    return lax.dot_general(a, b, (((0,), (0,)), ((), ())), preferred_element_type=F32)


def _dot(a, b, **kw):
    return jnp.dot(a, b, preferred_element_type=F32, **kw)


def _silu(x):
    return x * jax.nn.sigmoid(x)


def _mods_body(c_ref, w_ref, b_ref, o_ref):
    o_ref[...] = _dot(_silu(c_ref[...]), w_ref[...], precision=HIGHEST) + b_ref[...]


def _mods(cc, w, b):
    rows, d = cc.shape
    n = w.shape[1]
    tn = n // 4
    return pl.pallas_call(
        _mods_body,
        grid=(n // tn,),
        in_specs=[pl.BlockSpec((rows, d), lambda j: (0, 0)),
                  pl.BlockSpec((d, tn), lambda j: (0, j)),
                  pl.BlockSpec((1, tn), lambda j: (0, j))],
        out_specs=pl.BlockSpec((rows, tn), lambda j: (0, j)),
        out_shape=jax.ShapeDtypeStruct((rows, n), F32),
        compiler_params=_params(("arbitrary",), 2 * d * tn * 4),
        name="mods",
    )(cc, w, b)


def _norm_mod(x, w, shift, scale):
    ms = jnp.mean(x * x, axis=-1, keepdims=True)
    return (x * lax.rsqrt(ms + EPS) * w) * (1.0 + scale) + shift


def _inproj_body(x_ref, nw_ref, sh_ref, sc_ref, w_ref, o_ref, xn_ref):
    @pl.when(pl.program_id(2) == 0)
    def _():
        xn_ref[...] = _norm_mod(x_ref[0], nw_ref[...], sh_ref[0], sc_ref[0]).astype(BF16)

    o_ref[0] = _dot(xn_ref[...], w_ref[...]).astype(BF16)


def _inproj(x, nw, shift, scale, w, tm, tn):
    b, l, d = x.shape
    n = w.shape[1]
    vm = 2 * tm * d * 4 + 2 * d * tn * 2 + 2 * tm * tn * 2 + tm * d * 2
    return pl.pallas_call(
        _inproj_body,
        grid=(b, l // tm, n // tn),
        in_specs=[pl.BlockSpec((1, tm, d), lambda bi, i, j: (bi, i, 0)),
                  pl.BlockSpec((1, d), lambda bi, i, j: (0, 0)),
                  pl.BlockSpec((1, 1, d), lambda bi, i, j: (bi, 0, 0)),
                  pl.BlockSpec((1, 1, d), lambda bi, i, j: (bi, 0, 0)),
                  pl.BlockSpec((d, tn), lambda bi, i, j: (0, j))],
        out_specs=pl.BlockSpec((1, tm, tn), lambda bi, i, j: (bi, i, j)),
        out_shape=jax.ShapeDtypeStruct((b, l, n), BF16),
        scratch_shapes=[pltpu.VMEM((tm, d), BF16)],
        compiler_params=_params(("arbitrary", "arbitrary", "arbitrary"), vm),
        name="inproj",
    )(x, nw, shift, scale, w)


def _gla_tables(c, inclusive, flip):
    idx = np.arange(c)
    i = idx[:, None]
    x = idx[None, :]
    blocks = [x <= i]
    masks = []
    h = c // 2
    while h >= 1:
        mid = (idx // (2 * h)) * (2 * h) + h
        mi = mid[:, None]
        hi = i if inclusive else i - 1
        if h < GLA_VPU_MIN_HALF:
            blocks.append(((i >= mi) & (x >= mi) & (x <= hi)) | ((i < mi) & (x > i) & (x <= mi - 1)))
        same = (idx[:, None] // (2 * h)) == (idx[None, :] // (2 * h))
        masks.append(same & (idx[:, None] >= mi) & (idx[None, :] < mid[None, :]))
        h //= 2
    masks.append(np.eye(c, dtype=bool))
    if flip:
        blocks = [b[::-1, ::-1] for b in blocks]
        masks = [m[::-1, ::-1] for m in masks]
    lall = np.concatenate(blocks + [np.ones((8, c), bool)], axis=0)
    return lall.astype(np.float32), np.stack(masks).astype(np.float32)


def _gla_chunk(q, k, v, a, wa, ba, lall, masks_ref, st_ref, inclusive, flip, q_scale):
    c, dk = k.shape
    n_levels = int(math.log2(c))
    xg = _dot(a, wa) + ba
    g = (jnp.minimum(xg, 0.0) - jnp.log(1.0 + jnp.exp(-jnp.abs(xg)))) * (1.0 / GLA_TAU)
    g_hi = g.astype(BF16)
    g_lo = (g - g_hi.astype(F32)).astype(BF16)
    e2 = _dot(lall, jnp.concatenate([g_hi, g_lo], axis=1))
    e = e2[:, :dk] + e2[:, dk:]
    run = e[0:c]
    tot = e[e.shape[0] - 8:e.shape[0] - 7]
    upto = run if inclusive else run - g

    def decay(t):
        return jnp.exp(jnp.minimum(t, 0.0))

    row = lax.broadcasted_iota(jnp.int32, (c, dk), 0)
    ex_levels = []
    h = c // 2
    while h >= 1:
        if h >= GLA_VPU_MIN_HALF:
            piv = jnp.concatenate([jnp.broadcast_to(run[m + (h if flip else h - 1):m + (h if flip else h - 1) + 1, :],
                                                    (2 * h, dk)) for m in range(0, c, 2 * h)], axis=0)
            if inclusive:
                ex_levels.append(jnp.exp(-jnp.abs(run - piv)))
            else:
                key_side = ((row & (2 * h - 1)) >= h) if flip else ((row & (2 * h - 1)) < h)
                ex_levels.append(decay(jnp.where(key_side, piv - run, upto - piv)))
        else:
            lo = (1 + len(ex_levels) - (n_levels - int(math.log2(GLA_VPU_MIN_HALF)))) * c
            ex_levels.append(decay(e[lo:lo + c]))
        h //= 2

    kf = k.astype(F32)
    st = st_ref[...]
    k1 = (kf * decay(tot - run)).astype(BF16)
    st_ref[...] = st * decay(tot) + _tn(v, k1)
    if q is None:
        return None
    qf = q.astype(F32) * q_scale
    o = _nt((qf * decay(upto)).astype(BF16), st.astype(BF16))
    attn = jnp.zeros((c, c), BF16)
    for lv in range(n_levels):
        ex_l = ex_levels[lv]
        attn = attn + _nt((qf * ex_l).astype(BF16), (kf * ex_l).astype(BF16)).astype(BF16) * masks_ref[lv]
    if inclusive:
        attn = attn + _nt(qf.astype(BF16), k).astype(BF16) * masks_ref[n_levels]
    return o + _dot(attn, v)


def _gla_body(*refs, with_q, n_sub, chunk, q_scale):
    if with_q:
        (kf_ref, vf_ref, af_ref, qf_ref, kb_ref, vb_ref, ab_ref, qb_ref, waf_ref, wab_ref, baf_ref, bab_ref,
         lf_ref, lb_ref, mf_ref, mb_ref, s0f_ref, s0b_ref, of_ref, ob_ref, sf_ref, sb_ref, stf_ref, stb_ref) = refs
    else:
        (kf_ref, vf_ref, af_ref, kb_ref, vb_ref, ab_ref, waf_ref, wab_ref, baf_ref, bab_ref,
         lf_ref, lb_ref, mf_ref, mb_ref, s0f_ref, s0b_ref, sf_ref, sb_ref, stf_ref, stb_ref) = refs
        qf_ref = qb_ref = of_ref = ob_ref = None

    @pl.when(pl.program_id(2) == 0)
    def _():
        stf_ref[...] = s0f_ref[0, 0]
        stb_ref[...] = s0b_ref[0, 0]

    for s in range(n_sub):
        sl = slice(s * chunk, (s + 1) * chunk)
        o = _gla_chunk(None if qf_ref is None else qf_ref[0, sl, :], kf_ref[0, sl, :], vf_ref[0, sl, :],
                       af_ref[0, sl, :], waf_ref[0], baf_ref[0], lf_ref[...], mf_ref, stf_ref, True, False, q_scale)
        if with_q:
            of_ref[0, sl, :] = o.astype(BF16)
    for s in reversed(range(n_sub)):
        sl = slice(s * chunk, (s + 1) * chunk)
        o = _gla_chunk(None if qb_ref is None else qb_ref[0, sl, :], kb_ref[0, sl, :], vb_ref[0, sl, :],
                       ab_ref[0, sl, :], wab_ref[0], bab_ref[0], lb_ref[...], mb_ref, stb_ref, False, True, q_scale)
        if with_q:
            ob_ref[0, sl, :] = o.astype(BF16)
    sf_ref[0, 0] = stf_ref[...]
    sb_ref[0, 0] = stb_ref[...]


def _gla(u, cols, waf, wab, baf, bab, s0f, s0b, dk, dv, with_q, tb):
    b, l, _ = u.shape
    h = GLA_HEADS
    nb = l // tb
    n_sub = tb // GLA_CHUNK
    lf, mf = _gla_tables(GLA_CHUNK, True, False)
    lb, mb = _gla_tables(GLA_CHUNK, False, True)
    lf, lb = jnp.asarray(lf, BF16), jnp.asarray(lb, BF16)
    mf, mb = jnp.asarray(mf, BF16), jnp.asarray(mb, BF16)
    kc, vc, ac, qc = cols

    def seq_specs(rev):
        def blk(i):
            return (nb - 1 - i) if rev else i
        specs = [pl.BlockSpec((1, tb, dk), lambda bi, hi, i: (bi, blk(i), kc // dk + hi)),
                 pl.BlockSpec((1, tb, dv), lambda bi, hi, i: (bi, blk(i), vc // dv + hi)),
                 pl.BlockSpec((1, tb, LANES), lambda bi, hi, i: (bi, blk(i), ac // LANES))]
        if with_q:
            specs.append(pl.BlockSpec((1, tb, dk), lambda bi, hi, i: (bi, blk(i), qc // dk + hi)))
        return specs

    def const_spec(shape):
        nd = len(shape)
        return pl.BlockSpec(shape, lambda bi, hi, i: (0,) * nd)

    head_w = pl.BlockSpec((1, LANES, dk), lambda bi, hi, i: (hi, 0, 0))
    head_b = pl.BlockSpec((1, 1, dk), lambda bi, hi, i: (hi, 0, 0))
    st_spec = pl.BlockSpec((1, 1, dv, dk), lambda bi, hi, i: (bi, hi, 0, 0))
    in_specs = (seq_specs(False) + seq_specs(True) + [head_w, head_w, head_b, head_b,
                const_spec(lf.shape), const_spec(lb.shape), const_spec(mf.shape), const_spec(mb.shape),
                st_spec, st_spec])
    st_shape = jax.ShapeDtypeStruct((b, h, dv, dk), F32)
    if with_q:
        o_shape = jax.ShapeDtypeStruct((b, l, h * dv), BF16)
        out_shape = (o_shape, o_shape, st_shape, st_shape)
        out_specs = (pl.BlockSpec((1, tb, dv), lambda bi, hi, i: (bi, i, hi)),
                     pl.BlockSpec((1, tb, dv), lambda bi, hi, i: (bi, nb - 1 - i, hi)),
                     st_spec, st_spec)
        args = (u,) * 8
    else:
        out_shape = (st_shape, st_shape)
        out_specs = (st_spec, st_spec)
        args = (u,) * 6
    vm = 4 * tb * (2 * dk + dv + LANES) * 2 * 2 + 8 * dv * dk * 4 + 4 * tb * dv * 2 + (4 << 20)
    outs = pl.pallas_call(
        functools.partial(_gla_body, with_q=with_q, n_sub=n_sub, chunk=GLA_CHUNK, q_scale=dk ** -0.5),
        grid=(b, h, nb),
        in_specs=in_specs,
        out_specs=out_specs,
        out_shape=out_shape,
        scratch_shapes=[pltpu.VMEM((dv, dk), F32), pltpu.VMEM((dv, dk), F32)],
        compiler_params=_params(("arbitrary", "arbitrary", "arbitrary"), vm),
        name="gla" if with_q else "gla_ctx",
    )(*args, waf, wab, baf, bab, lf, lb, mf, mb, s0f, s0b)
    if with_q:
        return outs
    return None, None, outs[0], outs[1]


def _hy_tables(l):
    t = np.linspace(0.0, 1.0, l, dtype=np.float32).astype(np.float64)[:, None]
    w = 2.0 * math.pi * np.arange(l, dtype=np.float64)[:, None] / l
    f = np.linspace(1e-4, HY_BANDS - 1, HY_BANDS, dtype=np.float32).astype(np.float64)[None, :]
    z = np.concatenate([t, np.cos(f * w), -np.sin(f * w)], axis=-1)
    rev = (l - np.arange(l)) % l

    def pad(a):
        out = np.zeros((l, LANES), np.float32)
        out[:, :a.shape[1]] = a
        return out

    tt = np.broadcast_to(t, (l, LANES)).astype(np.float32)
    return pad(z), pad(z[rev]), tt, np.ascontiguousarray(tt[rev])


def _hyfilt_body(z1_ref, z2_ref, t1_ref, t2_ref, w1_ref, b1_ref, f1_ref, w2_ref, b2_ref, f2_ref,
                 w3f_ref, w3b_ref, dl_ref, o_ref, ha_ref, hb_ref):
    l = z1_ref.shape[0]

    @pl.when((pl.program_id(0) == 0) & (pl.program_id(1) == 0))
    def _():
        for z_ref, h_ref in ((z1_ref, ha_ref), (z2_ref, hb_ref)):
            h = jnp.sin(f1_ref[...] * (_dot(z_ref[...], w1_ref[...], precision=HIGHEST) + b1_ref[...]))
            h_ref[...] = jnp.sin(f2_ref[...] * (_dot(h, w2_ref[...], precision=HIGHEST) + b2_ref[...]))

    dl = dl_ref[...]
    hf = _dot(ha_ref[...], w3f_ref[...], precision=HIGHEST) * jnp.exp(-t1_ref[...] * dl)
    hb = _dot(hb_ref[...], w3b_ref[...], precision=HIGHEST) * jnp.exp(-t2_ref[...] * dl)
    row = lax.broadcasted_iota(jnp.int32, hb.shape, 0)
    hb = jnp.where(row == 0, 0.0, hb)
    ss = jnp.sum(hf * hf, axis=0, keepdims=True) + jnp.sum(hb * hb, axis=0, keepdims=True)
    scale = lax.rsqrt(ss)
    o_ref[0, 0:l, :] = hf * scale
    o_ref[0, l:2 * l, :] = hb * scale


def _hyfilt(l, c, w1, b1, f1, w2, b2, f2, w3, deltas):
    z1, z2, t1, t2 = (jnp.asarray(a) for a in _hy_tables(l))
    ncb = c // LANES
    tab = pl.BlockSpec((l, LANES), lambda o, j: (0, 0))
    sq = pl.BlockSpec((LANES, LANES), lambda o, j: (0, 0))
    row = pl.BlockSpec((1, LANES), lambda o, j: (0, 0))
    return pl.pallas_call(
        _hyfilt_body,
        grid=(HY_ORDER, ncb),
        in_specs=[tab, tab, tab, tab, sq, row, row, sq, row, row,
                  pl.BlockSpec((LANES, LANES), lambda o, j: (0, 2 * o * ncb + j)),
                  pl.BlockSpec((LANES, LANES), lambda o, j: (0, (2 * o + 1) * ncb + j)),
                  pl.BlockSpec((1, LANES), lambda o, j: (0, j))],
        out_specs=pl.BlockSpec((1, 2 * l, LANES), lambda o, j: (o, 0, j)),
        out_shape=jax.ShapeDtypeStruct((HY_ORDER, 2 * l, c), F32),
        scratch_shapes=[pltpu.VMEM((l, LANES), F32), pltpu.VMEM((l, LANES), F32)],
        compiler_params=_params(("arbitrary", "arbitrary"), 24 * l * LANES * 4),
        name="hyfilt",
    )(z1, z2, t1, t2, w1, b1, f1, w2, b2, f2, w3, w3, deltas)


def _fft_tables(n, n2):
    n1 = n // n2
    h = n1 // 2
    k1 = np.arange(n1)[:, None]
    a = 2.0 * math.pi * k1 * np.arange(h)[None, :] / n1
    c, s = np.cos(a), np.sin(a)
    f1c = np.block([[c, s], [-s, c]])
    a = 2.0 * math.pi * k1 * np.arange(n1)[None, :] / n1
    f1r = np.concatenate([np.cos(a), -np.sin(a)], axis=0)
    kk = np.arange(n1)[:, None, None] + n1 * np.arange(n2)[None, :, None]
    a = 2.0 * math.pi * kk * np.arange(n2)[None, None, :] / n
    c, s = np.cos(a), np.sin(a)
    gf = np.concatenate([np.concatenate([c, s], axis=2), np.concatenate([-s, c], axis=2)], axis=1)
    a = 2.0 * math.pi * np.arange(h)[:, None] * np.arange(n1)[None, :] / n1
    c, s = np.cos(a), np.sin(a)
    if1 = np.block([[c, -s], [s, c]])
    return tuple(jnp.asarray(m, BF16) for m in (f1c, f1r, gf, if1))


def _rows8(start, size):
    return pl.ds(pl.multiple_of(start, 8), size)


def _hyspec_body(k_ref, f1_ref, gf_ref, o_ref, as_ref, *, n, n2, ap):
    n1 = n // n2

    def stage1(j, carry):
        r = k_ref[0, pl.ds(j, n1, stride=n2), :].astype(BF16)
        as_ref[_rows8(j * ap, 2 * n1), :] = _dot(f1_ref[...], r)
        return carry

    lax.fori_loop(0, n2, stage1, 0, unroll=STAGE_UNROLL)

    def stage2(k1, carry):
        r = jnp.concatenate([as_ref[pl.ds(k1, n2, stride=ap), :],
                             as_ref[pl.ds(n1 + k1, n2, stride=ap), :]], axis=0).astype(BF16)
        o_ref[0, k1] = (_dot(gf_ref[k1], r) * (1.0 / n)).astype(BF16)
        return carry

    lax.fori_loop(0, n1, stage2, 0, unroll=STAGE_UNROLL)


def _hyspec(kern, f1r, gf, n2):
    order, n, c = kern.shape
    n1 = n // n2
    ap = 2 * n1 + STRIDE_PAD
    return pl.pallas_call(
        functools.partial(_hyspec_body, n=n, n2=n2, ap=ap),
        grid=(order, c // LANES),
        in_specs=[pl.BlockSpec((1, n, LANES), lambda o, j: (o, 0, j)),
                  pl.BlockSpec(f1r.shape, lambda o, j: (0, 0)),
                  pl.BlockSpec(gf.shape, lambda o, j: (0, 0, 0))],
        out_specs=pl.BlockSpec((1, n1, 2 * n2, LANES), lambda o, j: (o, 0, 0, j)),
        out_shape=jax.ShapeDtypeStruct((order, n1, 2 * n2, c), BF16),
        scratch_shapes=[pltpu.VMEM((n2 * ap, LANES), F32)],
        compiler_params=_params(("arbitrary", "arbitrary"),
                                2 * n * LANES * 4 + 2 * gf.size * 2 + n2 * ap * LANES * 4 + 2 * n * LANES * 2),
        name="hyspec",
    )(kern, f1r, gf)


def _short_conv(u, w_ref, b_ref):
    r = u.shape[0]
    row = lax.broadcasted_iota(jnp.int32, u.shape, 0)
    up = jnp.where(row == 0, 0.0, pltpu.roll(u, 1, axis=0))
    dn = jnp.where(row == r - 1, 0.0, pltpu.roll(u, r - 1, axis=0))
    return up * w_ref[0:1, :] + u * w_ref[1:2, :] + dn * w_ref[2:3, :] + b_ref[...]


def _hyconv_body(z_ref, g_ref, zw_ref, zb_ref, gw_ref, gb_ref, sp_ref, hb_ref, f1_ref, gf_ref, if1_ref,
                 o_ref, x_ref, as_ref, bs_ref, y_ref, *, conv_z, n2, xp, ap):
    n1h = z_ref.shape[1] // n2
    n1 = 2 * n1h
    half = n1h * xp

    def fill(i, carry):
        for p in range(2):
            u = z_ref[p, pl.ds(pl.multiple_of(i * n2, n2), n2), :].astype(F32)
            x_ref[_rows8(p * half + i * xp, n2), :] = _short_conv(u, zw_ref, zb_ref) if conv_z else u
        return carry

    lax.fori_loop(0, n1h, fill, 0, unroll=ROW_UNROLL)

    def stage1(j, carry):
        r = jnp.concatenate([x_ref[pl.ds(j, n1h, stride=xp), :],
                             x_ref[pl.ds(half + j, n1h, stride=xp), :]], axis=0).astype(BF16)
        as_ref[_rows8(j * ap, 2 * n1), :] = _dot(f1_ref[...], r)
        return carry

    lax.fori_loop(0, n2, stage1, 0, unroll=STAGE_UNROLL)

    def stage2(k1, carry):
        r = jnp.concatenate([as_ref[pl.ds(k1, n2, stride=ap), :],
                             as_ref[pl.ds(n1 + k1, n2, stride=ap), :]], axis=0).astype(BF16)
        gk = gf_ref[k1]
        xk = _dot(gk, r)
        xr, xi = xk[0:n2], xk[n2:2 * n2]
        sp = sp_ref[0, k1].astype(F32)
        sr, si = sp[0:n2], sp[n2:2 * n2]
        yk = jnp.concatenate([xr * sr - xi * si, xr * si + xi * sr], axis=0).astype(BF16)
        bk = _tn(gk, yk)
        bs_ref[pl.ds(k1, n2, stride=ap), :] = bk[0:n2]
        bs_ref[pl.ds(n1 + k1, n2, stride=ap), :] = bk[n2:2 * n2]
        return carry

    lax.fori_loop(0, n1, stage2, 0, unroll=2 * STAGE_UNROLL)

    def stage3(j, carry):
        yn = _dot(if1_ref[...], bs_ref[_rows8(j * ap, 2 * n1), :].astype(BF16))
        y_ref[pl.ds(j, n1h, stride=xp), :] = yn[0:n1h]
        y_ref[pl.ds(half + j, n1h, stride=xp), :] = yn[n1h:n1]
        return carry

    lax.fori_loop(0, n2, stage3, 0, unroll=STAGE_UNROLL)

    def finish(i, carry):
        rows = pl.ds(pl.multiple_of(i * n2, n2), n2)
        for p in range(2):
            gate = _short_conv(g_ref[p, rows, :].astype(F32), gw_ref, gb_ref)
            z = x_ref[_rows8(p * half + i * xp, n2), :]
            y = y_ref[_rows8(p * half + i * xp, n2), :]
            o_ref[p, rows, :] = (gate * (y + z * hb_ref[...])).astype(BF16)
        return carry

    lax.fori_loop(0, n1h, finish, 0, unroll=ROW_UNROLL)


def _hyconv(z, z_col, g, g_col, conv_w, conv_b, zw_col, gw_col, spec, order, hy_bias, tabs, conv_z):
    b, l, _ = z.shape
    f1c, _, gf, if1 = tabs
    n1, n2x2 = gf.shape[0], gf.shape[1]
    n2 = n2x2 // 2
    n1h = n1 // 2
    c = spec.shape[-1]
    ncb = c // LANES
    xp = n2 + STRIDE_PAD
    ap = 2 * n1 + STRIDE_PAD
    vm = (2 * 2 * 2 * l * LANES * 2 + 2 * 2 * l * LANES * 2 + 2 * n1 * n2x2 * LANES * 2
          + 2 * gf.size * 2 + 2 * 2 * n1h * xp * LANES * 4 + 2 * n2 * ap * LANES * 4)
    return pl.pallas_call(
        functools.partial(_hyconv_body, conv_z=conv_z, n2=n2, xp=xp, ap=ap),
        grid=(ncb, b // 2),
        in_specs=[pl.BlockSpec((2, l, LANES), lambda j, p: (p, 0, z_col // LANES + j)),
                  pl.BlockSpec((2, l, LANES), lambda j, p: (p, 0, g_col // LANES + j)),
                  pl.BlockSpec((3, LANES), lambda j, p: (0, zw_col // LANES + j)),
                  pl.BlockSpec((1, LANES), lambda j, p: (0, zw_col // LANES + j)),
                  pl.BlockSpec((3, LANES), lambda j, p: (0, gw_col // LANES + j)),
                  pl.BlockSpec((1, LANES), lambda j, p: (0, gw_col // LANES + j)),
                  pl.BlockSpec((1, n1, n2x2, LANES), lambda j, p: (order, 0, 0, j)),
                  pl.BlockSpec((1, LANES), lambda j, p: (0, j)),
                  pl.BlockSpec(f1c.shape, lambda j, p: (0, 0)),
                  pl.BlockSpec(gf.shape, lambda j, p: (0, 0, 0)),
                  pl.BlockSpec(if1.shape, lambda j, p: (0, 0))],
        out_specs=pl.BlockSpec((2, l, LANES), lambda j, p: (p, 0, j)),
        out_shape=jax.ShapeDtypeStruct((b, l, c), BF16),
        scratch_shapes=[pltpu.VMEM((2 * n1h * xp, LANES), F32),
                        pltpu.VMEM((n2 * ap, LANES), F32),
                        pltpu.VMEM((n2 * ap, LANES), F32),
                        pltpu.VMEM((2 * n1h * xp, LANES), F32)],
        compiler_params=_params(("arbitrary", "arbitrary"), vm),
        name="hyconv%d" % order,
    )(z, g, conv_w, conv_b, conv_w, conv_b, spec, hy_bias, f1c, gf, if1)


def _mix_body(x_ref, of_ref, ob_ref, ug_ref, ugate_ref, yhy_ref, gnw_ref, phy_ref, pgla_ref, wout_ref,
              g1_ref, n2w_ref, sh2_ref, sc2_ref, rwt_ref, x1_ref, xn2_ref, lg_ref, *, heads):
    d = x_ref.shape[2]
    o = of_ref[0].astype(F32) + ob_ref[0].astype(F32)
    dv = o.shape[1] // heads
    parts = []
    for h in range(heads):
        seg = o[:, h * dv:(h + 1) * dv]
        parts.append(seg * lax.rsqrt(jnp.mean(seg * seg, axis=-1, keepdims=True) + EPS))
    y_gla = jnp.concatenate(parts, axis=1) * gnw_ref[...] * _silu(ug_ref[0].astype(F32))
    gates = jax.nn.sigmoid(ugate_ref[0].astype(F32))
    merged = (gates[:, :d] * _dot(yhy_ref[0], phy_ref[...])
              + gates[:, d:] * _dot(y_gla.astype(BF16), pgla_ref[...]))
    x1 = x_ref[0] + g1_ref[0] * _dot(merged.astype(BF16), wout_ref[...])
    x1_ref[0] = x1
    xn2 = _norm_mod(x1, n2w_ref[...], sh2_ref[0], sc2_ref[0])
    xn2_ref[0] = xn2.astype(BF16)
    lg_ref[0] = _nt(rwt_ref[...], xn2, precision=HIGHEST)


def _mix(x, o_f, o_b, u, g_col, gate_col, y_hy, gnw, phy, pgla, wout, g1, n2w, sh2, sc2, rwt, tm):
    b, l, d = x.shape
    vw = o_f.shape[2]
    ne = rwt.shape[0]
    tok = lambda w: pl.BlockSpec((1, tm, w), lambda bi, i: (bi, i, 0))
    per_b = pl.BlockSpec((1, 1, d), lambda bi, i: (bi, 0, 0))
    const = lambda shape: pl.BlockSpec(shape, lambda bi, i: (0, 0))
    vm = (2 * tm * d * 4 * 2 + 2 * tm * (3 * vw + 2 * d + 2 * d) * 2 + 2 * 3 * d * d * 2 + 12 * tm * d * 4)
    return pl.pallas_call(
        functools.partial(_mix_body, heads=GLA_HEADS),
        grid=(b, l // tm),
        in_specs=[tok(d), tok(vw), tok(vw),
                  pl.BlockSpec((1, tm, vw), lambda bi, i: (bi, i, g_col // vw)),
                  pl.BlockSpec((1, tm, 2 * d), lambda bi, i: (bi, i, gate_col // (2 * d))),
                  tok(d), const((1, vw)), const((d, d)), const((vw, d)), const((d, d)),
                  per_b, const((1, d)), per_b, per_b, const((ne, d))],
        out_specs=(tok(d), tok(d), pl.BlockSpec((1, ne, tm), lambda bi, i: (bi, 0, i))),
        out_shape=(jax.ShapeDtypeStruct((b, l, d), F32), jax.ShapeDtypeStruct((b, l, d), BF16),
                   jax.ShapeDtypeStruct((b, ne, l), F32)),
        compiler_params=_params(("arbitrary", "arbitrary"), vm),
        name="mix",
    )(x, o_f, o_b, u, u, y_hy, gnw, phy, pgla, wout, g1, n2w, sh2, sc2, rwt)


def _select_body(lg_ref, bias_ref, o_ref, n_ref):
    ne, tn = lg_ref.shape[1], lg_ref.shape[2]
    ng = N_GROUPS
    pg = ne // ng
    scores = jax.nn.sigmoid(lg_ref[0]).reshape(ng, pg, tn)
    sel = scores + bias_ref[...]
    ie = lax.broadcasted_iota(jnp.int32, sel.shape, 1)
    m1 = jnp.max(sel, axis=1, keepdims=True)
    i1 = jnp.min(jnp.where(sel == m1, ie, pg), axis=1, keepdims=True)
    m2 = jnp.max(jnp.where(ie == i1, -jnp.inf, sel), axis=1, keepdims=True)
    grp = m1 + m2
    ig = lax.broadcasted_iota(jnp.int32, grp.shape, 0)
    rank = jnp.zeros(grp.shape, jnp.int32)
    for g in range(ng):
        other = grp[g:g + 1]
        rank = rank + jnp.where((other > grp) | ((other == grp) & (g < ig)), 1, 0)
    cand = jnp.where(rank < TOPK_GROUPS, sel, -jnp.inf)
    flat = ig * pg + ie
    rank = jnp.zeros(sel.shape, jnp.int32)
    for g in range(ng):
        for e in range(pg):
            other = cand[g:g + 1, e:e + 1, :]
            rank = rank + jnp.where((other > cand) | ((other == cand) & (g * pg + e < flat)), 1, 0)
    w = jnp.where(rank < TOP_K, scores, 0.0)
    tot = jnp.sum(jnp.sum(w, axis=1, keepdims=True), axis=0, keepdims=True)
    comb = (w / tot * ROUTED_SCALE).reshape(ne, tn)
    o_ref[0] = comb
    sub = tn // n_ref.shape[1]
    for s in range(n_ref.shape[1]):
        n_ref[0, s] = jnp.sum(jnp.where(comb[:, s * sub:(s + 1) * sub] > 0.0, 1.0, 0.0), axis=1, keepdims=True)


def _select(logits_t, bias, tn, sub):
    b, ne, l = logits_t.shape
    nb = l // tn
    return pl.pallas_call(
        _select_body,
        grid=(b, nb),
        in_specs=[pl.BlockSpec((1, ne, tn), lambda bi, i: (bi, 0, i)),
                  pl.BlockSpec(bias.shape, lambda bi, i: (0, 0, 0))],
        out_specs=(pl.BlockSpec((1, ne, tn), lambda bi, i: (bi, 0, i)),
                   pl.BlockSpec((1, tn // sub, ne, 1), lambda bi, i: (bi, i, 0, 0))),
        out_shape=(jax.ShapeDtypeStruct((b, ne, l), F32), jax.ShapeDtypeStruct((b, l // sub, ne, 1), F32)),
        compiler_params=_params(("arbitrary", "arbitrary"), 64 * ne * tn * 4),
        name="select",
    )(logits_t, bias)


def _moe_body(order_ref, x_ref, ct_ref, tri_ref, *refs, per, sub, cap):
    w1_refs, w3_refs, w2_refs = refs[:per], refs[per:2 * per], refs[2 * per:3 * per]
    sw1_ref, sw3_ref, sw2_ref, x1_ref, g2_ref, fnw_ref, o_ref, acc_ref, rank_ref = refs[3 * per:]
    g = pl.program_id(1)
    tm = x_ref.shape[0]
    ns = tm // sub

    @pl.when(g == 0)
    def _():
        x = x_ref[...]
        hs = _silu(_dot(x, sw1_ref[...])) * _dot(x, sw3_ref[...])
        acc_ref[...] = _dot(hs.astype(BF16), sw2_ref[...])
        for s in range(ns):
            chosen = ct_ref[0, :, s * sub:(s + 1) * sub] > 0.0
            before = _dot(jnp.where(chosen, 1.0, 0.0).astype(BF16), tri_ref[...])
            rank_ref[s] = jnp.where(chosen, before, -1.0)

    ids = [order_ref[pl.program_id(0), g * per + i] for i in range(per)]
    ranks = [[rank_ref[s, pl.ds(ids[i], 1), :] for i in range(per)] for s in range(ns)]
    wts = [[ct_ref[0, pl.ds(ids[i], 1), s * sub:(s + 1) * sub] for i in range(per)] for s in range(ns)]
    top = ranks[0][0]
    for s in range(ns):
        for i in range(per):
            top = jnp.maximum(top, ranks[s][i])
    n_rounds = (jnp.max(top).astype(jnp.int32) + cap) // cap
    slot = lax.broadcasted_iota(jnp.int32, (cap, sub), 0).astype(F32)

    def one_round(r, carry):
        base = slot + (r * cap).astype(F32)
        packed, spread = [], []
        for s in range(ns):
            hits = [base == ranks[s][i] for i in range(per)]
            pack = jnp.concatenate([jnp.where(h, 1.0, 0.0).astype(BF16) for h in hits], axis=0)
            spread.append(jnp.concatenate([jnp.where(h, wts[s][i], 0.0).astype(BF16)
                                           for i, h in enumerate(hits)], axis=0))
            packed.append(_dot(pack, x_ref[s * sub:(s + 1) * sub, :]).astype(BF16))
        outs = []
        for i in range(per):
            ze = jnp.concatenate([packed[s][i * cap:(i + 1) * cap] for s in range(ns)], axis=0)
            h = _silu(_dot(ze, w1_refs[i][0])) * _dot(ze, w3_refs[i][0])
            outs.append(_dot(h.astype(BF16), w2_refs[i][0]).astype(BF16))
        for s in range(ns):
            ys = jnp.concatenate([outs[i][s * cap:(s + 1) * cap] for i in range(per)], axis=0)
            acc_ref[s * sub:(s + 1) * sub, :] += _tn(spread[s], ys)
        return carry

    lax.fori_loop(0, n_rounds, one_round, 0)

    @pl.when(g == pl.num_programs(1) - 1)
    def _():
        y = x1_ref[...] + g2_ref[0] * acc_ref[...]
        ms = jnp.mean(y * y, axis=-1, keepdims=True)
        o_ref[...] = y * lax.rsqrt(ms + EPS) * fnw_ref[...]


def _moe(xn2, comb_t, counts, w1, w3, w2, sw1, sw3, sw2, x1, g2, fnw, tm):
    t, d = xn2.shape
    ne, _, f = w1.shape
    l = comb_t.shape[2]
    per = EXP_PER_STEP
    gpb = l // tm
    sub, cap = MOE_SUB, MOE_CAP
    tri = jnp.asarray(np.triu(np.ones((sub, sub), np.float32), 1), BF16)
    load = jnp.max(counts.reshape(comb_t.shape[0] * gpb, tm // sub, ne), axis=1)
    order = jnp.argsort(load, axis=-1).astype(jnp.int32)
    tok = lambda w: pl.BlockSpec((tm, w), lambda i, g, o: (i, 0))
    const = lambda shape: pl.BlockSpec(shape, lambda i, g, o: (0,) * len(shape))

    def expert(shape):
        return [pl.BlockSpec((1,) + shape, lambda i, g, o, k=k: (o[i, g * per + k], 0, 0)) for k in range(per)]

    vm = (2 * tm * d * 2 + 2 * ne * tm * 4 + 2 * 3 * per * d * f * 2 + 2 * 3 * d * f * 2
          + 2 * tm * d * 4 * 2 + tm * d * 4 + 16 * per * cap * (tm // sub) * d)
    grid_spec = pltpu.PrefetchScalarGridSpec(
        num_scalar_prefetch=1,
        grid=(t // tm, ne // per),
        in_specs=([tok(d),
                   pl.BlockSpec((1, ne, tm), lambda i, g, o: (i // gpb, 0, i % gpb)),
                   const(tri.shape)]
                  + expert((d, f)) + expert((d, f)) + expert((f, d))
                  + [const(sw1.shape), const(sw3.shape), const(sw2.shape),
                     tok(d),
                     pl.BlockSpec((1, 1, d), lambda i, g, o: (i // gpb, 0, 0)),
                     const((1, d))]),
        out_specs=tok(d),
        scratch_shapes=[pltpu.VMEM((tm, d), F32), pltpu.VMEM((tm // sub, ne, sub), F32)])
    return pl.pallas_call(
        functools.partial(_moe_body, per=per, sub=sub, cap=cap),
        grid_spec=grid_spec,
        out_shape=jax.ShapeDtypeStruct((t, d), F32),
        compiler_params=_params(("arbitrary", "arbitrary"), vm),
        name="moe",
    )(order, xn2, comb_t, tri, *([w1] * per), *([w3] * per), *([w2] * per), sw1, sw3, sw2, x1, g2, fnw)


def _pad_to(a, rows, cols):
    return jnp.pad(a, ((0, rows - a.shape[0]), (0, cols - a.shape[1])))


def kernel(x, c, ctx, c_ctx, ada_w, ada_b, norm1_w, norm2_w, w_in, hy_conv_w, hy_conv_b, hy_w1, hy_b1, hy_freq, hy_w2, hy_b2, hy_w3, hy_bias, gla_a_w2, gla_a_b, gla_norm_w, proj_hy, proj_gla, w_out, router_w, router_bias, exp_w1, exp_w3, exp_w2, sh_w1, sh_w3, sh_w2, final_norm_w):
    b, l, d = x.shape
    assert ada_w.shape[0] == 1, "single-layer block"
    assert l // GRID_W * GRID_W == l and FFT_N2 == GRID_W
    heads = GLA_HEADS
    qk_w = d // 2
    dk = qk_w // heads
    v_w = d
    dv = v_w // heads
    a_w = 2 * GLA_RANK
    hy_w = d
    hy_cols = (HY_ORDER + 1) * hy_w

    rows = -(-(b + 1) // 8) * 8
    cc = jnp.zeros((rows, d), F32).at[:b].set(c).at[b].set(c_ctx)
    mods = _mods(cc, ada_w[0], ada_b[0][None])
    sh1, sc1, g1, sh2, sc2, g2 = [m[:b, None, :] for m in jnp.split(mods, 6, axis=-1)]
    csh1, csc1 = [jnp.broadcast_to(m[b][None, None, :], (b, 1, d)) for m in jnp.split(mods, 6, axis=-1)[:2]]

    w = w_in[0]
    o_a = qk_w + v_w
    o_q = o_a + a_w
    o_g = o_q + qk_w
    o_hy = o_g + v_w
    o_gate = o_hy + hy_cols
    w_k, w_v = w[:, :qk_w], w[:, qk_w:o_a]
    w_a = jnp.pad(w[:, o_a:o_q], ((0, 0), (0, LANES - a_w)))
    wp = jnp.concatenate([w[:, o_gate:], w[:, o_g:o_hy], w_v, w_k, w[:, o_q:o_g], w[:, o_hy:o_gate], w_a],
                         axis=1).astype(BF16)
    p_gate = 0
    p_g = 2 * d
    p_v = p_g + v_w
    p_k = p_v + v_w
    p_q = p_k + qk_w
    p_hy = p_q + qk_w
    p_a = p_hy + hy_cols
    n_all = p_a + LANES
    w_ctx = jnp.concatenate([w_k, w_v, w_a], axis=1).astype(BF16)
    cols_ctx = (0, qk_w, qk_w + v_w, None)
    cols = (p_k, p_v, p_a, p_q)

    nw1 = norm1_w[0][None]
    u_ctx = _inproj(ctx, nw1, csh1, csc1, w_ctx, ctx.shape[1], w_ctx.shape[1])
    u = _inproj(x, nw1, sh1, sc1, wp, 1024, n_all // 5)

    wa = gla_a_w2[0].reshape(2, GLA_RANK, heads, dk).transpose(0, 2, 1, 3)
    waf = jnp.pad(wa[0], ((0, 0), (0, LANES - GLA_RANK), (0, 0))).astype(BF16)
    wab = jnp.pad(wa[1], ((0, 0), (GLA_RANK, LANES - 2 * GLA_RANK), (0, 0))).astype(BF16)
    ba = gla_a_b[0].reshape(2, heads, 1, dk)
    zeros_state = jnp.zeros((b, heads, dv, dk), F32)
    _, _, s_f, s_b = _gla(u_ctx, cols_ctx, waf, wab, ba[0], ba[1], zeros_state, zeros_state, dk, dv, False,
                          ctx.shape[1])
    o_f, o_b, _, _ = _gla(u, cols, waf, wab, ba[0], ba[1], s_f, s_b, dk, dv, True, GLA_BLOCK)

    n = 2 * l
    max_decay = math.log(HY_TARGET) / HY_FAST_DECAY
    min_decay = math.log(HY_TARGET) / HY_SLOW_DECAY
    deltas = jnp.asarray(np.abs(np.linspace(min_decay, max_decay, hy_w, dtype=np.float32))[None])
    ffn = hy_w1.shape[2]
    kern = _hyfilt(l, hy_w,
                   _pad_to(hy_w1[0], LANES, LANES), _pad_to(hy_b1[0][None], 1, LANES),
                   _pad_to(hy_freq[0, 0][None], 1, LANES),
                   _pad_to(hy_w2[0], LANES, LANES), _pad_to(hy_b2[0][None], 1, LANES),
                   _pad_to(hy_freq[0, 1][None], 1, LANES),
                   jnp.pad(hy_w3[0], ((0, LANES - ffn), (0, 0))), deltas)
    tabs = _fft_tables(n, FFT_N2)
    spec = _hyspec(kern, tabs[1], tabs[2], FFT_N2)
    cw, cb = hy_conv_w[0], hy_conv_b[0][None]
    z1 = _hyconv(u, p_hy, u, p_hy + hy_w, cw, cb, 0, hy_w, spec, 0, hy_bias[0, 0][None], tabs, True)
    y_hy = _hyconv(z1, 0, u, p_hy + 2 * hy_w, cw, cb, 0, 2 * hy_w, spec, 1, hy_bias[0, 1][None], tabs, False)

    x1, xn2, logits_t = _mix(x, o_f, o_b, u, p_g, p_gate, y_hy, gla_norm_w[0][None],
                             proj_hy[0].astype(BF16), proj_gla[0].astype(BF16), w_out[0].astype(BF16),
                             g1, norm2_w[0][None], sh2, sc2, router_w[0].T, 512)
    comb, counts = _select(logits_t, router_bias[0].reshape(N_GROUPS, N_EXPERTS // N_GROUPS, 1), 512, MOE_SUB)
    out = _moe(xn2.reshape(b * l, d), comb, counts, exp_w1[0].astype(BF16), exp_w3[0].astype(BF16),
               exp_w2[0].astype(BF16), sh_w1[0].astype(BF16), sh_w3[0].astype(BF16), sh_w2[0].astype(BF16),
               x1.reshape(b * l, d), g2, final_norm_w[None], 1024)
    return out.reshape(b, l, d)
```

```python
import functools
import math

import jax
import jax.numpy as jnp
import numpy as np
from jax import lax
from jax.experimental import pallas as pl
from jax.experimental.pallas import tpu as pltpu

F32 = jnp.float32
BF16 = jnp.bfloat16
HIGHEST = lax.Precision.HIGHEST

GRID_W = 64
EPS = 1e-6
HY_ORDER = 2
HY_BANDS = 16
HY_FAST_DECAY = 0.3
HY_SLOW_DECAY = 1.5
HY_TARGET = 1e-2
GLA_HEADS = 4
GLA_RANK = 16
GLA_TAU = 16.0
N_EXPERTS = 64
N_GROUPS = 8
TOPK_GROUPS = 4
TOP_K = 8
ROUTED_SCALE = 2.5

LANES = 128
V7X_VMEM_BYTES = 64 * 1024 * 1024
VMEM_CAP_BYTES = 56 * 1024 * 1024

GLA_CHUNK = 256
GLA_BLOCK = 512
GLA_VPU_MIN_HALF = 4
FFT_N2 = 64
STRIDE_PAD = 8
STAGE_UNROLL = 32
ROW_UNROLL = 2
EXP_PER_STEP = 4
MOE_SUB = 256
MOE_CAP = 64


def _params(sem, vmem_bytes):
    limit = int(min(VMEM_CAP_BYTES, max(16 * 1024 * 1024, vmem_bytes * 5 // 4 + (2 << 20))))
    return pltpu.CompilerParams(dimension_semantics=sem, vmem_limit_bytes=limit)


def _nt(a, b, **kw):
    return lax.dot_general(a, b, (((1,), (1,)), ((), ())), preferred_element_type=F32, **kw)


def _tn(a, b):
    return lax.dot_general(a, b, (((0,), (0,)), ((), ())), preferred_element_type=F32)


def _dot(a, b, **kw):
    return jnp.dot(a, b, preferred_element_type=F32, **kw)


def _silu(x):
    return x * jax.nn.sigmoid(x)


def _mods_body(c_ref, w_ref, b_ref, o_ref):
    o_ref[...] = _dot(_silu(c_ref[...]), w_ref[...], precision=HIGHEST) + b_ref[...]


def _mods(cc, w, b):
    rows, d = cc.shape
    n = w.shape[1]
    tn = n // 4
    return pl.pallas_call(
        _mods_body,
        grid=(n // tn,),
        in_specs=[pl.BlockSpec((rows, d), lambda j: (0, 0)),
                  pl.BlockSpec((d, tn), lambda j: (0, j)),
                  pl.BlockSpec((1, tn), lambda j: (0, j))],
        out_specs=pl.BlockSpec((rows, tn), lambda j: (0, j)),
        out_shape=jax.ShapeDtypeStruct((rows, n), F32),
        compiler_params=_params(("arbitrary",), 2 * d * tn * 4),
        name="mods",
    )(cc, w, b)


def _norm_mod(x, w, shift, scale):
    ms = jnp.mean(x * x, axis=-1, keepdims=True)
    return (x * lax.rsqrt(ms + EPS) * w) * (1.0 + scale) + shift


def _inproj_body(x_ref, nw_ref, sh_ref, sc_ref, w_ref, o_ref, xn_ref):
    @pl.when(pl.program_id(2) == 0)
    def _():
        xn_ref[...] = _norm_mod(x_ref[0], nw_ref[...], sh_ref[0], sc_ref[0]).astype(BF16)

    o_ref[0] = _dot(xn_ref[...], w_ref[...]).astype(BF16)


def _inproj(x, nw, shift, scale, w, tm, tn):
    b, l, d = x.shape
    n = w.shape[1]
    vm = 2 * tm * d * 4 + 2 * d * tn * 2 + 2 * tm * tn * 2 + tm * d * 2
    return pl.pallas_call(
        _inproj_body,
        grid=(b, l // tm, n // tn),
        in_specs=[pl.BlockSpec((1, tm, d), lambda bi, i, j: (bi, i, 0)),
                  pl.BlockSpec((1, d), lambda bi, i, j: (0, 0)),
                  pl.BlockSpec((1, 1, d), lambda bi, i, j: (bi, 0, 0)),
                  pl.BlockSpec((1, 1, d), lambda bi, i, j: (bi, 0, 0)),
                  pl.BlockSpec((d, tn), lambda bi, i, j: (0, j))],
        out_specs=pl.BlockSpec((1, tm, tn), lambda bi, i, j: (bi, i, j)),
        out_shape=jax.ShapeDtypeStruct((b, l, n), BF16),
        scratch_shapes=[pltpu.VMEM((tm, d), BF16)],
        compiler_params=_params(("arbitrary", "arbitrary", "arbitrary"), vm),
        name="inproj",
    )(x, nw, shift, scale, w)


def _gla_tables(c, inclusive, flip):
    idx = np.arange(c)
    i = idx[:, None]
    x = idx[None, :]
    blocks = [x <= i]
    masks = []
    h = c // 2
    while h >= 1:
        mid = (idx // (2 * h)) * (2 * h) + h
        mi = mid[:, None]
        hi = i if inclusive else i - 1
        if h < GLA_VPU_MIN_HALF:
            blocks.append(((i >= mi) & (x >= mi) & (x <= hi)) | ((i < mi) & (x > i) & (x <= mi - 1)))
        same = (idx[:, None] // (2 * h)) == (idx[None, :] // (2 * h))
        masks.append(same & (idx[:, None] >= mi) & (idx[None, :] < mid[None, :]))
        h //= 2
    masks.append(np.eye(c, dtype=bool))
    if flip:
        blocks = [b[::-1, ::-1] for b in blocks]
        masks = [m[::-1, ::-1] for m in masks]
    lall = np.concatenate(blocks + [np.ones((8, c), bool)], axis=0)
    return lall.astype(np.float32), np.stack(masks).astype(np.float32)


def _gla_chunk(q, k, v, a, wa, ba, lall, masks_ref, st_ref, inclusive, flip, q_scale):
    c, dk = k.shape
    n_levels = int(math.log2(c))
    xg = _dot(a, wa) + ba
    g = (jnp.minimum(xg, 0.0) - jnp.log(1.0 + jnp.exp(-jnp.abs(xg)))) * (1.0 / GLA_TAU)
    g_hi = g.astype(BF16)
    g_lo = (g - g_hi.astype(F32)).astype(BF16)
    e2 = _dot(lall, jnp.concatenate([g_hi, g_lo], axis=1))
    e = e2[:, :dk] + e2[:, dk:]
    run = e[0:c]
    tot = e[e.shape[0] - 8:e.shape[0] - 7]
    upto = run if inclusive else run - g

    def decay(t):
        return jnp.exp(jnp.minimum(t, 0.0))

    row = lax.broadcasted_iota(jnp.int32, (c, dk), 0)
    n_vpu = n_levels - int(math.log2(GLA_VPU_MIN_HALF))

    def level_decay(lv):
        h = c >> (lv + 1)
        if lv >= n_vpu:
            return decay(e[(1 + lv - n_vpu) * c:(2 + lv - n_vpu) * c])
        p0 = h if flip else h - 1
        piv = jnp.concatenate([jnp.broadcast_to(run[m + p0:m + p0 + 1, :], (2 * h, dk))
                               for m in range(0, c, 2 * h)], axis=0)
        if inclusive:
            return jnp.exp(-jnp.abs(run - piv))
        key_side = ((row & (2 * h - 1)) >= h) if flip else ((row & (2 * h - 1)) < h)
        return decay(jnp.where(key_side, piv - run, upto - piv))

    kf = k.astype(F32)
    st = st_ref[...]
    k1 = (kf * decay(tot - run)).astype(BF16)
    st_ref[...] = st * decay(tot) + _tn(v, k1)
    if q is None:
        return None
    qf = q.astype(F32) * q_scale
    o = _nt((qf * decay(upto)).astype(BF16), st.astype(BF16))
    attn = jnp.zeros((c, c), BF16)
    for lv in range(n_levels):
        ex_l = level_decay(lv)
        attn = attn + _nt((qf * ex_l).astype(BF16), (kf * ex_l).astype(BF16)).astype(BF16) * masks_ref[lv]
    if inclusive:
        attn = attn + _nt(qf.astype(BF16), k).astype(BF16) * masks_ref[n_levels]
    return o + _dot(attn, v)


def _gla_body(*refs, with_q, n_sub, chunk, q_scale):
    if with_q:
        (kf_ref, vf_ref, af_ref, qf_ref, kb_ref, vb_ref, ab_ref, qb_ref, waf_ref, wab_ref, baf_ref, bab_ref,
         lf_ref, lb_ref, mf_ref, mb_ref, s0f_ref, s0b_ref, of_ref, ob_ref, sf_ref, sb_ref, stf_ref, stb_ref) = refs
    else:
        (kf_ref, vf_ref, af_ref, kb_ref, vb_ref, ab_ref, waf_ref, wab_ref, baf_ref, bab_ref,
         lf_ref, lb_ref, mf_ref, mb_ref, s0f_ref, s0b_ref, sf_ref, sb_ref, stf_ref, stb_ref) = refs
        qf_ref = qb_ref = of_ref = ob_ref = None

    @pl.when(pl.program_id(2) == 0)
    def _():
        stf_ref[...] = s0f_ref[0, 0]
        stb_ref[...] = s0b_ref[0, 0]

    for s in range(n_sub):
        sl = slice(s * chunk, (s + 1) * chunk)
        o = _gla_chunk(None if qf_ref is None else qf_ref[0, sl, :], kf_ref[0, sl, :], vf_ref[0, sl, :],
                       af_ref[0, sl, :], waf_ref[0], baf_ref[0], lf_ref[...], mf_ref, stf_ref, True, False, q_scale)
        if with_q:
            of_ref[0, sl, :] = o.astype(BF16)
    for s in reversed(range(n_sub)):
        sl = slice(s * chunk, (s + 1) * chunk)
        o = _gla_chunk(None if qb_ref is None else qb_ref[0, sl, :], kb_ref[0, sl, :], vb_ref[0, sl, :],
                       ab_ref[0, sl, :], wab_ref[0], bab_ref[0], lb_ref[...], mb_ref, stb_ref, False, True, q_scale)
        if with_q:
            ob_ref[0, sl, :] = o.astype(BF16)
    sf_ref[0, 0] = stf_ref[...]
    sb_ref[0, 0] = stb_ref[...]


def _gla(u, cols, waf, wab, baf, bab, s0f, s0b, dk, dv, with_q, tb):
    b, l, _ = u.shape
    h = GLA_HEADS
    nb = l // tb
    n_sub = tb // GLA_CHUNK
    lf, mf = _gla_tables(GLA_CHUNK, True, False)
    lb, mb = _gla_tables(GLA_CHUNK, False, True)
    lf, lb = jnp.asarray(lf, BF16), jnp.asarray(lb, BF16)
    mf, mb = jnp.asarray(mf, BF16), jnp.asarray(mb, BF16)
    kc, vc, ac, qc = cols

    def seq_specs(rev):
        def blk(i):
            return (nb - 1 - i) if rev else i
        specs = [pl.BlockSpec((1, tb, dk), lambda bi, hi, i: (bi, blk(i), kc // dk + hi)),
                 pl.BlockSpec((1, tb, dv), lambda bi, hi, i: (bi, blk(i), vc // dv + hi)),
                 pl.BlockSpec((1, tb, LANES), lambda bi, hi, i: (bi, blk(i), ac // LANES))]
        if with_q:
            specs.append(pl.BlockSpec((1, tb, dk), lambda bi, hi, i: (bi, blk(i), qc // dk + hi)))
        return specs

    def const_spec(shape):
        nd = len(shape)
        return pl.BlockSpec(shape, lambda bi, hi, i: (0,) * nd)

    head_w = pl.BlockSpec((1, LANES, dk), lambda bi, hi, i: (hi, 0, 0))
    head_b = pl.BlockSpec((1, 1, dk), lambda bi, hi, i: (hi, 0, 0))
    st_spec = pl.BlockSpec((1, 1, dv, dk), lambda bi, hi, i: (bi, hi, 0, 0))
    in_specs = (seq_specs(False) + seq_specs(True) + [head_w, head_w, head_b, head_b,
                const_spec(lf.shape), const_spec(lb.shape), const_spec(mf.shape), const_spec(mb.shape),
                st_spec, st_spec])
    st_shape = jax.ShapeDtypeStruct((b, h, dv, dk), F32)
    if with_q:
        o_shape = jax.ShapeDtypeStruct((b, l, h * dv), BF16)
        out_shape = (o_shape, o_shape, st_shape, st_shape)
        out_specs = (pl.BlockSpec((1, tb, dv), lambda bi, hi, i: (bi, i, hi)),
                     pl.BlockSpec((1, tb, dv), lambda bi, hi, i: (bi, nb - 1 - i, hi)),
                     st_spec, st_spec)
        args = (u,) * 8
    else:
        out_shape = (st_shape, st_shape)
        out_specs = (st_spec, st_spec)
        args = (u,) * 6
    vm = 4 * tb * (2 * dk + dv + LANES) * 2 * 2 + 8 * dv * dk * 4 + 4 * tb * dv * 2 + (4 << 20)
    outs = pl.pallas_call(
        functools.partial(_gla_body, with_q=with_q, n_sub=n_sub, chunk=GLA_CHUNK, q_scale=dk ** -0.5),
        grid=(b, h, nb),
        in_specs=in_specs,
        out_specs=out_specs,
        out_shape=out_shape,
        scratch_shapes=[pltpu.VMEM((dv, dk), F32), pltpu.VMEM((dv, dk), F32)],
        compiler_params=_params(("arbitrary", "arbitrary", "arbitrary"), vm),
        name="gla" if with_q else "gla_ctx",
    )(*args, waf, wab, baf, bab, lf, lb, mf, mb, s0f, s0b)
    if with_q:
        return outs
    return None, None, outs[0], outs[1]


def _hy_tables(l):
    t = np.linspace(0.0, 1.0, l, dtype=np.float32).astype(np.float64)[:, None]
    w = 2.0 * math.pi * np.arange(l, dtype=np.float64)[:, None] / l
    f = np.linspace(1e-4, HY_BANDS - 1, HY_BANDS, dtype=np.float32).astype(np.float64)[None, :]
    z = np.concatenate([t, np.cos(f * w), -np.sin(f * w)], axis=-1)
    rev = (l - np.arange(l)) % l

    def pad(a):
        out = np.zeros((l, LANES), np.float32)
        out[:, :a.shape[1]] = a
        return out

    tt = np.broadcast_to(t, (l, LANES)).astype(np.float32)
    return pad(z), pad(z[rev]), tt, np.ascontiguousarray(tt[rev])


def _hyfilt_body(z1_ref, z2_ref, t1_ref, t2_ref, w1_ref, b1_ref, f1_ref, w2_ref, b2_ref, f2_ref,
                 w3f_ref, w3b_ref, dl_ref, o_ref, ha_ref, hb_ref):
    l = z1_ref.shape[0]

    @pl.when((pl.program_id(0) == 0) & (pl.program_id(1) == 0))
    def _():
        for z_ref, h_ref in ((z1_ref, ha_ref), (z2_ref, hb_ref)):
            h = jnp.sin(f1_ref[...] * (_dot(z_ref[...], w1_ref[...], precision=HIGHEST) + b1_ref[...]))
            h_ref[...] = jnp.sin(f2_ref[...] * (_dot(h, w2_ref[...], precision=HIGHEST) + b2_ref[...]))

    dl = dl_ref[...]
    hf = _dot(ha_ref[...], w3f_ref[...], precision=HIGHEST) * jnp.exp(-t1_ref[...] * dl)
    hb = _dot(hb_ref[...], w3b_ref[...], precision=HIGHEST) * jnp.exp(-t2_ref[...] * dl)
    row = lax.broadcasted_iota(jnp.int32, hb.shape, 0)
    hb = jnp.where(row == 0, 0.0, hb)
    ss = jnp.sum(hf * hf, axis=0, keepdims=True) + jnp.sum(hb * hb, axis=0, keepdims=True)
    scale = lax.rsqrt(ss)
    o_ref[0, 0:l, :] = hf * scale
    o_ref[0, l:2 * l, :] = hb * scale


def _hyfilt(l, c, w1, b1, f1, w2, b2, f2, w3, deltas):
    z1, z2, t1, t2 = (jnp.asarray(a) for a in _hy_tables(l))
    ncb = c // LANES
    tab = pl.BlockSpec((l, LANES), lambda o, j: (0, 0))
    sq = pl.BlockSpec((LANES, LANES), lambda o, j: (0, 0))
    row = pl.BlockSpec((1, LANES), lambda o, j: (0, 0))
    return pl.pallas_call(
        _hyfilt_body,
        grid=(HY_ORDER, ncb),
        in_specs=[tab, tab, tab, tab, sq, row, row, sq, row, row,
                  pl.BlockSpec((LANES, LANES), lambda o, j: (0, 2 * o * ncb + j)),
                  pl.BlockSpec((LANES, LANES), lambda o, j: (0, (2 * o + 1) * ncb + j)),
                  pl.BlockSpec((1, LANES), lambda o, j: (0, j))],
        out_specs=pl.BlockSpec((1, 2 * l, LANES), lambda o, j: (o, 0, j)),
        out_shape=jax.ShapeDtypeStruct((HY_ORDER, 2 * l, c), F32),
        scratch_shapes=[pltpu.VMEM((l, LANES), F32), pltpu.VMEM((l, LANES), F32)],
        compiler_params=_params(("arbitrary", "arbitrary"), 24 * l * LANES * 4),
        name="hyfilt",
    )(z1, z2, t1, t2, w1, b1, f1, w2, b2, f2, w3, w3, deltas)


def _fft_tables(n, n2):
    n1 = n // n2
    h = n1 // 2
    k1 = np.arange(n1)[:, None]
    a = 2.0 * math.pi * k1 * np.arange(h)[None, :] / n1
    c, s = np.cos(a), np.sin(a)
    f1c = np.block([[c, s], [-s, c]])
    a = 2.0 * math.pi * k1 * np.arange(n1)[None, :] / n1
    f1r = np.concatenate([np.cos(a), -np.sin(a)], axis=0)
    kk = np.arange(n1)[:, None, None] + n1 * np.arange(n2)[None, :, None]
    a = 2.0 * math.pi * kk * np.arange(n2)[None, None, :] / n
    c, s = np.cos(a), np.sin(a)
    gf = np.concatenate([np.concatenate([c, s], axis=2), np.concatenate([-s, c], axis=2)], axis=1)
    a = 2.0 * math.pi * np.arange(h)[:, None] * np.arange(n1)[None, :] / n1
    c, s = np.cos(a), np.sin(a)
    if1 = np.block([[c, -s], [s, c]])
    return tuple(jnp.asarray(m, BF16) for m in (f1c, f1r, gf, if1))


def _rows8(start, size):
    return pl.ds(pl.multiple_of(start, 8), size)


def _hyspec_body(k_ref, f1_ref, gf_ref, o_ref, as_ref, *, n, n2, ap):
    n1 = n // n2

    def stage1(j, carry):
        r = k_ref[0, pl.ds(j, n1, stride=n2), :].astype(BF16)
        as_ref[_rows8(j * ap, 2 * n1), :] = _dot(f1_ref[...], r)
        return carry

    lax.fori_loop(0, n2, stage1, 0, unroll=STAGE_UNROLL)

    def stage2(k1, carry):
        r = jnp.concatenate([as_ref[pl.ds(k1, n2, stride=ap), :],
                             as_ref[pl.ds(n1 + k1, n2, stride=ap), :]], axis=0).astype(BF16)
        o_ref[0, k1] = (_dot(gf_ref[k1], r) * (1.0 / n)).astype(BF16)
        return carry

    lax.fori_loop(0, n1, stage2, 0, unroll=STAGE_UNROLL)


def _hyspec(kern, f1r, gf, n2):
    order, n, c = kern.shape
    n1 = n // n2
    ap = 2 * n1 + STRIDE_PAD
    return pl.pallas_call(
        functools.partial(_hyspec_body, n=n, n2=n2, ap=ap),
        grid=(order, c // LANES),
        in_specs=[pl.BlockSpec((1, n, LANES), lambda o, j: (o, 0, j)),
                  pl.BlockSpec(f1r.shape, lambda o, j: (0, 0)),
                  pl.BlockSpec(gf.shape, lambda o, j: (0, 0, 0))],
        out_specs=pl.BlockSpec((1, n1, 2 * n2, LANES), lambda o, j: (o, 0, 0, j)),
        out_shape=jax.ShapeDtypeStruct((order, n1, 2 * n2, c), BF16),
        scratch_shapes=[pltpu.VMEM((n2 * ap, LANES), F32)],
        compiler_params=_params(("arbitrary", "arbitrary"),
                                2 * n * LANES * 4 + 2 * gf.size * 2 + n2 * ap * LANES * 4 + 2 * n * LANES * 2),
        name="hyspec",
    )(kern, f1r, gf)


def _short_conv(u, w_ref, b_ref):
    r = u.shape[0]
    row = lax.broadcasted_iota(jnp.int32, u.shape, 0)
    up = jnp.where(row == 0, 0.0, pltpu.roll(u, 1, axis=0))
    dn = jnp.where(row == r - 1, 0.0, pltpu.roll(u, r - 1, axis=0))
    return up * w_ref[0:1, :] + u * w_ref[1:2, :] + dn * w_ref[2:3, :] + b_ref[...]


def _hyconv_body(z_ref, g_ref, zw_ref, zb_ref, gw_ref, gb_ref, sp_ref, hb_ref, f1_ref, gf_ref, if1_ref,
                 o_ref, x_ref, as_ref, bs_ref, y_ref, *, conv_z, n2, xp, ap):
    n1h = z_ref.shape[1] // n2
    n1 = 2 * n1h
    half = n1h * xp

    def fill(i, carry):
        for p in range(2):
            u = z_ref[p, pl.ds(pl.multiple_of(i * n2, n2), n2), :].astype(F32)
            x_ref[_rows8(p * half + i * xp, n2), :] = _short_conv(u, zw_ref, zb_ref) if conv_z else u
        return carry

    lax.fori_loop(0, n1h, fill, 0, unroll=ROW_UNROLL)

    def stage1(j, carry):
        r = jnp.concatenate([x_ref[pl.ds(j, n1h, stride=xp), :],
                             x_ref[pl.ds(half + j, n1h, stride=xp), :]], axis=0).astype(BF16)
        as_ref[_rows8(j * ap, 2 * n1), :] = _dot(f1_ref[...], r)
        return carry

    lax.fori_loop(0, n2, stage1, 0, unroll=STAGE_UNROLL)

    def stage2(k1, carry):
        r = jnp.concatenate([as_ref[pl.ds(k1, n2, stride=ap), :],
                             as_ref[pl.ds(n1 + k1, n2, stride=ap), :]], axis=0).astype(BF16)
        gk = gf_ref[k1]
        xk = _dot(gk, r)
        xr, xi = xk[0:n2], xk[n2:2 * n2]
        sp = sp_ref[0, k1].astype(F32)
        sr, si = sp[0:n2], sp[n2:2 * n2]
        yk = jnp.concatenate([xr * sr - xi * si, xr * si + xi * sr], axis=0).astype(BF16)
        bk = _tn(gk, yk)
        bs_ref[pl.ds(k1, n2, stride=ap), :] = bk[0:n2]
        bs_ref[pl.ds(n1 + k1, n2, stride=ap), :] = bk[n2:2 * n2]
        return carry

    lax.fori_loop(0, n1, stage2, 0, unroll=2 * STAGE_UNROLL)

    def stage3(j, carry):
        yn = _dot(if1_ref[...], bs_ref[_rows8(j * ap, 2 * n1), :].astype(BF16))
        y_ref[pl.ds(j, n1h, stride=xp), :] = yn[0:n1h]
        y_ref[pl.ds(half + j, n1h, stride=xp), :] = yn[n1h:n1]
        return carry

    lax.fori_loop(0, n2, stage3, 0, unroll=STAGE_UNROLL)

    def finish(i, carry):
        rows = pl.ds(pl.multiple_of(i * n2, n2), n2)
        for p in range(2):
            gate = _short_conv(g_ref[p, rows, :].astype(F32), gw_ref, gb_ref)
            z = x_ref[_rows8(p * half + i * xp, n2), :]
            y = y_ref[_rows8(p * half + i * xp, n2), :]
            o_ref[p, rows, :] = (gate * (y + z * hb_ref[...])).astype(BF16)
        return carry

    lax.fori_loop(0, n1h, finish, 0, unroll=ROW_UNROLL)


def _hyconv(z, z_col, g, g_col, conv_w, conv_b, zw_col, gw_col, spec, order, hy_bias, tabs, conv_z):
    b, l, _ = z.shape
    f1c, _, gf, if1 = tabs
    n1, n2x2 = gf.shape[0], gf.shape[1]
    n2 = n2x2 // 2
    n1h = n1 // 2
    c = spec.shape[-1]
    ncb = c // LANES
    xp = n2 + STRIDE_PAD
    ap = 2 * n1 + STRIDE_PAD
    vm = (2 * 2 * 2 * l * LANES * 2 + 2 * 2 * l * LANES * 2 + 2 * n1 * n2x2 * LANES * 2
          + 2 * gf.size * 2 + 2 * 2 * n1h * xp * LANES * 4 + 2 * n2 * ap * LANES * 4)
    return pl.pallas_call(
        functools.partial(_hyconv_body, conv_z=conv_z, n2=n2, xp=xp, ap=ap),
        grid=(ncb, b // 2),
        in_specs=[pl.BlockSpec((2, l, LANES), lambda j, p: (p, 0, z_col // LANES + j)),
                  pl.BlockSpec((2, l, LANES), lambda j, p: (p, 0, g_col // LANES + j)),
                  pl.BlockSpec((3, LANES), lambda j, p: (0, zw_col // LANES + j)),
                  pl.BlockSpec((1, LANES), lambda j, p: (0, zw_col // LANES + j)),
                  pl.BlockSpec((3, LANES), lambda j, p: (0, gw_col // LANES + j)),
                  pl.BlockSpec((1, LANES), lambda j, p: (0, gw_col // LANES + j)),
                  pl.BlockSpec((1, n1, n2x2, LANES), lambda j, p: (order, 0, 0, j)),
                  pl.BlockSpec((1, LANES), lambda j, p: (0, j)),
                  pl.BlockSpec(f1c.shape, lambda j, p: (0, 0)),
                  pl.BlockSpec(gf.shape, lambda j, p: (0, 0, 0)),
                  pl.BlockSpec(if1.shape, lambda j, p: (0, 0))],
        out_specs=pl.BlockSpec((2, l, LANES), lambda j, p: (p, 0, j)),
        out_shape=jax.ShapeDtypeStruct((b, l, c), BF16),
        scratch_shapes=[pltpu.VMEM((2 * n1h * xp, LANES), F32),
                        pltpu.VMEM((n2 * ap, LANES), F32),
                        pltpu.VMEM((n2 * ap, LANES), F32),
                        pltpu.VMEM((2 * n1h * xp, LANES), F32)],
        compiler_params=_params(("arbitrary", "arbitrary"), vm),
        name="hyconv%d" % order,
    )(z, g, conv_w, conv_b, conv_w, conv_b, spec, hy_bias, f1c, gf, if1)


def _mix_body(x_ref, of_ref, ob_ref, ug_ref, ugate_ref, yhy_ref, gnw_ref, phy_ref, pgla_ref, wout_ref,
              g1_ref, n2w_ref, sh2_ref, sc2_ref, rwt_ref, x1_ref, xn2_ref, lg_ref, *, heads):
    d = x_ref.shape[2]
    o = of_ref[0].astype(F32) + ob_ref[0].astype(F32)
    dv = o.shape[1] // heads
    parts = []
    for h in range(heads):
        seg = o[:, h * dv:(h + 1) * dv]
        parts.append(seg * lax.rsqrt(jnp.mean(seg * seg, axis=-1, keepdims=True) + EPS))
    y_gla = jnp.concatenate(parts, axis=1) * gnw_ref[...] * _silu(ug_ref[0].astype(F32))
    gates = jax.nn.sigmoid(ugate_ref[0].astype(F32))
    merged = (gates[:, :d] * _dot(yhy_ref[0], phy_ref[...])
              + gates[:, d:] * _dot(y_gla.astype(BF16), pgla_ref[...]))
    x1 = x_ref[0] + g1_ref[0] * _dot(merged.astype(BF16), wout_ref[...])
    x1_ref[0] = x1
    xn2 = _norm_mod(x1, n2w_ref[...], sh2_ref[0], sc2_ref[0])
    xn2_ref[0] = xn2.astype(BF16)
    lg_ref[0] = _nt(rwt_ref[...], xn2, precision=HIGHEST)


def _mix(x, o_f, o_b, u, g_col, gate_col, y_hy, gnw, phy, pgla, wout, g1, n2w, sh2, sc2, rwt, tm):
    b, l, d = x.shape
    vw = o_f.shape[2]
    ne = rwt.shape[0]
    tok = lambda w: pl.BlockSpec((1, tm, w), lambda bi, i: (bi, i, 0))
    per_b = pl.BlockSpec((1, 1, d), lambda bi, i: (bi, 0, 0))
    const = lambda shape: pl.BlockSpec(shape, lambda bi, i: (0, 0))
    vm = (2 * tm * d * 4 * 2 + 2 * tm * (3 * vw + 2 * d + 2 * d) * 2 + 2 * 3 * d * d * 2 + 12 * tm * d * 4)
    return pl.pallas_call(
        functools.partial(_mix_body, heads=GLA_HEADS),
        grid=(b, l // tm),
        in_specs=[tok(d), tok(vw), tok(vw),
                  pl.BlockSpec((1, tm, vw), lambda bi, i: (bi, i, g_col // vw)),
                  pl.BlockSpec((1, tm, 2 * d), lambda bi, i: (bi, i, gate_col // (2 * d))),
                  tok(d), const((1, vw)), const((d, d)), const((vw, d)), const((d, d)),
                  per_b, const((1, d)), per_b, per_b, const((ne, d))],
        out_specs=(tok(d), tok(d), pl.BlockSpec((1, ne, tm), lambda bi, i: (bi, 0, i))),
        out_shape=(jax.ShapeDtypeStruct((b, l, d), F32), jax.ShapeDtypeStruct((b, l, d), BF16),
                   jax.ShapeDtypeStruct((b, ne, l), F32)),
        compiler_params=_params(("arbitrary", "arbitrary"), vm),
        name="mix",
    )(x, o_f, o_b, u, u, y_hy, gnw, phy, pgla, wout, g1, n2w, sh2, sc2, rwt)


def _select_body(lg_ref, bias_ref, o_ref, n_ref):
    ne, tn = lg_ref.shape[1], lg_ref.shape[2]
    ng = N_GROUPS
    pg = ne // ng
    scores = jax.nn.sigmoid(lg_ref[0]).reshape(ng, pg, tn)
    sel = scores + bias_ref[...]
    ie = lax.broadcasted_iota(jnp.int32, sel.shape, 1)
    m1 = jnp.max(sel, axis=1, keepdims=True)
    i1 = jnp.min(jnp.where(sel == m1, ie, pg), axis=1, keepdims=True)
    m2 = jnp.max(jnp.where(ie == i1, -jnp.inf, sel), axis=1, keepdims=True)
    grp = m1 + m2
    ig = lax.broadcasted_iota(jnp.int32, grp.shape, 0)
    rank = jnp.zeros(grp.shape, jnp.int32)
    for g in range(ng):
        other = grp[g:g + 1]
        rank = rank + jnp.where((other > grp) | ((other == grp) & (g < ig)), 1, 0)
    cand = jnp.where(rank < TOPK_GROUPS, sel, -jnp.inf)
    flat = ig * pg + ie
    rank = jnp.zeros(sel.shape, jnp.int32)
    for g in range(ng):
        for e in range(pg):
            other = cand[g:g + 1, e:e + 1, :]
            rank = rank + jnp.where((other > cand) | ((other == cand) & (g * pg + e < flat)), 1, 0)
    w = jnp.where(rank < TOP_K, scores, 0.0)
    tot = jnp.sum(jnp.sum(w, axis=1, keepdims=True), axis=0, keepdims=True)
    comb = (w / tot * ROUTED_SCALE).reshape(ne, tn)
    o_ref[0] = comb
    sub = tn // n_ref.shape[1]
    for s in range(n_ref.shape[1]):
        n_ref[0, s] = jnp.sum(jnp.where(comb[:, s * sub:(s + 1) * sub] > 0.0, 1.0, 0.0), axis=1, keepdims=True)


def _select(logits_t, bias, tn, sub):
    b, ne, l = logits_t.shape
    nb = l // tn
    return pl.pallas_call(
        _select_body,
        grid=(b, nb),
        in_specs=[pl.BlockSpec((1, ne, tn), lambda bi, i: (bi, 0, i)),
                  pl.BlockSpec(bias.shape, lambda bi, i: (0, 0, 0))],
        out_specs=(pl.BlockSpec((1, ne, tn), lambda bi, i: (bi, 0, i)),
                   pl.BlockSpec((1, tn // sub, ne, 1), lambda bi, i: (bi, i, 0, 0))),
        out_shape=(jax.ShapeDtypeStruct((b, ne, l), F32), jax.ShapeDtypeStruct((b, l // sub, ne, 1), F32)),
        compiler_params=_params(("arbitrary", "arbitrary"), 64 * ne * tn * 4),
        name="select",
    )(logits_t, bias)


def _moe_body(order_ref, x_ref, ct_ref, tri_ref, *refs, per, sub, cap):
    w1_refs, w3_refs, w2_refs = refs[:per], refs[per:2 * per], refs[2 * per:3 * per]
    sw1_ref, sw3_ref, sw2_ref, x1_ref, g2_ref, fnw_ref, o_ref, acc_ref, rank_ref = refs[3 * per:]
    g = pl.program_id(1)
    tm = x_ref.shape[0]
    ns = tm // sub

    @pl.when(g == 0)
    def _():
        x = x_ref[...]
        hs = _silu(_dot(x, sw1_ref[...])) * _dot(x, sw3_ref[...])
        acc_ref[...] = _dot(hs.astype(BF16), sw2_ref[...])
        for s in range(ns):
            chosen = ct_ref[0, :, s * sub:(s + 1) * sub] > 0.0
            before = _dot(jnp.where(chosen, 1.0, 0.0).astype(BF16), tri_ref[...])
            rank_ref[s] = jnp.where(chosen, before, -1.0)

    ids = [order_ref[pl.program_id(0), g * per + i] for i in range(per)]
    ranks = [[rank_ref[s, pl.ds(ids[i], 1), :] for i in range(per)] for s in range(ns)]
    wts = [[ct_ref[0, pl.ds(ids[i], 1), s * sub:(s + 1) * sub] for i in range(per)] for s in range(ns)]
    top = ranks[0][0]
    for s in range(ns):
        for i in range(per):
            top = jnp.maximum(top, ranks[s][i])
    n_rounds = (jnp.max(top).astype(jnp.int32) + cap) // cap
    slot = lax.broadcasted_iota(jnp.int32, (cap, sub), 0).astype(F32)

    def one_round(r, carry):
        base = slot + (r * cap).astype(F32)
        packed, spread = [], []
        for s in range(ns):
            hits = [base == ranks[s][i] for i in range(per)]
            pack = jnp.concatenate([jnp.where(h, 1.0, 0.0).astype(BF16) for h in hits], axis=0)
            spread.append(jnp.concatenate([jnp.where(h, wts[s][i], 0.0).astype(BF16)
                                           for i, h in enumerate(hits)], axis=0))
            packed.append(_dot(pack, x_ref[s * sub:(s + 1) * sub, :]).astype(BF16))
        outs = []
        for i in range(per):
            ze = jnp.concatenate([packed[s][i * cap:(i + 1) * cap] for s in range(ns)], axis=0)
            h = _silu(_dot(ze, w1_refs[i][0])) * _dot(ze, w3_refs[i][0])
            outs.append(_dot(h.astype(BF16), w2_refs[i][0]).astype(BF16))
        for s in range(ns):
            ys = jnp.concatenate([outs[i][s * cap:(s + 1) * cap] for i in range(per)], axis=0)
            acc_ref[s * sub:(s + 1) * sub, :] += _tn(spread[s], ys)
        return carry

    lax.fori_loop(0, n_rounds, one_round, 0)

    @pl.when(g == pl.num_programs(1) - 1)
    def _():
        y = x1_ref[...] + g2_ref[0] * acc_ref[...]
        ms = jnp.mean(y * y, axis=-1, keepdims=True)
        o_ref[...] = y * lax.rsqrt(ms + EPS) * fnw_ref[...]


def _moe(xn2, comb_t, counts, w1, w3, w2, sw1, sw3, sw2, x1, g2, fnw, tm):
    t, d = xn2.shape
    ne, _, f = w1.shape
    l = comb_t.shape[2]
    per = EXP_PER_STEP
    gpb = l // tm
    sub, cap = MOE_SUB, MOE_CAP
    tri = jnp.asarray(np.triu(np.ones((sub, sub), np.float32), 1), BF16)
    load = jnp.max(counts.reshape(comb_t.shape[0] * gpb, tm // sub, ne), axis=1)
    order = jnp.argsort(load, axis=-1).astype(jnp.int32)
    tok = lambda w: pl.BlockSpec((tm, w), lambda i, g, o: (i, 0))
    const = lambda shape: pl.BlockSpec(shape, lambda i, g, o: (0,) * len(shape))

    def expert(shape):
        return [pl.BlockSpec((1,) + shape, lambda i, g, o, k=k: (o[i, g * per + k], 0, 0)) for k in range(per)]

    vm = (2 * tm * d * 2 + 2 * ne * tm * 4 + 2 * 3 * per * d * f * 2 + 2 * 3 * d * f * 2
          + 2 * tm * d * 4 * 2 + tm * d * 4 + 16 * per * cap * (tm // sub) * d)
    grid_spec = pltpu.PrefetchScalarGridSpec(
        num_scalar_prefetch=1,
        grid=(t // tm, ne // per),
        in_specs=([tok(d),
                   pl.BlockSpec((1, ne, tm), lambda i, g, o: (i // gpb, 0, i % gpb)),
                   const(tri.shape)]
                  + expert((d, f)) + expert((d, f)) + expert((f, d))
                  + [const(sw1.shape), const(sw3.shape), const(sw2.shape),
                     tok(d),
                     pl.BlockSpec((1, 1, d), lambda i, g, o: (i // gpb, 0, 0)),
                     const((1, d))]),
        out_specs=tok(d),
        scratch_shapes=[pltpu.VMEM((tm, d), F32), pltpu.VMEM((tm // sub, ne, sub), F32)])
    return pl.pallas_call(
        functools.partial(_moe_body, per=per, sub=sub, cap=cap),
        grid_spec=grid_spec,
        out_shape=jax.ShapeDtypeStruct((t, d), F32),
        compiler_params=_params(("arbitrary", "arbitrary"), vm),
        name="moe",
    )(order, xn2, comb_t, tri, *([w1] * per), *([w3] * per), *([w2] * per), sw1, sw3, sw2, x1, g2, fnw)


def _pad_to(a, rows, cols):
    return jnp.pad(a, ((0, rows - a.shape[0]), (0, cols - a.shape[1])))


def kernel(x, c, ctx, c_ctx, ada_w, ada_b, norm1_w, norm2_w, w_in, hy_conv_w, hy_conv_b, hy_w1, hy_b1, hy_freq, hy_w2, hy_b2, hy_w3, hy_bias, gla_a_w2, gla_a_b, gla_norm_w, proj_hy, proj_gla, w_out, router_w, router_bias, exp_w1, exp_w3, exp_w2, sh_w1, sh_w3, sh_w2, final_norm_w):
    b, l, d = x.shape
    assert ada_w.shape[0] == 1, "single-layer block"
    assert l // GRID_W * GRID_W == l and FFT_N2 == GRID_W
    heads = GLA_HEADS
    qk_w = d // 2
    dk = qk_w // heads
    v_w = d
    dv = v_w // heads
    a_w = 2 * GLA_RANK
    hy_w = d
    hy_cols = (HY_ORDER + 1) * hy_w

    rows = -(-(b + 1) // 8) * 8
    cc = jnp.zeros((rows, d), F32).at[:b].set(c).at[b].set(c_ctx)
    mods = _mods(cc, ada_w[0], ada_b[0][None])
    sh1, sc1, g1, sh2, sc2, g2 = [m[:b, None, :] for m in jnp.split(mods, 6, axis=-1)]
    csh1, csc1 = [jnp.broadcast_to(m[b][None, None, :], (b, 1, d)) for m in jnp.split(mods, 6, axis=-1)[:2]]

    w = w_in[0]
    o_a = qk_w + v_w
    o_q = o_a + a_w
    o_g = o_q + qk_w
    o_hy = o_g + v_w
    o_gate = o_hy + hy_cols
    w_k, w_v = w[:, :qk_w], w[:, qk_w:o_a]
    w_a = jnp.pad(w[:, o_a:o_q], ((0, 0), (0, LANES - a_w)))
    wp = jnp.concatenate([w[:, o_gate:], w[:, o_g:o_hy], w_v, w_k, w[:, o_q:o_g], w[:, o_hy:o_gate], w_a],
                         axis=1).astype(BF16)
    p_gate = 0
    p_g = 2 * d
    p_v = p_g + v_w
    p_k = p_v + v_w
    p_q = p_k + qk_w
    p_hy = p_q + qk_w
    p_a = p_hy + hy_cols
    n_all = p_a + LANES
    w_ctx = jnp.concatenate([w_k, w_v, w_a], axis=1).astype(BF16)
    cols_ctx = (0, qk_w, qk_w + v_w, None)
    cols = (p_k, p_v, p_a, p_q)

    nw1 = norm1_w[0][None]
    u_ctx = _inproj(ctx, nw1, csh1, csc1, w_ctx, ctx.shape[1], w_ctx.shape[1])
    u = _inproj(x, nw1, sh1, sc1, wp, 1024, n_all // 5)

    wa = gla_a_w2[0].reshape(2, GLA_RANK, heads, dk).transpose(0, 2, 1, 3)
    waf = jnp.pad(wa[0], ((0, 0), (0, LANES - GLA_RANK), (0, 0))).astype(BF16)
    wab = jnp.pad(wa[1], ((0, 0), (GLA_RANK, LANES - 2 * GLA_RANK), (0, 0))).astype(BF16)
    ba = gla_a_b[0].reshape(2, heads, 1, dk)
    zeros_state = jnp.zeros((b, heads, dv, dk), F32)
    _, _, s_f, s_b = _gla(u_ctx, cols_ctx, waf, wab, ba[0], ba[1], zeros_state, zeros_state, dk, dv, False,
                          ctx.shape[1])
    o_f, o_b, _, _ = _gla(u, cols, waf, wab, ba[0], ba[1], s_f, s_b, dk, dv, True, GLA_BLOCK)

    n = 2 * l
    max_decay = math.log(HY_TARGET) / HY_FAST_DECAY
    min_decay = math.log(HY_TARGET) / HY_SLOW_DECAY
    deltas = jnp.asarray(np.abs(np.linspace(min_decay, max_decay, hy_w, dtype=np.float32))[None])
    ffn = hy_w1.shape[2]
    kern = _hyfilt(l, hy_w,
                   _pad_to(hy_w1[0], LANES, LANES), _pad_to(hy_b1[0][None], 1, LANES),
                   _pad_to(hy_freq[0, 0][None], 1, LANES),
                   _pad_to(hy_w2[0], LANES, LANES), _pad_to(hy_b2[0][None], 1, LANES),
                   _pad_to(hy_freq[0, 1][None], 1, LANES),
                   jnp.pad(hy_w3[0], ((0, LANES - ffn), (0, 0))), deltas)
    tabs = _fft_tables(n, FFT_N2)
    spec = _hyspec(kern, tabs[1], tabs[2], FFT_N2)
    cw, cb = hy_conv_w[0], hy_conv_b[0][None]
    z1 = _hyconv(u, p_hy, u, p_hy + hy_w, cw, cb, 0, hy_w, spec, 0, hy_bias[0, 0][None], tabs, True)
    y_hy = _hyconv(z1, 0, u, p_hy + 2 * hy_w, cw, cb, 0, 2 * hy_w, spec, 1, hy_bias[0, 1][None], tabs, False)

    x1, xn2, logits_t = _mix(x, o_f, o_b, u, p_g, p_gate, y_hy, gla_norm_w[0][None],
                             proj_hy[0].astype(BF16), proj_gla[0].astype(BF16), w_out[0].astype(BF16),
                             g1, norm2_w[0][None], sh2, sc2, router_w[0].T, 512)
    comb, counts = _select(logits_t, router_bias[0].reshape(N_GROUPS, N_EXPERTS // N_GROUPS, 1), 512, MOE_SUB)
    out = _moe(xn2.reshape(b * l, d), comb, counts, exp_w1[0].astype(BF16), exp_w3[0].astype(BF16),
               exp_w2[0].astype(BF16), sh_w1[0].astype(BF16), sh_w3[0].astype(BF16), sh_w2[0].astype(BF16),
               x1.reshape(b * l, d), g2, final_norm_w[None], 1024)
    return out.reshape(b, l, d)
```

```python
import functools
import math

import jax
import jax.numpy as jnp
import numpy as np
from jax import lax
from jax.experimental import pallas as pl
from jax.experimental.pallas import tpu as pltpu

F32 = jnp.float32
BF16 = jnp.bfloat16
HIGHEST = lax.Precision.HIGHEST

GRID_W = 64
EPS = 1e-6
HY_ORDER = 2
HY_BANDS = 16
HY_FAST_DECAY = 0.3
HY_SLOW_DECAY = 1.5
HY_TARGET = 1e-2
GLA_HEADS = 4
GLA_RANK = 16
GLA_TAU = 16.0
N_EXPERTS = 64
N_GROUPS = 8
TOPK_GROUPS = 4
TOP_K = 8
ROUTED_SCALE = 2.5

LANES = 128
V7X_VMEM_BYTES = 64 * 1024 * 1024
VMEM_CAP_BYTES = 56 * 1024 * 1024

GLA_CHUNK = 256
GLA_BLOCK = 512
GLA_VPU_MIN_HALF = 4
FFT_N2 = 64
STRIDE_PAD = 8
STAGE_UNROLL = 32
ROW_UNROLL = 2
EXP_PER_STEP = 4
MOE_SUB = 256
MOE_CAP = 64


def _params(sem, vmem_bytes):
    limit = int(min(VMEM_CAP_BYTES, max(16 * 1024 * 1024, vmem_bytes * 5 // 4 + (2 << 20))))
    return pltpu.CompilerParams(dimension_semantics=sem, vmem_limit_bytes=limit)


def _nt(a, b, **kw):
    return lax.dot_general(a, b, (((1,), (1,)), ((), ())), preferred_element_type=F32, **kw)


def _tn(a, b):
    return lax.dot_general(a, b, (((0,), (0,)), ((), ())), preferred_element_type=F32)


def _dot(a, b, **kw):
    return jnp.dot(a, b, preferred_element_type=F32, **kw)


def _silu(x):
    return x * jax.nn.sigmoid(x)


def _mods_body(c_ref, w_ref, b_ref, o_ref):
    o_ref[...] = _dot(_silu(c_ref[...]), w_ref[...], precision=HIGHEST) + b_ref[...]


def _mods(cc, w, b):
    rows, d = cc.shape
    n = w.shape[1]
    tn = n // 4
    return pl.pallas_call(
        _mods_body,
        grid=(n // tn,),
        in_specs=[pl.BlockSpec((rows, d), lambda j: (0, 0)),
                  pl.BlockSpec((d, tn), lambda j: (0, j)),
                  pl.BlockSpec((1, tn), lambda j: (0, j))],
        out_specs=pl.BlockSpec((rows, tn), lambda j: (0, j)),
        out_shape=jax.ShapeDtypeStruct((rows, n), F32),
        compiler_params=_params(("arbitrary",), 2 * d * tn * 4),
        name="mods",
    )(cc, w, b)


def _norm_mod(x, w, shift, scale):
    ms = jnp.mean(x * x, axis=-1, keepdims=True)
    return (x * lax.rsqrt(ms + EPS) * w) * (1.0 + scale) + shift


def _inproj_body(x_ref, nw_ref, sh_ref, sc_ref, w_ref, o_ref, xn_ref):
    @pl.when(pl.program_id(2) == 0)
    def _():
        xn_ref[...] = _norm_mod(x_ref[0], nw_ref[...], sh_ref[0], sc_ref[0]).astype(BF16)

    o_ref[0] = _dot(xn_ref[...], w_ref[...]).astype(BF16)


def _inproj(x, nw, shift, scale, w, tm, tn):
    b, l, d = x.shape
    n = w.shape[1]
    vm = 2 * tm * d * 4 + 2 * d * tn * 2 + 2 * tm * tn * 2 + tm * d * 2
    return pl.pallas_call(
        _inproj_body,
        grid=(b, l // tm, n // tn),
        in_specs=[pl.BlockSpec((1, tm, d), lambda bi, i, j: (bi, i, 0)),
                  pl.BlockSpec((1, d), lambda bi, i, j: (0, 0)),
                  pl.BlockSpec((1, 1, d), lambda bi, i, j: (bi, 0, 0)),
                  pl.BlockSpec((1, 1, d), lambda bi, i, j: (bi, 0, 0)),
                  pl.BlockSpec((d, tn), lambda bi, i, j: (0, j))],
        out_specs=pl.BlockSpec((1, tm, tn), lambda bi, i, j: (bi, i, j)),
        out_shape=jax.ShapeDtypeStruct((b, l, n), BF16),
        scratch_shapes=[pltpu.VMEM((tm, d), BF16)],
        compiler_params=_params(("arbitrary", "arbitrary", "arbitrary"), vm),
        name="inproj",
    )(x, nw, shift, scale, w)


def _gla_tables(c, inclusive, flip):
    idx = np.arange(c)
    i = idx[:, None]
    x = idx[None, :]
    blocks = [x <= i]
    masks = []
    h = c // 2
    while h >= 1:
        mid = (idx // (2 * h)) * (2 * h) + h
        mi = mid[:, None]
        hi = i if inclusive else i - 1
        if h < GLA_VPU_MIN_HALF:
            blocks.append(((i >= mi) & (x >= mi) & (x <= hi)) | ((i < mi) & (x > i) & (x <= mi - 1)))
        same = (idx[:, None] // (2 * h)) == (idx[None, :] // (2 * h))
        masks.append(same & (idx[:, None] >= mi) & (idx[None, :] < mid[None, :]))
        h //= 2
    masks.append(np.eye(c, dtype=bool))
    if flip:
        blocks = [b[::-1, ::-1] for b in blocks]
        masks = [m[::-1, ::-1] for m in masks]
    lall = np.concatenate(blocks + [np.ones((8, c), bool)], axis=0)
    return lall.astype(np.float32), np.stack(masks).astype(np.float32)


def _gla_chunk(q, k, v, a, wa, ba, lall, masks_ref, st_ref, inclusive, flip, q_scale):
    c, dk = k.shape
    n_levels = int(math.log2(c))
    xg = _dot(a, wa) + ba
    g = (jnp.minimum(xg, 0.0) - jnp.log(1.0 + jnp.exp(-jnp.abs(xg)))) * (1.0 / GLA_TAU)
    g_hi = g.astype(BF16)
    g_lo = (g - g_hi.astype(F32)).astype(BF16)
    e2 = _dot(lall, jnp.concatenate([g_hi, g_lo], axis=1))
    e = e2[:, :dk] + e2[:, dk:]
    run = e[0:c]
    tot = e[e.shape[0] - 8:e.shape[0] - 7]
    upto = run if inclusive else run - g

    def decay(t):
        return jnp.exp(jnp.minimum(t, 0.0))

    row = lax.broadcasted_iota(jnp.int32, (c, dk), 0)
    n_vpu = n_levels - int(math.log2(GLA_VPU_MIN_HALF))

    def level_decay(lv):
        h = c >> (lv + 1)
        if lv >= n_vpu:
            return decay(e[(1 + lv - n_vpu) * c:(2 + lv - n_vpu) * c])
        p0 = h if flip else h - 1
        piv = jnp.concatenate([jnp.broadcast_to(run[m + p0:m + p0 + 1, :], (2 * h, dk))
                               for m in range(0, c, 2 * h)], axis=0)
        if inclusive:
            return jnp.exp(-jnp.abs(run - piv))
        key_side = ((row & (2 * h - 1)) >= h) if flip else ((row & (2 * h - 1)) < h)
        return decay(jnp.where(key_side, piv - run, upto - piv))

    kf = k.astype(F32)
    st = st_ref[...]
    k1 = (kf * decay(tot - run)).astype(BF16)
    st_ref[...] = st * decay(tot) + _tn(v, k1)
    if q is None:
        return None
    qf = q.astype(F32) * q_scale
    o = _nt((qf * decay(upto)).astype(BF16), st.astype(BF16))
    attn = jnp.zeros((c, c), BF16)
    for lv in range(n_levels):
        ex_l = level_decay(lv)
        attn = attn + _nt((qf * ex_l).astype(BF16), (kf * ex_l).astype(BF16)).astype(BF16) * masks_ref[lv]
    if inclusive:
        attn = attn + _nt(qf.astype(BF16), k).astype(BF16) * masks_ref[n_levels]
    return o + _dot(attn, v)


def _gla_body(*refs, with_q, n_sub, chunk, q_scale):
    if with_q:
        (kf_ref, vf_ref, af_ref, qf_ref, kb_ref, vb_ref, ab_ref, qb_ref, waf_ref, wab_ref, baf_ref, bab_ref,
         lf_ref, lb_ref, mf_ref, mb_ref, s0f_ref, s0b_ref, of_ref, ob_ref, sf_ref, sb_ref, stf_ref, stb_ref) = refs
    else:
        (kf_ref, vf_ref, af_ref, kb_ref, vb_ref, ab_ref, waf_ref, wab_ref, baf_ref, bab_ref,
         lf_ref, lb_ref, mf_ref, mb_ref, s0f_ref, s0b_ref, sf_ref, sb_ref, stf_ref, stb_ref) = refs
        qf_ref = qb_ref = of_ref = ob_ref = None

    @pl.when(pl.program_id(2) == 0)
    def _():
        stf_ref[...] = s0f_ref[0, 0]
        stb_ref[...] = s0b_ref[0, 0]

    for s in range(n_sub):
        sl = slice(s * chunk, (s + 1) * chunk)
        o = _gla_chunk(None if qf_ref is None else qf_ref[0, sl, :], kf_ref[0, sl, :], vf_ref[0, sl, :],
                       af_ref[0, sl, :], waf_ref[0], baf_ref[0], lf_ref[...], mf_ref, stf_ref, True, False, q_scale)
        if with_q:
            of_ref[0, sl, :] = o.astype(BF16)
    for s in reversed(range(n_sub)):
        sl = slice(s * chunk, (s + 1) * chunk)
        o = _gla_chunk(None if qb_ref is None else qb_ref[0, sl, :], kb_ref[0, sl, :], vb_ref[0, sl, :],
                       ab_ref[0, sl, :], wab_ref[0], bab_ref[0], lb_ref[...], mb_ref, stb_ref, False, True, q_scale)
        if with_q:
            ob_ref[0, sl, :] = o.astype(BF16)
    sf_ref[0, 0] = stf_ref[...]
    sb_ref[0, 0] = stb_ref[...]


def _gla(u, cols, waf, wab, baf, bab, s0f, s0b, dk, dv, with_q, tb):
    b, l, _ = u.shape
    h = GLA_HEADS
    nb = l // tb
    n_sub = tb // GLA_CHUNK
    lf, mf = _gla_tables(GLA_CHUNK, True, False)
    lb, mb = _gla_tables(GLA_CHUNK, False, True)
    lf, lb = jnp.asarray(lf, BF16), jnp.asarray(lb, BF16)
    mf, mb = jnp.asarray(mf, BF16), jnp.asarray(mb, BF16)
    kc, vc, ac, qc = cols

    def seq_specs(rev):
        def blk(i):
            return (nb - 1 - i) if rev else i
        specs = [pl.BlockSpec((1, tb, dk), lambda bi, hi, i: (bi, blk(i), kc // dk + hi)),
                 pl.BlockSpec((1, tb, dv), lambda bi, hi, i: (bi, blk(i), vc // dv + hi)),
                 pl.BlockSpec((1, tb, LANES), lambda bi, hi, i: (bi, blk(i), ac // LANES))]
        if with_q:
            specs.append(pl.BlockSpec((1, tb, dk), lambda bi, hi, i: (bi, blk(i), qc // dk + hi)))
        return specs

    def const_spec(shape):
        nd = len(shape)
        return pl.BlockSpec(shape, lambda bi, hi, i: (0,) * nd)

    head_w = pl.BlockSpec((1, LANES, dk), lambda bi, hi, i: (hi, 0, 0))
    head_b = pl.BlockSpec((1, 1, dk), lambda bi, hi, i: (hi, 0, 0))
    st_spec = pl.BlockSpec((1, 1, dv, dk), lambda bi, hi, i: (bi, hi, 0, 0))
    in_specs = (seq_specs(False) + seq_specs(True) + [head_w, head_w, head_b, head_b,
                const_spec(lf.shape), const_spec(lb.shape), const_spec(mf.shape), const_spec(mb.shape),
                st_spec, st_spec])
    st_shape = jax.ShapeDtypeStruct((b, h, dv, dk), F32)
    if with_q:
        o_shape = jax.ShapeDtypeStruct((b, l, h * dv), BF16)
        out_shape = (o_shape, o_shape, st_shape, st_shape)
        out_specs = (pl.BlockSpec((1, tb, dv), lambda bi, hi, i: (bi, i, hi)),
                     pl.BlockSpec((1, tb, dv), lambda bi, hi, i: (bi, nb - 1 - i, hi)),
                     st_spec, st_spec)
        args = (u,) * 8
    else:
        out_shape = (st_shape, st_shape)
        out_specs = (st_spec, st_spec)
        args = (u,) * 6
    vm = 4 * tb * (2 * dk + dv + LANES) * 2 * 2 + 8 * dv * dk * 4 + 4 * tb * dv * 2 + (4 << 20)
    outs = pl.pallas_call(
        functools.partial(_gla_body, with_q=with_q, n_sub=n_sub, chunk=GLA_CHUNK, q_scale=dk ** -0.5),
        grid=(b, h, nb),
        in_specs=in_specs,
        out_specs=out_specs,
        out_shape=out_shape,
        scratch_shapes=[pltpu.VMEM((dv, dk), F32), pltpu.VMEM((dv, dk), F32)],
        compiler_params=_params(("arbitrary", "arbitrary", "arbitrary"), vm),
        name="gla" if with_q else "gla_ctx",
    )(*args, waf, wab, baf, bab, lf, lb, mf, mb, s0f, s0b)
    if with_q:
        return outs
    return None, None, outs[0], outs[1]


def _hy_tables(l):
    t = np.linspace(0.0, 1.0, l, dtype=np.float32).astype(np.float64)[:, None]
    w = 2.0 * math.pi * np.arange(l, dtype=np.float64)[:, None] / l
    f = np.linspace(1e-4, HY_BANDS - 1, HY_BANDS, dtype=np.float32).astype(np.float64)[None, :]
    z = np.concatenate([t, np.cos(f * w), -np.sin(f * w)], axis=-1)
    rev = (l - np.arange(l)) % l

    def pad(a):
        out = np.zeros((l, LANES), np.float32)
        out[:, :a.shape[1]] = a
        return out

    tt = np.broadcast_to(t, (l, LANES)).astype(np.float32)
    return pad(z), pad(z[rev]), tt, np.ascontiguousarray(tt[rev])


def _hyfilt_body(z1_ref, z2_ref, t1_ref, t2_ref, w1_ref, b1_ref, f1_ref, w2_ref, b2_ref, f2_ref,
                 w3f_ref, w3b_ref, dl_ref, o_ref, ha_ref, hb_ref):
    l = z1_ref.shape[0]

    @pl.when((pl.program_id(0) == 0) & (pl.program_id(1) == 0))
    def _():
        for z_ref, h_ref in ((z1_ref, ha_ref), (z2_ref, hb_ref)):
            h = jnp.sin(f1_ref[...] * (_dot(z_ref[...], w1_ref[...], precision=HIGHEST) + b1_ref[...]))
            h_ref[...] = jnp.sin(f2_ref[...] * (_dot(h, w2_ref[...], precision=HIGHEST) + b2_ref[...]))

    dl = dl_ref[...]
    hf = _dot(ha_ref[...], w3f_ref[...], precision=HIGHEST) * jnp.exp(-t1_ref[...] * dl)
    hb = _dot(hb_ref[...], w3b_ref[...], precision=HIGHEST) * jnp.exp(-t2_ref[...] * dl)
    row = lax.broadcasted_iota(jnp.int32, hb.shape, 0)
    hb = jnp.where(row == 0, 0.0, hb)
    ss = jnp.sum(hf * hf, axis=0, keepdims=True) + jnp.sum(hb * hb, axis=0, keepdims=True)
    scale = lax.rsqrt(ss)
    o_ref[0, 0:l, :] = hf * scale
    o_ref[0, l:2 * l, :] = hb * scale


def _hyfilt(l, c, w1, b1, f1, w2, b2, f2, w3, deltas):
    z1, z2, t1, t2 = (jnp.asarray(a) for a in _hy_tables(l))
    ncb = c // LANES
    tab = pl.BlockSpec((l, LANES), lambda o, j: (0, 0))
    sq = pl.BlockSpec((LANES, LANES), lambda o, j: (0, 0))
    row = pl.BlockSpec((1, LANES), lambda o, j: (0, 0))
    return pl.pallas_call(
        _hyfilt_body,
        grid=(HY_ORDER, ncb),
        in_specs=[tab, tab, tab, tab, sq, row, row, sq, row, row,
                  pl.BlockSpec((LANES, LANES), lambda o, j: (0, 2 * o * ncb + j)),
                  pl.BlockSpec((LANES, LANES), lambda o, j: (0, (2 * o + 1) * ncb + j)),
                  pl.BlockSpec((1, LANES), lambda o, j: (0, j))],
        out_specs=pl.BlockSpec((1, 2 * l, LANES), lambda o, j: (o, 0, j)),
        out_shape=jax.ShapeDtypeStruct((HY_ORDER, 2 * l, c), F32),
        scratch_shapes=[pltpu.VMEM((l, LANES), F32), pltpu.VMEM((l, LANES), F32)],
        compiler_params=_params(("arbitrary", "arbitrary"), 24 * l * LANES * 4),
        name="hyfilt",
    )(z1, z2, t1, t2, w1, b1, f1, w2, b2, f2, w3, w3, deltas)


def _fft_tables(n, n2):
    n1 = n // n2
    h = n1 // 2
    k1 = np.arange(n1)[:, None]
    a = 2.0 * math.pi * k1 * np.arange(h)[None, :] / n1
    c, s = np.cos(a), np.sin(a)
    f1c = np.block([[c, s], [-s, c]])
    a = 2.0 * math.pi * k1 * np.arange(n1)[None, :] / n1
    f1r = np.concatenate([np.cos(a), -np.sin(a)], axis=0)
    kk = np.arange(n1)[:, None, None] + n1 * np.arange(n2)[None, :, None]
    a = 2.0 * math.pi * kk * np.arange(n2)[None, None, :] / n
    c, s = np.cos(a), np.sin(a)
    gf = np.concatenate([np.concatenate([c, s], axis=2), np.concatenate([-s, c], axis=2)], axis=1)
    a = 2.0 * math.pi * np.arange(h)[:, None] * np.arange(n1)[None, :] / n1
    c, s = np.cos(a), np.sin(a)
    if1 = np.block([[c, -s], [s, c]])
    return tuple(jnp.asarray(m, BF16) for m in (f1c, f1r, gf, if1))


def _rows8(start, size):
    return pl.ds(pl.multiple_of(start, 8), size)


def _hyspec_body(k_ref, f1_ref, gf_ref, o_ref, as_ref, *, n, n2, ap):
    n1 = n // n2

    def stage1(j, carry):
        r = k_ref[0, pl.ds(j, n1, stride=n2), :].astype(BF16)
        as_ref[_rows8(j * ap, 2 * n1), :] = _dot(f1_ref[...], r)
        return carry

    lax.fori_loop(0, n2, stage1, 0, unroll=STAGE_UNROLL)

    def stage2(k1, carry):
        r = jnp.concatenate([as_ref[pl.ds(k1, n2, stride=ap), :],
                             as_ref[pl.ds(n1 + k1, n2, stride=ap), :]], axis=0).astype(BF16)
        o_ref[0, k1] = (_dot(gf_ref[k1], r) * (1.0 / n)).astype(BF16)
        return carry

    lax.fori_loop(0, n1, stage2, 0, unroll=STAGE_UNROLL)


def _hyspec(kern, f1r, gf, n2):
    order, n, c = kern.shape
    n1 = n // n2
    ap = 2 * n1 + STRIDE_PAD
    return pl.pallas_call(
        functools.partial(_hyspec_body, n=n, n2=n2, ap=ap),
        grid=(order, c // LANES),
        in_specs=[pl.BlockSpec((1, n, LANES), lambda o, j: (o, 0, j)),
                  pl.BlockSpec(f1r.shape, lambda o, j: (0, 0)),
                  pl.BlockSpec(gf.shape, lambda o, j: (0, 0, 0))],
        out_specs=pl.BlockSpec((1, n1, 2 * n2, LANES), lambda o, j: (o, 0, 0, j)),
        out_shape=jax.ShapeDtypeStruct((order, n1, 2 * n2, c), BF16),
        scratch_shapes=[pltpu.VMEM((n2 * ap, LANES), F32)],
        compiler_params=_params(("arbitrary", "arbitrary"),
                                2 * n * LANES * 4 + 2 * gf.size * 2 + n2 * ap * LANES * 4 + 2 * n * LANES * 2),
        name="hyspec",
    )(kern, f1r, gf)


def _short_conv(u, w_ref, b_ref):
    r = u.shape[0]
    row = lax.broadcasted_iota(jnp.int32, u.shape, 0)
    up = jnp.where(row == 0, 0.0, pltpu.roll(u, 1, axis=0))
    dn = jnp.where(row == r - 1, 0.0, pltpu.roll(u, r - 1, axis=0))
    return up * w_ref[0:1, :] + u * w_ref[1:2, :] + dn * w_ref[2:3, :] + b_ref[...]


def _hyconv_body(z_ref, g_ref, zw_ref, zb_ref, gw_ref, gb_ref, sp_ref, hb_ref, f1_ref, gf_ref, if1_ref,
                 o_ref, x_ref, as_ref, bs_ref, y_ref, *, conv_z, n2, xp, ap):
    n1h = z_ref.shape[1] // n2
    n1 = 2 * n1h
    half = n1h * xp

    def fill(i, carry):
        for p in range(2):
            u = z_ref[p, pl.ds(pl.multiple_of(i * n2, n2), n2), :].astype(F32)
            x_ref[_rows8(p * half + i * xp, n2), :] = _short_conv(u, zw_ref, zb_ref) if conv_z else u
        return carry

    lax.fori_loop(0, n1h, fill, 0, unroll=ROW_UNROLL)

    def stage1(j, carry):
        r = jnp.concatenate([x_ref[pl.ds(j, n1h, stride=xp), :],
                             x_ref[pl.ds(half + j, n1h, stride=xp), :]], axis=0).astype(BF16)
        as_ref[_rows8(j * ap, 2 * n1), :] = _dot(f1_ref[...], r)
        return carry

    lax.fori_loop(0, n2, stage1, 0, unroll=STAGE_UNROLL)

    def stage2(k1, carry):
        r = jnp.concatenate([as_ref[pl.ds(k1, n2, stride=ap), :],
                             as_ref[pl.ds(n1 + k1, n2, stride=ap), :]], axis=0).astype(BF16)
        gk = gf_ref[k1]
        xk = _dot(gk, r)
        xr, xi = xk[0:n2], xk[n2:2 * n2]
        sp = sp_ref[0, k1].astype(F32)
        sr, si = sp[0:n2], sp[n2:2 * n2]
        yk = jnp.concatenate([xr * sr - xi * si, xr * si + xi * sr], axis=0).astype(BF16)
        bk = _tn(gk, yk)
        bs_ref[pl.ds(k1, n2, stride=ap), :] = bk[0:n2]
        bs_ref[pl.ds(n1 + k1, n2, stride=ap), :] = bk[n2:2 * n2]
        return carry

    lax.fori_loop(0, n1, stage2, 0, unroll=2 * STAGE_UNROLL)

    def stage3(j, carry):
        yn = _dot(if1_ref[...], bs_ref[_rows8(j * ap, 2 * n1), :].astype(BF16))
        y_ref[pl.ds(j, n1h, stride=xp), :] = yn[0:n1h]
        y_ref[pl.ds(half + j, n1h, stride=xp), :] = yn[n1h:n1]
        return carry

    lax.fori_loop(0, n2, stage3, 0, unroll=STAGE_UNROLL)

    def finish(i, carry):
        rows = pl.ds(pl.multiple_of(i * n2, n2), n2)
        for p in range(2):
            gate = _short_conv(g_ref[p, rows, :].astype(F32), gw_ref, gb_ref)
            z = x_ref[_rows8(p * half + i * xp, n2), :]
            y = y_ref[_rows8(p * half + i * xp, n2), :]
            o_ref[p, rows, :] = (gate * (y + z * hb_ref[...])).astype(BF16)
        return carry

    lax.fori_loop(0, n1h, finish, 0, unroll=ROW_UNROLL)


def _hyconv(z, z_col, g, g_col, conv_w, conv_b, zw_col, gw_col, spec, order, hy_bias, tabs, conv_z):
    b, l, _ = z.shape
    f1c, _, gf, if1 = tabs
    n1, n2x2 = gf.shape[0], gf.shape[1]
    n2 = n2x2 // 2
    n1h = n1 // 2
    c = spec.shape[-1]
    ncb = c // LANES
    xp = n2 + STRIDE_PAD
    ap = 2 * n1 + STRIDE_PAD
    vm = (2 * 2 * 2 * l * LANES * 2 + 2 * 2 * l * LANES * 2 + 2 * n1 * n2x2 * LANES * 2
          + 2 * gf.size * 2 + 2 * 2 * n1h * xp * LANES * 4 + 2 * n2 * ap * LANES * 4)
    return pl.pallas_call(
        functools.partial(_hyconv_body, conv_z=conv_z, n2=n2, xp=xp, ap=ap),
        grid=(ncb, b // 2),
        in_specs=[pl.BlockSpec((2, l, LANES), lambda j, p: (p, 0, z_col // LANES + j)),
                  pl.BlockSpec((2, l, LANES), lambda j, p: (p, 0, g_col // LANES + j)),
                  pl.BlockSpec((3, LANES), lambda j, p: (0, zw_col // LANES + j)),
                  pl.BlockSpec((1, LANES), lambda j, p: (0, zw_col // LANES + j)),
                  pl.BlockSpec((3, LANES), lambda j, p: (0, gw_col // LANES + j)),
                  pl.BlockSpec((1, LANES), lambda j, p: (0, gw_col // LANES + j)),
                  pl.BlockSpec((1, n1, n2x2, LANES), lambda j, p: (order, 0, 0, j)),
                  pl.BlockSpec((1, LANES), lambda j, p: (0, j)),
                  pl.BlockSpec(f1c.shape, lambda j, p: (0, 0)),
                  pl.BlockSpec(gf.shape, lambda j, p: (0, 0, 0)),
                  pl.BlockSpec(if1.shape, lambda j, p: (0, 0))],
        out_specs=pl.BlockSpec((2, l, LANES), lambda j, p: (p, 0, j)),
        out_shape=jax.ShapeDtypeStruct((b, l, c), BF16),
        scratch_shapes=[pltpu.VMEM((2 * n1h * xp, LANES), F32),
                        pltpu.VMEM((n2 * ap, LANES), F32),
                        pltpu.VMEM((n2 * ap, LANES), F32),
                        pltpu.VMEM((2 * n1h * xp, LANES), F32)],
        compiler_params=_params(("arbitrary", "arbitrary"), vm),
        name="hyconv%d" % order,
    )(z, g, conv_w, conv_b, conv_w, conv_b, spec, hy_bias, f1c, gf, if1)


def _mix_body(x_ref, of_ref, ob_ref, ug_ref, ugate_ref, yhy_ref, gnw_ref, phy_ref, pgla_ref, wout_ref,
              g1_ref, n2w_ref, sh2_ref, sc2_ref, rwt_ref, x1_ref, xn2_ref, lg_ref, *, heads):
    d = x_ref.shape[2]
    o = of_ref[0].astype(F32) + ob_ref[0].astype(F32)
    dv = o.shape[1] // heads
    parts = []
    for h in range(heads):
        seg = o[:, h * dv:(h + 1) * dv]
        parts.append(seg * lax.rsqrt(jnp.mean(seg * seg, axis=-1, keepdims=True) + EPS))
    y_gla = jnp.concatenate(parts, axis=1) * gnw_ref[...] * _silu(ug_ref[0].astype(F32))
    gates = jax.nn.sigmoid(ugate_ref[0].astype(F32))
    merged = (gates[:, :d] * _dot(yhy_ref[0], phy_ref[...])
              + gates[:, d:] * _dot(y_gla.astype(BF16), pgla_ref[...]))
    x1 = x_ref[0] + g1_ref[0] * _dot(merged.astype(BF16), wout_ref[...])
    x1_ref[0] = x1
    xn2 = _norm_mod(x1, n2w_ref[...], sh2_ref[0], sc2_ref[0])
    xn2_ref[0] = xn2.astype(BF16)
    lg_ref[0] = _nt(rwt_ref[...], xn2, precision=HIGHEST)


def _mix(x, o_f, o_b, u, g_col, gate_col, y_hy, gnw, phy, pgla, wout, g1, n2w, sh2, sc2, rwt, tm):
    b, l, d = x.shape
    vw = o_f.shape[2]
    ne = rwt.shape[0]
    tok = lambda w: pl.BlockSpec((1, tm, w), lambda bi, i: (bi, i, 0))
    per_b = pl.BlockSpec((1, 1, d), lambda bi, i: (bi, 0, 0))
    const = lambda shape: pl.BlockSpec(shape, lambda bi, i: (0, 0))
    vm = (2 * tm * d * 4 * 2 + 2 * tm * (3 * vw + 2 * d + 2 * d) * 2 + 2 * 3 * d * d * 2 + 12 * tm * d * 4)
    return pl.pallas_call(
        functools.partial(_mix_body, heads=GLA_HEADS),
        grid=(b, l // tm),
        in_specs=[tok(d), tok(vw), tok(vw),
                  pl.BlockSpec((1, tm, vw), lambda bi, i: (bi, i, g_col // vw)),
                  pl.BlockSpec((1, tm, 2 * d), lambda bi, i: (bi, i, gate_col // (2 * d))),
                  tok(d), const((1, vw)), const((d, d)), const((vw, d)), const((d, d)),
                  per_b, const((1, d)), per_b, per_b, const((ne, d))],
        out_specs=(tok(d), tok(d), pl.BlockSpec((1, ne, tm), lambda bi, i: (bi, 0, i))),
        out_shape=(jax.ShapeDtypeStruct((b, l, d), F32), jax.ShapeDtypeStruct((b, l, d), BF16),
                   jax.ShapeDtypeStruct((b, ne, l), F32)),
        compiler_params=_params(("arbitrary", "arbitrary"), vm),
        name="mix",
    )(x, o_f, o_b, u, u, y_hy, gnw, phy, pgla, wout, g1, n2w, sh2, sc2, rwt)


def _select_body(lg_ref, bias_ref, o_ref, n_ref):
    ne, tn = lg_ref.shape[1], lg_ref.shape[2]
    ng = N_GROUPS
    pg = ne // ng
    scores = jax.nn.sigmoid(lg_ref[0]).reshape(ng, pg, tn)
    sel = scores + bias_ref[...]
    ie = lax.broadcasted_iota(jnp.int32, sel.shape, 1)
    m1 = jnp.max(sel, axis=1, keepdims=True)
    i1 = jnp.min(jnp.where(sel == m1, ie, pg), axis=1, keepdims=True)
    m2 = jnp.max(jnp.where(ie == i1, -jnp.inf, sel), axis=1, keepdims=True)
    grp = m1 + m2
    ig = lax.broadcasted_iota(jnp.int32, grp.shape, 0)
    rank = jnp.zeros(grp.shape, jnp.int32)
    for g in range(ng):
        other = grp[g:g + 1]
        rank = rank + jnp.where((other > grp) | ((other == grp) & (g < ig)), 1, 0)
    cand = jnp.where(rank < TOPK_GROUPS, sel, -jnp.inf)
    flat = ig * pg + ie
    w = jnp.zeros(sel.shape, F32)
    for _ in range(TOP_K):
        best = jnp.max(jnp.max(cand, axis=1, keepdims=True), axis=0, keepdims=True)
        first = jnp.min(jnp.min(jnp.where(cand == best, flat, ne), axis=1, keepdims=True), axis=0, keepdims=True)
        hit = flat == first
        w = jnp.where(hit, scores, w)
        cand = jnp.where(hit, -jnp.inf, cand)
    tot = jnp.sum(jnp.sum(w, axis=1, keepdims=True), axis=0, keepdims=True)
    comb = (w / tot * ROUTED_SCALE).reshape(ne, tn)
    o_ref[0] = comb
    sub = tn // n_ref.shape[1]
    for s in range(n_ref.shape[1]):
        n_ref[0, s] = jnp.sum(jnp.where(comb[:, s * sub:(s + 1) * sub] > 0.0, 1.0, 0.0), axis=1, keepdims=True)


def _select(logits_t, bias, tn, sub):
    b, ne, l = logits_t.shape
    nb = l // tn
    return pl.pallas_call(
        _select_body,
        grid=(b, nb),
        in_specs=[pl.BlockSpec((1, ne, tn), lambda bi, i: (bi, 0, i)),
                  pl.BlockSpec(bias.shape, lambda bi, i: (0, 0, 0))],
        out_specs=(pl.BlockSpec((1, ne, tn), lambda bi, i: (bi, 0, i)),
                   pl.BlockSpec((1, tn // sub, ne, 1), lambda bi, i: (bi, i, 0, 0))),
        out_shape=(jax.ShapeDtypeStruct((b, ne, l), F32), jax.ShapeDtypeStruct((b, l // sub, ne, 1), F32)),
        compiler_params=_params(("arbitrary", "arbitrary"), 64 * ne * tn * 4),
        name="select",
    )(logits_t, bias)


def _moe_body(order_ref, rounds_ref, x_ref, ct_ref, tri_ref, *refs, per, sub, cap):
    w1_refs, w3_refs, w2_refs = refs[:per], refs[per:2 * per], refs[2 * per:3 * per]
    sw1_ref, sw3_ref, sw2_ref, x1_ref, g2_ref, fnw_ref, o_ref, acc_ref, rank_ref = refs[3 * per:]
    g = pl.program_id(1)
    tm = x_ref.shape[0]
    ns = tm // sub

    @pl.when(g == 0)
    def _():
        x = x_ref[...]
        hs = _silu(_dot(x, sw1_ref[...])) * _dot(x, sw3_ref[...])
        acc_ref[...] = _dot(hs.astype(BF16), sw2_ref[...])
        for s in range(ns):
            chosen = ct_ref[0, :, s * sub:(s + 1) * sub] > 0.0
            before = _dot(jnp.where(chosen, 1.0, 0.0).astype(BF16), tri_ref[...])
            rank_ref[s] = jnp.where(chosen, before, -1.0)

    ids = [order_ref[pl.program_id(0), g * per + i] for i in range(per)]
    ranks = [[rank_ref[s, pl.ds(ids[i], 1), :] for i in range(per)] for s in range(ns)]
    wts = [[ct_ref[0, pl.ds(ids[i], 1), s * sub:(s + 1) * sub] for i in range(per)] for s in range(ns)]
    n_rounds = rounds_ref[pl.program_id(0), g]
    slot = lax.broadcasted_iota(jnp.int32, (cap, sub), 0).astype(F32)

    def one_round(r, carry):
        base = slot + (r * cap).astype(F32)
        packed, spread = [], []
        for s in range(ns):
            hits = [base == ranks[s][i] for i in range(per)]
            pack = jnp.concatenate([jnp.where(h, 1.0, 0.0).astype(BF16) for h in hits], axis=0)
            spread.append(jnp.concatenate([jnp.where(h, wts[s][i], 0.0).astype(BF16)
                                           for i, h in enumerate(hits)], axis=0))
            packed.append(_dot(pack, x_ref[s * sub:(s + 1) * sub, :]).astype(BF16))
        outs = []
        for i in range(per):
            ze = jnp.concatenate([packed[s][i * cap:(i + 1) * cap] for s in range(ns)], axis=0)
            h = _silu(_dot(ze, w1_refs[i][0])) * _dot(ze, w3_refs[i][0])
            outs.append(_dot(h.astype(BF16), w2_refs[i][0]).astype(BF16))
        for s in range(ns):
            ys = jnp.concatenate([outs[i][s * cap:(s + 1) * cap] for i in range(per)], axis=0)
            acc_ref[s * sub:(s + 1) * sub, :] += _tn(spread[s], ys)
        return carry

    lax.fori_loop(0, n_rounds, one_round, 0)

    @pl.when(g == pl.num_programs(1) - 1)
    def _():
        y = x1_ref[...] + g2_ref[0] * acc_ref[...]
        ms = jnp.mean(y * y, axis=-1, keepdims=True)
        o_ref[...] = y * lax.rsqrt(ms + EPS) * fnw_ref[...]


def _moe(xn2, comb_t, counts, w1, w3, w2, sw1, sw3, sw2, x1, g2, fnw, tm):
    t, d = xn2.shape
    ne, _, f = w1.shape
    l = comb_t.shape[2]
    per = EXP_PER_STEP
    gpb = l // tm
    sub, cap = MOE_SUB, MOE_CAP
    tri = jnp.asarray(np.triu(np.ones((sub, sub), np.float32), 1), BF16)
    load = jnp.max(counts.reshape(comb_t.shape[0] * gpb, tm // sub, ne), axis=1).astype(jnp.int32)
    order = jnp.argsort(load, axis=-1).astype(jnp.int32)
    step_load = jnp.max(jnp.take_along_axis(load, order, axis=-1).reshape(-1, ne // per, per), axis=-1)
    rounds = (step_load + (cap - 1)) // cap
    tok = lambda w: pl.BlockSpec((tm, w), lambda i, g, o, r: (i, 0))
    const = lambda shape: pl.BlockSpec(shape, lambda i, g, o, r: (0,) * len(shape))

    def expert(shape):
        return [pl.BlockSpec((1,) + shape, lambda i, g, o, r, k=k: (o[i, g * per + k], 0, 0)) for k in range(per)]

    vm = (2 * tm * d * 2 + 2 * ne * tm * 4 + 2 * 3 * per * d * f * 2 + 2 * 3 * d * f * 2
          + 2 * tm * d * 4 * 2 + tm * d * 4 + 16 * per * cap * (tm // sub) * d)
    grid_spec = pltpu.PrefetchScalarGridSpec(
        num_scalar_prefetch=2,
        grid=(t // tm, ne // per),
        in_specs=([tok(d),
                   pl.BlockSpec((1, ne, tm), lambda i, g, o, r: (i // gpb, 0, i % gpb)),
                   const(tri.shape)]
                  + expert((d, f)) + expert((d, f)) + expert((f, d))
                  + [const(sw1.shape), const(sw3.shape), const(sw2.shape),
                     tok(d),
                     pl.BlockSpec((1, 1, d), lambda i, g, o, r: (i // gpb, 0, 0)),
                     const((1, d))]),
        out_specs=tok(d),
        scratch_shapes=[pltpu.VMEM((tm, d), F32), pltpu.VMEM((tm // sub, ne, sub), F32)])
    return pl.pallas_call(
        functools.partial(_moe_body, per=per, sub=sub, cap=cap),
        grid_spec=grid_spec,
        out_shape=jax.ShapeDtypeStruct((t, d), F32),
        compiler_params=_params(("arbitrary", "arbitrary"), vm),
        name="moe",
    )(order, rounds, xn2, comb_t, tri, *([w1] * per), *([w3] * per), *([w2] * per), sw1, sw3, sw2, x1, g2, fnw)


def _pad_to(a, rows, cols):
    return jnp.pad(a, ((0, rows - a.shape[0]), (0, cols - a.shape[1])))


def kernel(x, c, ctx, c_ctx, ada_w, ada_b, norm1_w, norm2_w, w_in, hy_conv_w, hy_conv_b, hy_w1, hy_b1, hy_freq, hy_w2, hy_b2, hy_w3, hy_bias, gla_a_w2, gla_a_b, gla_norm_w, proj_hy, proj_gla, w_out, router_w, router_bias, exp_w1, exp_w3, exp_w2, sh_w1, sh_w3, sh_w2, final_norm_w):
    b, l, d = x.shape
    assert ada_w.shape[0] == 1, "single-layer block"
    assert l // GRID_W * GRID_W == l and FFT_N2 == GRID_W
    heads = GLA_HEADS
    qk_w = d // 2
    dk = qk_w // heads
    v_w = d
    dv = v_w // heads
    a_w = 2 * GLA_RANK
    hy_w = d
    hy_cols = (HY_ORDER + 1) * hy_w

    rows = -(-(b + 1) // 8) * 8
    cc = jnp.zeros((rows, d), F32).at[:b].set(c).at[b].set(c_ctx)
    mods = _mods(cc, ada_w[0], ada_b[0][None])
    sh1, sc1, g1, sh2, sc2, g2 = [m[:b, None, :] for m in jnp.split(mods, 6, axis=-1)]
    csh1, csc1 = [jnp.broadcast_to(m[b][None, None, :], (b, 1, d)) for m in jnp.split(mods, 6, axis=-1)[:2]]

    w = w_in[0]
    o_a = qk_w + v_w
    o_q = o_a + a_w
    o_g = o_q + qk_w
    o_hy = o_g + v_w
    o_gate = o_hy + hy_cols
    w_k, w_v = w[:, :qk_w], w[:, qk_w:o_a]
    w_a = jnp.pad(w[:, o_a:o_q], ((0, 0), (0, LANES - a_w)))
    wp = jnp.concatenate([w[:, o_gate:], w[:, o_g:o_hy], w_v, w_k, w[:, o_q:o_g], w[:, o_hy:o_gate], w_a],
                         axis=1).astype(BF16)
    p_gate = 0
    p_g = 2 * d
    p_v = p_g + v_w
    p_k = p_v + v_w
    p_q = p_k + qk_w
    p_hy = p_q + qk_w
    p_a = p_hy + hy_cols
    n_all = p_a + LANES
    w_ctx = jnp.concatenate([w_k, w_v, w_a], axis=1).astype(BF16)
    cols_ctx = (0, qk_w, qk_w + v_w, None)
    cols = (p_k, p_v, p_a, p_q)

    nw1 = norm1_w[0][None]
    u_ctx = _inproj(ctx, nw1, csh1, csc1, w_ctx, ctx.shape[1], w_ctx.shape[1])
    u = _inproj(x, nw1, sh1, sc1, wp, 1024, n_all // 5)

    wa = gla_a_w2[0].reshape(2, GLA_RANK, heads, dk).transpose(0, 2, 1, 3)
    waf = jnp.pad(wa[0], ((0, 0), (0, LANES - GLA_RANK), (0, 0))).astype(BF16)
    wab = jnp.pad(wa[1], ((0, 0), (GLA_RANK, LANES - 2 * GLA_RANK), (0, 0))).astype(BF16)
    ba = gla_a_b[0].reshape(2, heads, 1, dk)
    zeros_state = jnp.zeros((b, heads, dv, dk), F32)
    _, _, s_f, s_b = _gla(u_ctx, cols_ctx, waf, wab, ba[0], ba[1], zeros_state, zeros_state, dk, dv, False,
                          ctx.shape[1])
    o_f, o_b, _, _ = _gla(u, cols, waf, wab, ba[0], ba[1], s_f, s_b, dk, dv, True, GLA_BLOCK)

    n = 2 * l
    max_decay = math.log(HY_TARGET) / HY_FAST_DECAY
    min_decay = math.log(HY_TARGET) / HY_SLOW_DECAY
    deltas = jnp.asarray(np.abs(np.linspace(min_decay, max_decay, hy_w, dtype=np.float32))[None])
    ffn = hy_w1.shape[2]
    kern = _hyfilt(l, hy_w,
                   _pad_to(hy_w1[0], LANES, LANES), _pad_to(hy_b1[0][None], 1, LANES),
                   _pad_to(hy_freq[0, 0][None], 1, LANES),
                   _pad_to(hy_w2[0], LANES, LANES), _pad_to(hy_b2[0][None], 1, LANES),
                   _pad_to(hy_freq[0, 1][None], 1, LANES),
                   jnp.pad(hy_w3[0], ((0, LANES - ffn), (0, 0))), deltas)
    tabs = _fft_tables(n, FFT_N2)
    spec = _hyspec(kern, tabs[1], tabs[2], FFT_N2)
    cw, cb = hy_conv_w[0], hy_conv_b[0][None]
    z1 = _hyconv(u, p_hy, u, p_hy + hy_w, cw, cb, 0, hy_w, spec, 0, hy_bias[0, 0][None], tabs, True)
    y_hy = _hyconv(z1, 0, u, p_hy + 2 * hy_w, cw, cb, 0, 2 * hy_w, spec, 1, hy_bias[0, 1][None], tabs, False)

    x1, xn2, logits_t = _mix(x, o_f, o_b, u, p_g, p_gate, y_hy, gla_norm_w[0][None],
                             proj_hy[0].astype(BF16), proj_gla[0].astype(BF16), w_out[0].astype(BF16),
                             g1, norm2_w[0][None], sh2, sc2, router_w[0].T, 512)
    comb, counts = _select(logits_t, router_bias[0].reshape(N_GROUPS, N_EXPERTS // N_GROUPS, 1), 512, MOE_SUB)
    out = _moe(xn2.reshape(b * l, d), comb, counts, exp_w1[0].astype(BF16), exp_w3[0].astype(BF16),
               exp_w2[0].astype(BF16), sh_w1[0].astype(BF16), sh_w3[0].astype(BF16), sh_w2[0].astype(BF16),
               x1.reshape(b * l, d), g2, final_norm_w[None], 1024)
    return out.reshape(b, l, d)
```

```python
import functools
import math

import jax
import jax.numpy as jnp
import numpy as np
from jax import lax
from jax.experimental import pallas as pl
from jax.experimental.pallas import tpu as pltpu

F32 = jnp.float32
BF16 = jnp.bfloat16
HIGHEST = lax.Precision.HIGHEST

GRID_W = 64
EPS = 1e-6
HY_ORDER = 2
HY_BANDS = 16
HY_FAST_DECAY = 0.3
HY_SLOW_DECAY = 1.5
HY_TARGET = 1e-2
GLA_HEADS = 4
GLA_RANK = 16
GLA_TAU = 16.0
N_EXPERTS = 64
N_GROUPS = 8
TOPK_GROUPS = 4
TOP_K = 8
ROUTED_SCALE = 2.5

LANES = 128
V7X_VMEM_BYTES = 64 * 1024 * 1024
VMEM_CAP_BYTES = 56 * 1024 * 1024

GLA_CHUNK = 256
GLA_BLOCK = 512
GLA_VPU_MIN_HALF = 4
FFT_N2 = 64
STRIDE_PAD = 8
STAGE_UNROLL = 64
ROW_UNROLL = 2
EXP_PER_STEP = 4
MOE_SUB = 256
MOE_CAP = 64


def _params(sem, vmem_bytes):
    limit = int(min(VMEM_CAP_BYTES, max(16 * 1024 * 1024, vmem_bytes * 5 // 4 + (2 << 20))))
    return pltpu.CompilerParams(dimension_semantics=sem, vmem_limit_bytes=limit)


def _nt(a, b, **kw):
    return lax.dot_general(a, b, (((1,), (1,)), ((), ())), preferred_element_type=F32, **kw)


def _tn(a, b):
    return lax.dot_general(a, b, (((0,), (0,)), ((), ())), preferred_element_type=F32)


def _dot(a, b, **kw):
    return jnp.dot(a, b, preferred_element_type=F32, **kw)


def _silu(x):
    return x * jax.nn.sigmoid(x)


def _mods_body(c_ref, w_ref, b_ref, o_ref):
    o_ref[...] = _dot(_silu(c_ref[...]), w_ref[...], precision=HIGHEST) + b_ref[...]


def _mods(cc, w, b):
    rows, d = cc.shape
    n = w.shape[1]
    tn = n // 4
    return pl.pallas_call(
        _mods_body,
        grid=(n // tn,),
        in_specs=[pl.BlockSpec((rows, d), lambda j: (0, 0)),
                  pl.BlockSpec((d, tn), lambda j: (0, j)),
                  pl.BlockSpec((1, tn), lambda j: (0, j))],
        out_specs=pl.BlockSpec((rows, tn), lambda j: (0, j)),
        out_shape=jax.ShapeDtypeStruct((rows, n), F32),
        compiler_params=_params(("arbitrary",), 2 * d * tn * 4),
        name="mods",
    )(cc, w, b)


def _norm_mod(x, w, shift, scale):
    ms = jnp.mean(x * x, axis=-1, keepdims=True)
    return (x * lax.rsqrt(ms + EPS) * w) * (1.0 + scale) + shift


def _inproj_body(x_ref, nw_ref, sh_ref, sc_ref, w_ref, o_ref, xn_ref):
    @pl.when(pl.program_id(2) == 0)
    def _():
        xn_ref[...] = _norm_mod(x_ref[0], nw_ref[...], sh_ref[0], sc_ref[0]).astype(BF16)

    o_ref[0] = _dot(xn_ref[...], w_ref[...]).astype(BF16)


def _inproj(x, nw, shift, scale, w, tm, tn):
    b, l, d = x.shape
    n = w.shape[1]
    vm = 2 * tm * d * 4 + 2 * d * tn * 2 + 2 * tm * tn * 2 + tm * d * 2
    return pl.pallas_call(
        _inproj_body,
        grid=(b, l // tm, n // tn),
        in_specs=[pl.BlockSpec((1, tm, d), lambda bi, i, j: (bi, i, 0)),
                  pl.BlockSpec((1, d), lambda bi, i, j: (0, 0)),
                  pl.BlockSpec((1, 1, d), lambda bi, i, j: (bi, 0, 0)),
                  pl.BlockSpec((1, 1, d), lambda bi, i, j: (bi, 0, 0)),
                  pl.BlockSpec((d, tn), lambda bi, i, j: (0, j))],
        out_specs=pl.BlockSpec((1, tm, tn), lambda bi, i, j: (bi, i, j)),
        out_shape=jax.ShapeDtypeStruct((b, l, n), BF16),
        scratch_shapes=[pltpu.VMEM((tm, d), BF16)],
        compiler_params=_params(("arbitrary", "arbitrary", "arbitrary"), vm),
        name="inproj",
    )(x, nw, shift, scale, w)


def _gla_tables(c, inclusive, flip):
    idx = np.arange(c)
    i = idx[:, None]
    x = idx[None, :]
    blocks = [x <= i]
    masks = []
    h = c // 2
    while h >= 1:
        mid = (idx // (2 * h)) * (2 * h) + h
        mi = mid[:, None]
        hi = i if inclusive else i - 1
        if h < GLA_VPU_MIN_HALF:
            blocks.append(((i >= mi) & (x >= mi) & (x <= hi)) | ((i < mi) & (x > i) & (x <= mi - 1)))
        same = (idx[:, None] // (2 * h)) == (idx[None, :] // (2 * h))
        masks.append(same & (idx[:, None] >= mi) & (idx[None, :] < mid[None, :]))
        h //= 2
    masks.append(np.eye(c, dtype=bool))
    if flip:
        blocks = [b[::-1, ::-1] for b in blocks]
        masks = [m[::-1, ::-1] for m in masks]
    lall = np.concatenate(blocks + [np.ones((8, c), bool)], axis=0)
    return lall.astype(np.float32), np.stack(masks).astype(np.float32)


def _gla_chunk(q, k, v, a, wa, ba, lall, masks_ref, st_ref, inclusive, flip, q_scale):
    c, dk = k.shape
    n_levels = int(math.log2(c))
    xg = _dot(a, wa) + ba
    g = (jnp.minimum(xg, 0.0) - jnp.log(1.0 + jnp.exp(-jnp.abs(xg)))) * (1.0 / GLA_TAU)
    g_hi = g.astype(BF16)
    g_lo = (g - g_hi.astype(F32)).astype(BF16)
    e2 = _dot(lall, jnp.concatenate([g_hi, g_lo], axis=1))
    e = e2[:, :dk] + e2[:, dk:]
    run = e[0:c]
    tot = e[e.shape[0] - 8:e.shape[0] - 7]
    upto = run if inclusive else run - g

    def decay(t):
        return jnp.exp(jnp.minimum(t, 0.0))

    row = lax.broadcasted_iota(jnp.int32, (c, dk), 0)
    n_vpu = n_levels - int(math.log2(GLA_VPU_MIN_HALF))

    def level_decay(lv):
        h = c >> (lv + 1)
        if lv >= n_vpu:
            return decay(e[(1 + lv - n_vpu) * c:(2 + lv - n_vpu) * c])
        p0 = h if flip else h - 1
        piv = jnp.concatenate([jnp.broadcast_to(run[m + p0:m + p0 + 1, :], (2 * h, dk))
                               for m in range(0, c, 2 * h)], axis=0)
        if inclusive:
            return jnp.exp(-jnp.abs(run - piv))
        key_side = ((row & (2 * h - 1)) >= h) if flip else ((row & (2 * h - 1)) < h)
        return decay(jnp.where(key_side, piv - run, upto - piv))

    kf = k.astype(F32)
    st = st_ref[...]
    k1 = (kf * decay(tot - run)).astype(BF16)
    st_ref[...] = st * decay(tot) + _tn(v, k1)
    if q is None:
        return None
    qf = q.astype(F32) * q_scale
    o = _nt((qf * decay(upto)).astype(BF16), st.astype(BF16))
    attn = jnp.zeros((c, c), BF16)
    for lv in range(n_levels):
        ex_l = level_decay(lv)
        attn = attn + _nt((qf * ex_l).astype(BF16), (kf * ex_l).astype(BF16)).astype(BF16) * masks_ref[lv]
    if inclusive:
        attn = attn + _nt(qf.astype(BF16), k).astype(BF16) * masks_ref[n_levels]
    return o + _dot(attn, v)


def _gla_body(*refs, with_q, n_sub, chunk, q_scale):
    if with_q:
        (kf_ref, vf_ref, af_ref, qf_ref, kb_ref, vb_ref, ab_ref, qb_ref, waf_ref, wab_ref, baf_ref, bab_ref,
         lf_ref, lb_ref, mf_ref, mb_ref, s0f_ref, s0b_ref, of_ref, ob_ref, sf_ref, sb_ref, stf_ref, stb_ref) = refs
    else:
        (kf_ref, vf_ref, af_ref, kb_ref, vb_ref, ab_ref, waf_ref, wab_ref, baf_ref, bab_ref,
         lf_ref, lb_ref, mf_ref, mb_ref, s0f_ref, s0b_ref, sf_ref, sb_ref, stf_ref, stb_ref) = refs
        qf_ref = qb_ref = of_ref = ob_ref = None

    @pl.when(pl.program_id(2) == 0)
    def _():
        stf_ref[...] = s0f_ref[0, 0]
        stb_ref[...] = s0b_ref[0, 0]

    for s in range(n_sub):
        sl = slice(s * chunk, (s + 1) * chunk)
        o = _gla_chunk(None if qf_ref is None else qf_ref[0, sl, :], kf_ref[0, sl, :], vf_ref[0, sl, :],
                       af_ref[0, sl, :], waf_ref[0], baf_ref[0], lf_ref[...], mf_ref, stf_ref, True, False, q_scale)
        if with_q:
            of_ref[0, sl, :] = o.astype(BF16)
    for s in reversed(range(n_sub)):
        sl = slice(s * chunk, (s + 1) * chunk)
        o = _gla_chunk(None if qb_ref is None else qb_ref[0, sl, :], kb_ref[0, sl, :], vb_ref[0, sl, :],
                       ab_ref[0, sl, :], wab_ref[0], bab_ref[0], lb_ref[...], mb_ref, stb_ref, False, True, q_scale)
        if with_q:
            ob_ref[0, sl, :] = o.astype(BF16)
    sf_ref[0, 0] = stf_ref[...]
    sb_ref[0, 0] = stb_ref[...]


def _gla(u, cols, waf, wab, baf, bab, s0f, s0b, dk, dv, with_q, tb):
    b, l, _ = u.shape
    h = GLA_HEADS
    nb = l // tb
    n_sub = tb // GLA_CHUNK
    lf, mf = _gla_tables(GLA_CHUNK, True, False)
    lb, mb = _gla_tables(GLA_CHUNK, False, True)
    lf, lb = jnp.asarray(lf, BF16), jnp.asarray(lb, BF16)
    mf, mb = jnp.asarray(mf, BF16), jnp.asarray(mb, BF16)
    kc, vc, ac, qc = cols

    def seq_specs(rev):
        def blk(i):
            return (nb - 1 - i) if rev else i
        specs = [pl.BlockSpec((1, tb, dk), lambda bi, hi, i: (bi, blk(i), kc // dk + hi)),
                 pl.BlockSpec((1, tb, dv), lambda bi, hi, i: (bi, blk(i), vc // dv + hi)),
                 pl.BlockSpec((1, tb, LANES), lambda bi, hi, i: (bi, blk(i), ac // LANES))]
        if with_q:
            specs.append(pl.BlockSpec((1, tb, dk), lambda bi, hi, i: (bi, blk(i), qc // dk + hi)))
        return specs

    def const_spec(shape):
        nd = len(shape)
        return pl.BlockSpec(shape, lambda bi, hi, i: (0,) * nd)

    head_w = pl.BlockSpec((1, LANES, dk), lambda bi, hi, i: (hi, 0, 0))
    head_b = pl.BlockSpec((1, 1, dk), lambda bi, hi, i: (hi, 0, 0))
    st_spec = pl.BlockSpec((1, 1, dv, dk), lambda bi, hi, i: (bi, hi, 0, 0))
    in_specs = (seq_specs(False) + seq_specs(True) + [head_w, head_w, head_b, head_b,
                const_spec(lf.shape), const_spec(lb.shape), const_spec(mf.shape), const_spec(mb.shape),
                st_spec, st_spec])
    st_shape = jax.ShapeDtypeStruct((b, h, dv, dk), F32)
    if with_q:
        o_shape = jax.ShapeDtypeStruct((b, l, h * dv), BF16)
        out_shape = (o_shape, o_shape, st_shape, st_shape)
        out_specs = (pl.BlockSpec((1, tb, dv), lambda bi, hi, i: (bi, i, hi)),
                     pl.BlockSpec((1, tb, dv), lambda bi, hi, i: (bi, nb - 1 - i, hi)),
                     st_spec, st_spec)
        args = (u,) * 8
    else:
        out_shape = (st_shape, st_shape)
        out_specs = (st_spec, st_spec)
        args = (u,) * 6
    vm = 4 * tb * (2 * dk + dv + LANES) * 2 * 2 + 8 * dv * dk * 4 + 4 * tb * dv * 2 + (4 << 20)
    outs = pl.pallas_call(
        functools.partial(_gla_body, with_q=with_q, n_sub=n_sub, chunk=GLA_CHUNK, q_scale=dk ** -0.5),
        grid=(b, h, nb),
        in_specs=in_specs,
        out_specs=out_specs,
        out_shape=out_shape,
        scratch_shapes=[pltpu.VMEM((dv, dk), F32), pltpu.VMEM((dv, dk), F32)],
        compiler_params=_params(("arbitrary", "arbitrary", "arbitrary"), vm),
        name="gla" if with_q else "gla_ctx",
    )(*args, waf, wab, baf, bab, lf, lb, mf, mb, s0f, s0b)
    if with_q:
        return outs
    return None, None, outs[0], outs[1]


def _hy_tables(l):
    t = np.linspace(0.0, 1.0, l, dtype=np.float32).astype(np.float64)[:, None]
    w = 2.0 * math.pi * np.arange(l, dtype=np.float64)[:, None] / l
    f = np.linspace(1e-4, HY_BANDS - 1, HY_BANDS, dtype=np.float32).astype(np.float64)[None, :]
    z = np.concatenate([t, np.cos(f * w), -np.sin(f * w)], axis=-1)
    rev = (l - np.arange(l)) % l

    def pad(a):
        out = np.zeros((l, LANES), np.float32)
        out[:, :a.shape[1]] = a
        return out

    tt = np.broadcast_to(t, (l, LANES)).astype(np.float32)
    return pad(z), pad(z[rev]), tt, np.ascontiguousarray(tt[rev])


def _hyfilt_body(z1_ref, z2_ref, t1_ref, t2_ref, w1_ref, b1_ref, f1_ref, w2_ref, b2_ref, f2_ref,
                 w3f_ref, w3b_ref, dl_ref, o_ref, ha_ref, hb_ref):
    l = z1_ref.shape[0]

    def split(a):
        hi = a.astype(BF16)
        return hi, (a - hi.astype(F32)).astype(BF16)

    @pl.when((pl.program_id(0) == 0) & (pl.program_id(1) == 0))
    def _():
        for z_ref, h_ref in ((z1_ref, ha_ref), (z2_ref, hb_ref)):
            h = jnp.sin(f1_ref[...] * (_dot(z_ref[...], w1_ref[...], precision=HIGHEST) + b1_ref[...]))
            h = jnp.sin(f2_ref[...] * (_dot(h, w2_ref[...], precision=HIGHEST) + b2_ref[...]))
            h_ref[0], h_ref[1] = split(h)

    def out_proj(h_ref, w_ref):
        w_hi, w_lo = split(w_ref[...])
        return _dot(h_ref[0], w_hi) + (_dot(h_ref[0], w_lo) + _dot(h_ref[1], w_hi))

    dl = dl_ref[...]
    hf = out_proj(ha_ref, w3f_ref) * jnp.exp(-t1_ref[...] * dl)
    hb = out_proj(hb_ref, w3b_ref) * jnp.exp(-t2_ref[...] * dl)
    row = lax.broadcasted_iota(jnp.int32, hb.shape, 0)
    hb = jnp.where(row == 0, 0.0, hb)
    ss = jnp.sum(hf * hf, axis=0, keepdims=True) + jnp.sum(hb * hb, axis=0, keepdims=True)
    scale = lax.rsqrt(ss)
    o_ref[0, 0:l, :] = hf * scale
    o_ref[0, l:2 * l, :] = hb * scale


def _hyfilt(l, c, w1, b1, f1, w2, b2, f2, w3, deltas):
    z1, z2, t1, t2 = (jnp.asarray(a) for a in _hy_tables(l))
    ncb = c // LANES
    tab = pl.BlockSpec((l, LANES), lambda o, j: (0, 0))
    sq = pl.BlockSpec((LANES, LANES), lambda o, j: (0, 0))
    row = pl.BlockSpec((1, LANES), lambda o, j: (0, 0))
    return pl.pallas_call(
        _hyfilt_body,
        grid=(HY_ORDER, ncb),
        in_specs=[tab, tab, tab, tab, sq, row, row, sq, row, row,
                  pl.BlockSpec((LANES, LANES), lambda o, j: (0, 2 * o * ncb + j)),
                  pl.BlockSpec((LANES, LANES), lambda o, j: (0, (2 * o + 1) * ncb + j)),
                  pl.BlockSpec((1, LANES), lambda o, j: (0, j))],
        out_specs=pl.BlockSpec((1, 2 * l, LANES), lambda o, j: (o, 0, j)),
        out_shape=jax.ShapeDtypeStruct((HY_ORDER, 2 * l, c), F32),
        scratch_shapes=[pltpu.VMEM((2, l, LANES), BF16), pltpu.VMEM((2, l, LANES), BF16)],
        compiler_params=_params(("arbitrary", "arbitrary"), 24 * l * LANES * 4),
        name="hyfilt",
    )(z1, z2, t1, t2, w1, b1, f1, w2, b2, f2, w3, w3, deltas)


def _fft_tables(n, n2):
    n1 = n // n2
    h = n1 // 2
    k1 = np.arange(n1)[:, None]
    a = 2.0 * math.pi * k1 * np.arange(h)[None, :] / n1
    c, s = np.cos(a), np.sin(a)
    f1c = np.block([[c, s], [-s, c]])
    a = 2.0 * math.pi * k1 * np.arange(n1)[None, :] / n1
    f1r = np.concatenate([np.cos(a), -np.sin(a)], axis=0)
    kk = np.arange(n1)[:, None, None] + n1 * np.arange(n2)[None, :, None]
    a = 2.0 * math.pi * kk * np.arange(n2)[None, None, :] / n
    c, s = np.cos(a), np.sin(a)
    gf = np.concatenate([np.concatenate([c, s], axis=2), np.concatenate([-s, c], axis=2)], axis=1)
    a = 2.0 * math.pi * np.arange(h)[:, None] * np.arange(n1)[None, :] / n1
    c, s = np.cos(a), np.sin(a)
    if1 = np.block([[c, -s], [s, c]])
    return tuple(jnp.asarray(m, BF16) for m in (f1c, f1r, gf, if1))


def _rows8(start, size):
    return pl.ds(pl.multiple_of(start, 8), size)


def _hyspec_body(k_ref, f1_ref, gf_ref, o_ref, as_ref, *, n, n2, ap):
    n1 = n // n2

    def stage1(j, carry):
        r = k_ref[0, pl.ds(j, n1, stride=n2), :].astype(BF16)
        as_ref[_rows8(j * ap, 2 * n1), :] = _dot(f1_ref[...], r)
        return carry

    lax.fori_loop(0, n2, stage1, 0, unroll=STAGE_UNROLL)

    def stage2(k1, carry):
        r = jnp.concatenate([as_ref[pl.ds(k1, n2, stride=ap), :],
                             as_ref[pl.ds(n1 + k1, n2, stride=ap), :]], axis=0).astype(BF16)
        o_ref[0, k1] = (_dot(gf_ref[k1], r) * (1.0 / n)).astype(BF16)
        return carry

    lax.fori_loop(0, n1, stage2, 0, unroll=STAGE_UNROLL)


def _hyspec(kern, f1r, gf, n2):
    order, n, c = kern.shape
    n1 = n // n2
    ap = 2 * n1 + STRIDE_PAD
    return pl.pallas_call(
        functools.partial(_hyspec_body, n=n, n2=n2, ap=ap),
        grid=(order, c // LANES),
        in_specs=[pl.BlockSpec((1, n, LANES), lambda o, j: (o, 0, j)),
                  pl.BlockSpec(f1r.shape, lambda o, j: (0, 0)),
                  pl.BlockSpec(gf.shape, lambda o, j: (0, 0, 0))],
        out_specs=pl.BlockSpec((1, n1, 2 * n2, LANES), lambda o, j: (o, 0, 0, j)),
        out_shape=jax.ShapeDtypeStruct((order, n1, 2 * n2, c), BF16),
        scratch_shapes=[pltpu.VMEM((n2 * ap, LANES), F32)],
        compiler_params=_params(("arbitrary", "arbitrary"),
                                2 * n * LANES * 4 + 2 * gf.size * 2 + n2 * ap * LANES * 4 + 2 * n * LANES * 2),
        name="hyspec",
    )(kern, f1r, gf)


def _short_conv(u, w_ref, b_ref):
    r = u.shape[0]
    row = lax.broadcasted_iota(jnp.int32, u.shape, 0)
    up = jnp.where(row == 0, 0.0, pltpu.roll(u, 1, axis=0))
    dn = jnp.where(row == r - 1, 0.0, pltpu.roll(u, r - 1, axis=0))
    return up * w_ref[0:1, :] + u * w_ref[1:2, :] + dn * w_ref[2:3, :] + b_ref[...]


def _hyconv_body(z_ref, g_ref, zw_ref, zb_ref, gw_ref, gb_ref, sp_ref, hb_ref, f1_ref, gf_ref, if1_ref,
                 o_ref, x_ref, as_ref, bs_ref, y_ref, *, conv_z, n2, xp, ap):
    n1h = z_ref.shape[1] // n2
    n1 = 2 * n1h
    half = n1h * xp

    def fill(i, carry):
        for p in range(2):
            u = z_ref[p, pl.ds(pl.multiple_of(i * n2, n2), n2), :].astype(F32)
            x_ref[_rows8(p * half + i * xp, n2), :] = _short_conv(u, zw_ref, zb_ref) if conv_z else u
        return carry

    lax.fori_loop(0, n1h, fill, 0, unroll=ROW_UNROLL)

    def stage1(j, carry):
        r = jnp.concatenate([x_ref[pl.ds(j, n1h, stride=xp), :],
                             x_ref[pl.ds(half + j, n1h, stride=xp), :]], axis=0).astype(BF16)
        as_ref[_rows8(j * ap, 2 * n1), :] = _dot(f1_ref[...], r)
        return carry

    lax.fori_loop(0, n2, stage1, 0, unroll=STAGE_UNROLL)

    def stage2(k1, carry):
        r = jnp.concatenate([as_ref[pl.ds(k1, n2, stride=ap), :],
                             as_ref[pl.ds(n1 + k1, n2, stride=ap), :]], axis=0).astype(BF16)
        gk = gf_ref[k1]
        xk = _dot(gk, r)
        xr, xi = xk[0:n2], xk[n2:2 * n2]
        sp = sp_ref[0, k1].astype(F32)
        sr, si = sp[0:n2], sp[n2:2 * n2]
        yk = jnp.concatenate([xr * sr - xi * si, xr * si + xi * sr], axis=0).astype(BF16)
        bk = _tn(gk, yk)
        bs_ref[pl.ds(k1, n2, stride=ap), :] = bk[0:n2]
        bs_ref[pl.ds(n1 + k1, n2, stride=ap), :] = bk[n2:2 * n2]
        return carry

    lax.fori_loop(0, n1, stage2, 0, unroll=2 * STAGE_UNROLL)

    def stage3(j, carry):
        yn = _dot(if1_ref[...], bs_ref[_rows8(j * ap, 2 * n1), :].astype(BF16))
        y_ref[pl.ds(j, n1h, stride=xp), :] = yn[0:n1h]
        y_ref[pl.ds(half + j, n1h, stride=xp), :] = yn[n1h:n1]
        return carry

    lax.fori_loop(0, n2, stage3, 0, unroll=STAGE_UNROLL)

    def finish(i, carry):
        rows = pl.ds(pl.multiple_of(i * n2, n2), n2)
        for p in range(2):
            gate = _short_conv(g_ref[p, rows, :].astype(F32), gw_ref, gb_ref)
            z = x_ref[_rows8(p * half + i * xp, n2), :]
            y = y_ref[_rows8(p * half + i * xp, n2), :]
            o_ref[p, rows, :] = (gate * (y + z * hb_ref[...])).astype(BF16)
        return carry

    lax.fori_loop(0, n1h, finish, 0, unroll=ROW_UNROLL)


def _hyconv(z, z_col, g, g_col, conv_w, conv_b, zw_col, gw_col, spec, order, hy_bias, tabs, conv_z):
    b, l, _ = z.shape
    f1c, _, gf, if1 = tabs
    n1, n2x2 = gf.shape[0], gf.shape[1]
    n2 = n2x2 // 2
    n1h = n1 // 2
    c = spec.shape[-1]
    ncb = c // LANES
    xp = n2 + STRIDE_PAD
    ap = 2 * n1 + STRIDE_PAD
    vm = (2 * 2 * 2 * l * LANES * 2 + 2 * 2 * l * LANES * 2 + 2 * n1 * n2x2 * LANES * 2
          + 2 * gf.size * 2 + 2 * 2 * n1h * xp * LANES * 4 + 2 * n2 * ap * LANES * 4)
    return pl.pallas_call(
        functools.partial(_hyconv_body, conv_z=conv_z, n2=n2, xp=xp, ap=ap),
        grid=(ncb, b // 2),
        in_specs=[pl.BlockSpec((2, l, LANES), lambda j, p: (p, 0, z_col // LANES + j)),
                  pl.BlockSpec((2, l, LANES), lambda j, p: (p, 0, g_col // LANES + j)),
                  pl.BlockSpec((3, LANES), lambda j, p: (0, zw_col // LANES + j)),
                  pl.BlockSpec((1, LANES), lambda j, p: (0, zw_col // LANES + j)),
                  pl.BlockSpec((3, LANES), lambda j, p: (0, gw_col // LANES + j)),
                  pl.BlockSpec((1, LANES), lambda j, p: (0, gw_col // LANES + j)),
                  pl.BlockSpec((1, n1, n2x2, LANES), lambda j, p: (order, 0, 0, j)),
                  pl.BlockSpec((1, LANES), lambda j, p: (0, j)),
                  pl.BlockSpec(f1c.shape, lambda j, p: (0, 0)),
                  pl.BlockSpec(gf.shape, lambda j, p: (0, 0, 0)),
                  pl.BlockSpec(if1.shape, lambda j, p: (0, 0))],
        out_specs=pl.BlockSpec((2, l, LANES), lambda j, p: (p, 0, j)),
        out_shape=jax.ShapeDtypeStruct((b, l, c), BF16),
        scratch_shapes=[pltpu.VMEM((2 * n1h * xp, LANES), F32),
                        pltpu.VMEM((n2 * ap, LANES), F32),
                        pltpu.VMEM((n2 * ap, LANES), F32),
                        pltpu.VMEM((2 * n1h * xp, LANES), F32)],
        compiler_params=_params(("arbitrary", "arbitrary"), vm),
        name="hyconv%d" % order,
    )(z, g, conv_w, conv_b, conv_w, conv_b, spec, hy_bias, f1c, gf, if1)


def _mix_body(x_ref, of_ref, ob_ref, ug_ref, ugate_ref, yhy_ref, gnw_ref, phy_ref, pgla_ref, wout_ref,
              g1_ref, n2w_ref, sh2_ref, sc2_ref, rwt_ref, x1_ref, xn2_ref, lg_ref, *, heads):
    d = x_ref.shape[2]
    o = of_ref[0].astype(F32) + ob_ref[0].astype(F32)
    dv = o.shape[1] // heads
    parts = []
    for h in range(heads):
        seg = o[:, h * dv:(h + 1) * dv]
        parts.append(seg * lax.rsqrt(jnp.mean(seg * seg, axis=-1, keepdims=True) + EPS))
    y_gla = jnp.concatenate(parts, axis=1) * gnw_ref[...] * _silu(ug_ref[0].astype(F32))
    gates = jax.nn.sigmoid(ugate_ref[0].astype(F32))
    merged = (gates[:, :d] * _dot(yhy_ref[0], phy_ref[...])
              + gates[:, d:] * _dot(y_gla.astype(BF16), pgla_ref[...]))
    x1 = x_ref[0] + g1_ref[0] * _dot(merged.astype(BF16), wout_ref[...])
    x1_ref[0] = x1
    xn2 = _norm_mod(x1, n2w_ref[...], sh2_ref[0], sc2_ref[0])
    xn2_ref[0] = xn2.astype(BF16)
    lg_ref[0] = _nt(rwt_ref[...], xn2, precision=HIGHEST)


def _mix(x, o_f, o_b, u, g_col, gate_col, y_hy, gnw, phy, pgla, wout, g1, n2w, sh2, sc2, rwt, tm):
    b, l, d = x.shape
    vw = o_f.shape[2]
    ne = rwt.shape[0]
    tok = lambda w: pl.BlockSpec((1, tm, w), lambda bi, i: (bi, i, 0))
    per_b = pl.BlockSpec((1, 1, d), lambda bi, i: (bi, 0, 0))
    const = lambda shape: pl.BlockSpec(shape, lambda bi, i: (0, 0))
    vm = (2 * tm * d * 4 * 2 + 2 * tm * (3 * vw + 2 * d + 2 * d) * 2 + 2 * 3 * d * d * 2 + 12 * tm * d * 4)
    return pl.pallas_call(
        functools.partial(_mix_body, heads=GLA_HEADS),
        grid=(b, l // tm),
        in_specs=[tok(d), tok(vw), tok(vw),
                  pl.BlockSpec((1, tm, vw), lambda bi, i: (bi, i, g_col // vw)),
                  pl.BlockSpec((1, tm, 2 * d), lambda bi, i: (bi, i, gate_col // (2 * d))),
                  tok(d), const((1, vw)), const((d, d)), const((vw, d)), const((d, d)),
                  per_b, const((1, d)), per_b, per_b, const((ne, d))],
        out_specs=(tok(d), tok(d), pl.BlockSpec((1, ne, tm), lambda bi, i: (bi, 0, i))),
        out_shape=(jax.ShapeDtypeStruct((b, l, d), F32), jax.ShapeDtypeStruct((b, l, d), BF16),
                   jax.ShapeDtypeStruct((b, ne, l), F32)),
        compiler_params=_params(("arbitrary", "arbitrary"), vm),
        name="mix",
    )(x, o_f, o_b, u, u, y_hy, gnw, phy, pgla, wout, g1, n2w, sh2, sc2, rwt)


def _select_body(lg_ref, bias_ref, o_ref, n_ref):
    ne, tn = lg_ref.shape[1], lg_ref.shape[2]
    ng = N_GROUPS
    pg = ne // ng
    scores = jax.nn.sigmoid(lg_ref[0]).reshape(ng, pg, tn)
    sel = scores + bias_ref[...]
    ie = lax.broadcasted_iota(jnp.int32, sel.shape, 1)
    m1 = jnp.max(sel, axis=1, keepdims=True)
    i1 = jnp.min(jnp.where(sel == m1, ie, pg), axis=1, keepdims=True)
    m2 = jnp.max(jnp.where(ie == i1, -jnp.inf, sel), axis=1, keepdims=True)
    grp = m1 + m2
    ig = lax.broadcasted_iota(jnp.int32, grp.shape, 0)
    rank = jnp.zeros(grp.shape, jnp.int32)
    for g in range(ng):
        other = grp[g:g + 1]
        rank = rank + jnp.where((other > grp) | ((other == grp) & (g < ig)), 1, 0)
    cand = jnp.where(rank < TOPK_GROUPS, sel, -jnp.inf)
    flat = ig * pg + ie
    w = jnp.zeros(sel.shape, F32)
    for _ in range(TOP_K):
        best = jnp.max(jnp.max(cand, axis=1, keepdims=True), axis=0, keepdims=True)
        first = jnp.min(jnp.min(jnp.where(cand == best, flat, ne), axis=1, keepdims=True), axis=0, keepdims=True)
        hit = flat == first
        w = jnp.where(hit, scores, w)
        cand = jnp.where(hit, -jnp.inf, cand)
    tot = jnp.sum(jnp.sum(w, axis=1, keepdims=True), axis=0, keepdims=True)
    comb = (w / tot * ROUTED_SCALE).reshape(ne, tn)
    o_ref[0] = comb
    sub = tn // n_ref.shape[1]
    for s in range(n_ref.shape[1]):
        n_ref[0, s] = jnp.sum(jnp.where(comb[:, s * sub:(s + 1) * sub] > 0.0, 1.0, 0.0), axis=1, keepdims=True)


def _select(logits_t, bias, tn, sub):
    b, ne, l = logits_t.shape
    nb = l // tn
    return pl.pallas_call(
        _select_body,
        grid=(b, nb),
        in_specs=[pl.BlockSpec((1, ne, tn), lambda bi, i: (bi, 0, i)),
                  pl.BlockSpec(bias.shape, lambda bi, i: (0, 0, 0))],
        out_specs=(pl.BlockSpec((1, ne, tn), lambda bi, i: (bi, 0, i)),
                   pl.BlockSpec((1, tn // sub, ne, 1), lambda bi, i: (bi, i, 0, 0))),
        out_shape=(jax.ShapeDtypeStruct((b, ne, l), F32), jax.ShapeDtypeStruct((b, l // sub, ne, 1), F32)),
        compiler_params=_params(("arbitrary", "arbitrary"), 64 * ne * tn * 4),
        name="select",
    )(logits_t, bias)


def _moe_body(order_ref, rounds_ref, x_ref, ct_ref, tri_ref, *refs, per, sub, cap):
    w1_refs, w3_refs, w2_refs = refs[:per], refs[per:2 * per], refs[2 * per:3 * per]
    sw1_ref, sw3_ref, sw2_ref, x1_ref, g2_ref, fnw_ref, o_ref, acc_ref, rank_ref = refs[3 * per:]
    g = pl.program_id(1)
    tm = x_ref.shape[0]
    ns = tm // sub

    @pl.when(g == 0)
    def _():
        x = x_ref[...]
        hs = _silu(_dot(x, sw1_ref[...])) * _dot(x, sw3_ref[...])
        acc_ref[...] = _dot(hs.astype(BF16), sw2_ref[...])
        for s in range(ns):
            chosen = ct_ref[0, :, s * sub:(s + 1) * sub] > 0.0
            before = _dot(jnp.where(chosen, 1.0, 0.0).astype(BF16), tri_ref[...])
            rank_ref[s] = jnp.where(chosen, before, -1.0)

    ids = [order_ref[pl.program_id(0), g * per + i] for i in range(per)]
    ranks = [[rank_ref[s, pl.ds(ids[i], 1), :] for i in range(per)] for s in range(ns)]
    wts = [[ct_ref[0, pl.ds(ids[i], 1), s * sub:(s + 1) * sub] for i in range(per)] for s in range(ns)]
    n_rounds = rounds_ref[pl.program_id(0), g]
    slot = lax.broadcasted_iota(jnp.int32, (cap, sub), 0).astype(F32)

    def one_round(r, carry):
        base = slot + (r * cap).astype(F32)
        packed, spread = [], []
        for s in range(ns):
            hits = [base == ranks[s][i] for i in range(per)]
            pack = jnp.concatenate([jnp.where(h, 1.0, 0.0).astype(BF16) for h in hits], axis=0)
            spread.append(jnp.concatenate([jnp.where(h, wts[s][i], 0.0).astype(BF16)
                                           for i, h in enumerate(hits)], axis=0))
            packed.append(_dot(pack, x_ref[s * sub:(s + 1) * sub, :]).astype(BF16))
        outs = []
        for i in range(per):
            ze = jnp.concatenate([packed[s][i * cap:(i + 1) * cap] for s in range(ns)], axis=0)
            h = _silu(_dot(ze, w1_refs[i][0])) * _dot(ze, w3_refs[i][0])
            outs.append(_dot(h.astype(BF16), w2_refs[i][0]).astype(BF16))
        for s in range(ns):
            ys = jnp.concatenate([outs[i][s * cap:(s + 1) * cap] for i in range(per)], axis=0)
            acc_ref[s * sub:(s + 1) * sub, :] += _tn(spread[s], ys)
        return carry

    lax.fori_loop(0, n_rounds, one_round, 0)

    @pl.when(g == pl.num_programs(1) - 1)
    def _():
        y = x1_ref[...] + g2_ref[0] * acc_ref[...]
        ms = jnp.mean(y * y, axis=-1, keepdims=True)
        o_ref[...] = y * lax.rsqrt(ms + EPS) * fnw_ref[...]


def _moe(xn2, comb_t, counts, w1, w3, w2, sw1, sw3, sw2, x1, g2, fnw, tm):
    t, d = xn2.shape
    ne, _, f = w1.shape
    l = comb_t.shape[2]
    per = EXP_PER_STEP
    gpb = l // tm
    sub, cap = MOE_SUB, MOE_CAP
    tri = jnp.asarray(np.triu(np.ones((sub, sub), np.float32), 1), BF16)
    load = jnp.max(counts.reshape(comb_t.shape[0] * gpb, tm // sub, ne), axis=1).astype(jnp.int32)
    order = jnp.argsort(load, axis=-1).astype(jnp.int32)
    step_load = jnp.max(jnp.take_along_axis(load, order, axis=-1).reshape(-1, ne // per, per), axis=-1)
    rounds = (step_load + (cap - 1)) // cap
    tok = lambda w: pl.BlockSpec((tm, w), lambda i, g, o, r: (i, 0))
    const = lambda shape: pl.BlockSpec(shape, lambda i, g, o, r: (0,) * len(shape))

    def expert(shape):
        return [pl.BlockSpec((1,) + shape, lambda i, g, o, r, k=k: (o[i, g * per + k], 0, 0)) for k in range(per)]

    vm = (2 * tm * d * 2 + 2 * ne * tm * 4 + 2 * 3 * per * d * f * 2 + 2 * 3 * d * f * 2
          + 2 * tm * d * 4 * 2 + tm * d * 4 + 16 * per * cap * (tm // sub) * d)
    grid_spec = pltpu.PrefetchScalarGridSpec(
        num_scalar_prefetch=2,
        grid=(t // tm, ne // per),
        in_specs=([tok(d),
                   pl.BlockSpec((1, ne, tm), lambda i, g, o, r: (i // gpb, 0, i % gpb)),
                   const(tri.shape)]
                  + expert((d, f)) + expert((d, f)) + expert((f, d))
                  + [const(sw1.shape), const(sw3.shape), const(sw2.shape),
                     tok(d),
                     pl.BlockSpec((1, 1, d), lambda i, g, o, r: (i // gpb, 0, 0)),
                     const((1, d))]),
        out_specs=tok(d),
        scratch_shapes=[pltpu.VMEM((tm, d), F32), pltpu.VMEM((tm // sub, ne, sub), F32)])
    return pl.pallas_call(
        functools.partial(_moe_body, per=per, sub=sub, cap=cap),
        grid_spec=grid_spec,
        out_shape=jax.ShapeDtypeStruct((t, d), F32),
        compiler_params=_params(("arbitrary", "arbitrary"), vm),
        name="moe",
    )(order, rounds, xn2, comb_t, tri, *([w1] * per), *([w3] * per), *([w2] * per), sw1, sw3, sw2, x1, g2, fnw)


def _pad_to(a, rows, cols):
    return jnp.pad(a, ((0, rows - a.shape[0]), (0, cols - a.shape[1])))


def kernel(x, c, ctx, c_ctx, ada_w, ada_b, norm1_w, norm2_w, w_in, hy_conv_w, hy_conv_b, hy_w1, hy_b1, hy_freq, hy_w2, hy_b2, hy_w3, hy_bias, gla_a_w2, gla_a_b, gla_norm_w, proj_hy, proj_gla, w_out, router_w, router_bias, exp_w1, exp_w3, exp_w2, sh_w1, sh_w3, sh_w2, final_norm_w):
    b, l, d = x.shape
    assert ada_w.shape[0] == 1, "single-layer block"
    assert l // GRID_W * GRID_W == l and FFT_N2 == GRID_W
    heads = GLA_HEADS
    qk_w = d // 2
    dk = qk_w // heads
    v_w = d
    dv = v_w // heads
    a_w = 2 * GLA_RANK
    hy_w = d
    hy_cols = (HY_ORDER + 1) * hy_w

    rows = -(-(b + 1) // 8) * 8
    cc = jnp.zeros((rows, d), F32).at[:b].set(c).at[b].set(c_ctx)
    mods = _mods(cc, ada_w[0], ada_b[0][None])
    sh1, sc1, g1, sh2, sc2, g2 = [m[:b, None, :] for m in jnp.split(mods, 6, axis=-1)]
    csh1, csc1 = [jnp.broadcast_to(m[b][None, None, :], (b, 1, d)) for m in jnp.split(mods, 6, axis=-1)[:2]]

    w = w_in[0]
    o_a = qk_w + v_w
    o_q = o_a + a_w
    o_g = o_q + qk_w
    o_hy = o_g + v_w
    o_gate = o_hy + hy_cols
    w_k, w_v = w[:, :qk_w], w[:, qk_w:o_a]
    w_a = jnp.pad(w[:, o_a:o_q], ((0, 0), (0, LANES - a_w)))
    wp = jnp.concatenate([w[:, o_gate:], w[:, o_g:o_hy], w_v, w_k, w[:, o_q:o_g], w[:, o_hy:o_gate], w_a],
                         axis=1).astype(BF16)
    p_gate = 0
    p_g = 2 * d
    p_v = p_g + v_w
    p_k = p_v + v_w
    p_q = p_k + qk_w
    p_hy = p_q + qk_w
    p_a = p_hy + hy_cols
    n_all = p_a + LANES
    w_ctx = jnp.concatenate([w_k, w_v, w_a], axis=1).astype(BF16)
    cols_ctx = (0, qk_w, qk_w + v_w, None)
    cols = (p_k, p_v, p_a, p_q)

    nw1 = norm1_w[0][None]
    u_ctx = _inproj(ctx, nw1, csh1, csc1, w_ctx, ctx.shape[1], w_ctx.shape[1])
    u = _inproj(x, nw1, sh1, sc1, wp, 1024, n_all // 5)

    wa = gla_a_w2[0].reshape(2, GLA_RANK, heads, dk).transpose(0, 2, 1, 3)
    waf = jnp.pad(wa[0], ((0, 0), (0, LANES - GLA_RANK), (0, 0))).astype(BF16)
    wab = jnp.pad(wa[1], ((0, 0), (GLA_RANK, LANES - 2 * GLA_RANK), (0, 0))).astype(BF16)
    ba = gla_a_b[0].reshape(2, heads, 1, dk)
    zeros_state = jnp.zeros((b, heads, dv, dk), F32)
    _, _, s_f, s_b = _gla(u_ctx, cols_ctx, waf, wab, ba[0], ba[1], zeros_state, zeros_state, dk, dv, False,
                          ctx.shape[1])
    o_f, o_b, _, _ = _gla(u, cols, waf, wab, ba[0], ba[1], s_f, s_b, dk, dv, True, GLA_BLOCK)

    n = 2 * l
    max_decay = math.log(HY_TARGET) / HY_FAST_DECAY
    min_decay = math.log(HY_TARGET) / HY_SLOW_DECAY
    deltas = jnp.asarray(np.abs(np.linspace(min_decay, max_decay, hy_w, dtype=np.float32))[None])
    ffn = hy_w1.shape[2]
    kern = _hyfilt(l, hy_w,
                   _pad_to(hy_w1[0], LANES, LANES), _pad_to(hy_b1[0][None], 1, LANES),
                   _pad_to(hy_freq[0, 0][None], 1, LANES),
                   _pad_to(hy_w2[0], LANES, LANES), _pad_to(hy_b2[0][None], 1, LANES),
                   _pad_to(hy_freq[0, 1][None], 1, LANES),
                   jnp.pad(hy_w3[0], ((0, LANES - ffn), (0, 0))), deltas)
    tabs = _fft_tables(n, FFT_N2)
    spec = _hyspec(kern, tabs[1], tabs[2], FFT_N2)
    cw, cb = hy_conv_w[0], hy_conv_b[0][None]
    z1 = _hyconv(u, p_hy, u, p_hy + hy_w, cw, cb, 0, hy_w, spec, 0, hy_bias[0, 0][None], tabs, True)
    y_hy = _hyconv(z1, 0, u, p_hy + 2 * hy_w, cw, cb, 0, 2 * hy_w, spec, 1, hy_bias[0, 1][None], tabs, False)

    x1, xn2, logits_t = _mix(x, o_f, o_b, u, p_g, p_gate, y_hy, gla_norm_w[0][None],
                             proj_hy[0].astype(BF16), proj_gla[0].astype(BF16), w_out[0].astype(BF16),
                             g1, norm2_w[0][None], sh2, sc2, router_w[0].T, 512)
    comb, counts = _select(logits_t, router_bias[0].reshape(N_GROUPS, N_EXPERTS // N_GROUPS, 1), 512, MOE_SUB)
    out = _moe(xn2.reshape(b * l, d), comb, counts, exp_w1[0].astype(BF16), exp_w3[0].astype(BF16),
               exp_w2[0].astype(BF16), sh_w1[0].astype(BF16), sh_w3[0].astype(BF16), sh_w2[0].astype(BF16),
               x1.reshape(b * l, d), g2, final_norm_w[None], 1024)
    return out.reshape(b, l, d)
```

```python
import functools
import math

import jax
import jax.numpy as jnp
import numpy as np
from jax import lax
from jax.experimental import pallas as pl
from jax.experimental.pallas import tpu as pltpu

F32 = jnp.float32
BF16 = jnp.bfloat16
HIGHEST = lax.Precision.HIGHEST

GRID_W = 64
EPS = 1e-6
HY_ORDER = 2
HY_BANDS = 16
HY_FAST_DECAY = 0.3
HY_SLOW_DECAY = 1.5
HY_TARGET = 1e-2
GLA_HEADS = 4
GLA_RANK = 16
GLA_TAU = 16.0
N_EXPERTS = 64
N_GROUPS = 8
TOPK_GROUPS = 4
TOP_K = 8
ROUTED_SCALE = 2.5

LANES = 128
V7X_VMEM_BYTES = 64 * 1024 * 1024
VMEM_CAP_BYTES = 56 * 1024 * 1024

GLA_CHUNK = 256
GLA_BLOCK = 512
GLA_VPU_MIN_HALF = 4
FFT_N2 = 64
STRIDE_PAD = 8
STAGE_UNROLL = 64
ROW_UNROLL = 2
EXP_PER_STEP = 4
MOE_SUB = 256
MOE_CAP = 64


def _params(sem, vmem_bytes):
    limit = int(min(VMEM_CAP_BYTES, max(16 * 1024 * 1024, vmem_bytes * 5 // 4 + (2 << 20))))
    return pltpu.CompilerParams(dimension_semantics=sem, vmem_limit_bytes=limit)


def _nt(a, b, **kw):
    return lax.dot_general(a, b, (((1,), (1,)), ((), ())), preferred_element_type=F32, **kw)


def _tn(a, b):
    return lax.dot_general(a, b, (((0,), (0,)), ((), ())), preferred_element_type=F32)


def _dot(a, b, **kw):
    return jnp.dot(a, b, preferred_element_type=F32, **kw)


def _silu(x):
    return x * jax.nn.sigmoid(x)


def _mods_body(c_ref, w_ref, b_ref, o_ref):
    o_ref[...] = _dot(_silu(c_ref[...]), w_ref[...], precision=HIGHEST) + b_ref[...]


def _mods(cc, w, b):
    rows, d = cc.shape
    n = w.shape[1]
    tn = n // 4
    return pl.pallas_call(
        _mods_body,
        grid=(n // tn,),
        in_specs=[pl.BlockSpec((rows, d), lambda j: (0, 0)),
                  pl.BlockSpec((d, tn), lambda j: (0, j)),
                  pl.BlockSpec((1, tn), lambda j: (0, j))],
        out_specs=pl.BlockSpec((rows, tn), lambda j: (0, j)),
        out_shape=jax.ShapeDtypeStruct((rows, n), F32),
        compiler_params=_params(("arbitrary",), 2 * d * tn * 4),
        name="mods",
    )(cc, w, b)


def _norm_mod(x, w, shift, scale):
    ms = jnp.mean(x * x, axis=-1, keepdims=True)
    return (x * lax.rsqrt(ms + EPS) * w) * (1.0 + scale) + shift


def _inproj_body(x_ref, nw_ref, sh_ref, sc_ref, w_ref, o_ref, xn_ref):
    @pl.when(pl.program_id(2) == 0)
    def _():
        xn_ref[...] = _norm_mod(x_ref[0], nw_ref[...], sh_ref[0], sc_ref[0]).astype(BF16)

    o_ref[0] = _dot(xn_ref[...], w_ref[...]).astype(BF16)


def _inproj(x, nw, shift, scale, w, tm, tn):
    b, l, d = x.shape
    n = w.shape[1]
    vm = 2 * tm * d * 4 + 2 * d * tn * 2 + 2 * tm * tn * 2 + tm * d * 2
    return pl.pallas_call(
        _inproj_body,
        grid=(b, l // tm, n // tn),
        in_specs=[pl.BlockSpec((1, tm, d), lambda bi, i, j: (bi, i, 0)),
                  pl.BlockSpec((1, d), lambda bi, i, j: (0, 0)),
                  pl.BlockSpec((1, 1, d), lambda bi, i, j: (bi, 0, 0)),
                  pl.BlockSpec((1, 1, d), lambda bi, i, j: (bi, 0, 0)),
                  pl.BlockSpec((d, tn), lambda bi, i, j: (0, j))],
        out_specs=pl.BlockSpec((1, tm, tn), lambda bi, i, j: (bi, i, j)),
        out_shape=jax.ShapeDtypeStruct((b, l, n), BF16),
        scratch_shapes=[pltpu.VMEM((tm, d), BF16)],
        compiler_params=_params(("arbitrary", "arbitrary", "arbitrary"), vm),
        name="inproj",
    )(x, nw, shift, scale, w)


def _gla_tables(c, inclusive, flip):
    idx = np.arange(c)
    i = idx[:, None]
    x = idx[None, :]
    blocks = [x <= i]
    masks = []
    h = c // 2
    while h >= 1:
        mid = (idx // (2 * h)) * (2 * h) + h
        mi = mid[:, None]
        hi = i if inclusive else i - 1
        if h < GLA_VPU_MIN_HALF:
            blocks.append(((i >= mi) & (x >= mi) & (x <= hi)) | ((i < mi) & (x > i) & (x <= mi - 1)))
        same = (idx[:, None] // (2 * h)) == (idx[None, :] // (2 * h))
        masks.append(same & (idx[:, None] >= mi) & (idx[None, :] < mid[None, :]))
        h //= 2
    masks.append(np.eye(c, dtype=bool))
    if flip:
        blocks = [b[::-1, ::-1] for b in blocks]
        masks = [m[::-1, ::-1] for m in masks]
    lall = np.concatenate(blocks + [np.ones((8, c), bool)], axis=0)
    return lall.astype(np.float32), np.stack(masks).astype(np.float32)


def _gla_chunk(q, k, v, a, wa, ba, lall, masks_ref, st_ref, inclusive, flip, q_scale):
    c, dk = k.shape
    n_levels = int(math.log2(c))
    xg = _dot(a, wa) + ba
    g = (jnp.minimum(xg, 0.0) - jnp.log(1.0 + jnp.exp(-jnp.abs(xg)))) * (1.0 / GLA_TAU)
    g_hi = g.astype(BF16)
    g_lo = (g - g_hi.astype(F32)).astype(BF16)
    e2 = _dot(lall, jnp.concatenate([g_hi, g_lo], axis=1))
    e = e2[:, :dk] + e2[:, dk:]
    run = e[0:c]
    tot = e[e.shape[0] - 8:e.shape[0] - 7]
    upto = run if inclusive else run - g

    def decay(t):
        return jnp.exp(jnp.minimum(t, 0.0))

    row = lax.broadcasted_iota(jnp.int32, (c, dk), 0)
    n_vpu = n_levels - int(math.log2(GLA_VPU_MIN_HALF))

    def level_decay(lv):
        h = c >> (lv + 1)
        if lv >= n_vpu:
            return decay(e[(1 + lv - n_vpu) * c:(2 + lv - n_vpu) * c])
        p0 = h if flip else h - 1
        piv = jnp.concatenate([jnp.broadcast_to(run[m + p0:m + p0 + 1, :], (2 * h, dk))
                               for m in range(0, c, 2 * h)], axis=0)
        if inclusive:
            return jnp.exp(-jnp.abs(run - piv))
        key_side = ((row & (2 * h - 1)) >= h) if flip else ((row & (2 * h - 1)) < h)
        return decay(jnp.where(key_side, piv - run, upto - piv))

    kf = k.astype(F32)
    st = st_ref[...]
    k1 = (kf * decay(tot - run)).astype(BF16)
    st_ref[...] = st * decay(tot) + _tn(v, k1)
    if q is None:
        return None
    qf = q.astype(F32) * q_scale
    o = _nt((qf * decay(upto)).astype(BF16), st.astype(BF16))
    attn = jnp.zeros((c, c), BF16)
    for lv in range(n_levels):
        ex_l = level_decay(lv)
        attn = attn + _nt((qf * ex_l).astype(BF16), (kf * ex_l).astype(BF16)).astype(BF16) * masks_ref[lv]
    if inclusive:
        attn = attn + _nt(qf.astype(BF16), k).astype(BF16) * masks_ref[n_levels]
    return o + _dot(attn, v)


def _gla_body(*refs, with_q, n_sub, chunk, q_scale):
    if with_q:
        (kf_ref, vf_ref, af_ref, qf_ref, kb_ref, vb_ref, ab_ref, qb_ref, waf_ref, wab_ref, baf_ref, bab_ref,
         lf_ref, lb_ref, mf_ref, mb_ref, s0f_ref, s0b_ref, of_ref, ob_ref, sf_ref, sb_ref, stf_ref, stb_ref) = refs
    else:
        (kf_ref, vf_ref, af_ref, kb_ref, vb_ref, ab_ref, waf_ref, wab_ref, baf_ref, bab_ref,
         lf_ref, lb_ref, mf_ref, mb_ref, s0f_ref, s0b_ref, sf_ref, sb_ref, stf_ref, stb_ref) = refs
        qf_ref = qb_ref = of_ref = ob_ref = None

    @pl.when(pl.program_id(2) == 0)
    def _():
        stf_ref[...] = s0f_ref[0, 0]
        stb_ref[...] = s0b_ref[0, 0]

    for s in range(n_sub):
        sl = slice(s * chunk, (s + 1) * chunk)
        o = _gla_chunk(None if qf_ref is None else qf_ref[0, sl, :], kf_ref[0, sl, :], vf_ref[0, sl, :],
                       af_ref[0, sl, :], waf_ref[0], baf_ref[0], lf_ref[...], mf_ref, stf_ref, True, False, q_scale)
        if with_q:
            of_ref[0, sl, :] = o.astype(BF16)
    for s in reversed(range(n_sub)):
        sl = slice(s * chunk, (s + 1) * chunk)
        o = _gla_chunk(None if qb_ref is None else qb_ref[0, sl, :], kb_ref[0, sl, :], vb_ref[0, sl, :],
                       ab_ref[0, sl, :], wab_ref[0], bab_ref[0], lb_ref[...], mb_ref, stb_ref, False, True, q_scale)
        if with_q:
            ob_ref[0, sl, :] = o.astype(BF16)
    sf_ref[0, 0] = stf_ref[...]
    sb_ref[0, 0] = stb_ref[...]


def _gla(u, cols, waf, wab, baf, bab, s0f, s0b, dk, dv, with_q, tb):
    b, l, _ = u.shape
    h = GLA_HEADS
    nb = l // tb
    n_sub = tb // GLA_CHUNK
    lf, mf = _gla_tables(GLA_CHUNK, True, False)
    lb, mb = _gla_tables(GLA_CHUNK, False, True)
    lf, lb = jnp.asarray(lf, BF16), jnp.asarray(lb, BF16)
    mf, mb = jnp.asarray(mf, BF16), jnp.asarray(mb, BF16)
    kc, vc, ac, qc = cols

    def seq_specs(rev):
        def blk(i):
            return (nb - 1 - i) if rev else i
        specs = [pl.BlockSpec((1, tb, dk), lambda bi, hi, i: (bi, blk(i), kc // dk + hi)),
                 pl.BlockSpec((1, tb, dv), lambda bi, hi, i: (bi, blk(i), vc // dv + hi)),
                 pl.BlockSpec((1, tb, LANES), lambda bi, hi, i: (bi, blk(i), ac // LANES))]
        if with_q:
            specs.append(pl.BlockSpec((1, tb, dk), lambda bi, hi, i: (bi, blk(i), qc // dk + hi)))
        return specs

    def const_spec(shape):
        nd = len(shape)
        return pl.BlockSpec(shape, lambda bi, hi, i: (0,) * nd)

    head_w = pl.BlockSpec((1, LANES, dk), lambda bi, hi, i: (hi, 0, 0))
    head_b = pl.BlockSpec((1, 1, dk), lambda bi, hi, i: (hi, 0, 0))
    st_spec = pl.BlockSpec((1, 1, dv, dk), lambda bi, hi, i: (bi, hi, 0, 0))
    in_specs = (seq_specs(False) + seq_specs(True) + [head_w, head_w, head_b, head_b,
                const_spec(lf.shape), const_spec(lb.shape), const_spec(mf.shape), const_spec(mb.shape),
                st_spec, st_spec])
    st_shape = jax.ShapeDtypeStruct((b, h, dv, dk), F32)
    if with_q:
        o_shape = jax.ShapeDtypeStruct((b, l, h * dv), BF16)
        out_shape = (o_shape, o_shape, st_shape, st_shape)
        out_specs = (pl.BlockSpec((1, tb, dv), lambda bi, hi, i: (bi, i, hi)),
                     pl.BlockSpec((1, tb, dv), lambda bi, hi, i: (bi, nb - 1 - i, hi)),
                     st_spec, st_spec)
        args = (u,) * 8
    else:
        out_shape = (st_shape, st_shape)
        out_specs = (st_spec, st_spec)
        args = (u,) * 6
    vm = 4 * tb * (2 * dk + dv + LANES) * 2 * 2 + 8 * dv * dk * 4 + 4 * tb * dv * 2 + (4 << 20)
    outs = pl.pallas_call(
        functools.partial(_gla_body, with_q=with_q, n_sub=n_sub, chunk=GLA_CHUNK, q_scale=dk ** -0.5),
        grid=(b, h, nb),
        in_specs=in_specs,
        out_specs=out_specs,
        out_shape=out_shape,
        scratch_shapes=[pltpu.VMEM((dv, dk), F32), pltpu.VMEM((dv, dk), F32)],
        compiler_params=_params(("arbitrary", "arbitrary", "arbitrary"), vm),
        name="gla" if with_q else "gla_ctx",
    )(*args, waf, wab, baf, bab, lf, lb, mf, mb, s0f, s0b)
    if with_q:
        return outs
    return None, None, outs[0], outs[1]


def _hy_tables(l):
    t = np.linspace(0.0, 1.0, l, dtype=np.float32).astype(np.float64)[:, None]
    w = 2.0 * math.pi * np.arange(l, dtype=np.float64)[:, None] / l
    f = np.linspace(1e-4, HY_BANDS - 1, HY_BANDS, dtype=np.float32).astype(np.float64)[None, :]
    z = np.concatenate([t, np.cos(f * w), -np.sin(f * w)], axis=-1)
    rev = (l - np.arange(l)) % l

    def pad(a):
        out = np.zeros((l, LANES), np.float32)
        out[:, :a.shape[1]] = a
        return out

    tt = np.broadcast_to(t, (l, LANES)).astype(np.float32)
    return pad(z), pad(z[rev]), tt, np.ascontiguousarray(tt[rev])


def _hyfilt_body(z_ref, t1_ref, t2_ref, w1_ref, b1_ref, f1_ref, w2_ref, b2_ref, f2_ref,
                 w3f_ref, w3b_ref, dl_ref, o_ref, h_ref):
    l = z_ref.shape[0]

    def split(a):
        hi = a.astype(BF16)
        return hi, (a - hi.astype(F32)).astype(BF16)

    @pl.when((pl.program_id(0) == 0) & (pl.program_id(1) == 0))
    def _():
        h = jnp.sin(f1_ref[...] * (_dot(z_ref[...], w1_ref[...], precision=HIGHEST) + b1_ref[...]))
        h = jnp.sin(f2_ref[...] * (_dot(h, w2_ref[...], precision=HIGHEST) + b2_ref[...]))
        h_ref[0], h_ref[1] = split(h)

    def out_proj(w_ref):
        w_hi, w_lo = split(w_ref[...])
        return _dot(h_ref[0], w_hi) + (_dot(h_ref[0], w_lo) + _dot(h_ref[1], w_hi))

    dl = dl_ref[...]
    hf = out_proj(w3f_ref) * jnp.exp(-t1_ref[...] * dl)
    hb = out_proj(w3b_ref) * jnp.exp(-t2_ref[...] * dl)
    row = lax.broadcasted_iota(jnp.int32, hb.shape, 0)
    hb = jnp.where(row == 0, 0.0, hb)
    ss = jnp.sum(hf * hf, axis=0, keepdims=True) + jnp.sum(hb * hb, axis=0, keepdims=True)
    scale = lax.rsqrt(ss)
    o_ref[0, 0:l, :] = hf * scale
    o_ref[0, l:2 * l, :] = hb * scale


def _hyfilt(l, c, w1, b1, f1, w2, b2, f2, w3, deltas):
    z1, z2, t1, t2 = (jnp.asarray(a) for a in _hy_tables(l))
    hl = LANES // 2
    z = jnp.concatenate([z1[:, :hl], z2[:, :hl]], axis=1)
    zero = jnp.zeros((hl, hl), F32)

    def diag2(m):
        return jnp.concatenate([jnp.concatenate([m[:hl, :hl], zero], axis=1),
                                jnp.concatenate([zero, m[:hl, :hl]], axis=1)], axis=0)

    def twice(r):
        return jnp.concatenate([r[:, :hl], r[:, :hl]], axis=1)

    w1, w2 = diag2(w1), diag2(w2)
    b1, f1, b2, f2 = twice(b1), twice(f1), twice(b2), twice(f2)
    w3_second = jnp.concatenate([w3[hl:], w3[:hl]], axis=0)
    ncb = c // LANES
    tab = pl.BlockSpec((l, LANES), lambda o, j: (0, 0))
    sq = pl.BlockSpec((LANES, LANES), lambda o, j: (0, 0))
    row = pl.BlockSpec((1, LANES), lambda o, j: (0, 0))
    return pl.pallas_call(
        _hyfilt_body,
        grid=(HY_ORDER, ncb),
        in_specs=[tab, tab, tab, sq, row, row, sq, row, row,
                  pl.BlockSpec((LANES, LANES), lambda o, j: (0, 2 * o * ncb + j)),
                  pl.BlockSpec((LANES, LANES), lambda o, j: (0, (2 * o + 1) * ncb + j)),
                  pl.BlockSpec((1, LANES), lambda o, j: (0, j))],
        out_specs=pl.BlockSpec((1, 2 * l, LANES), lambda o, j: (o, 0, j)),
        out_shape=jax.ShapeDtypeStruct((HY_ORDER, 2 * l, c), F32),
        scratch_shapes=[pltpu.VMEM((2, l, LANES), BF16)],
        compiler_params=_params(("arbitrary", "arbitrary"), 24 * l * LANES * 4),
        name="hyfilt",
    )(z, t1, t2, w1, b1, f1, w2, b2, f2, w3, w3_second, deltas)


def _fft_tables(n, n2):
    n1 = n // n2
    h = n1 // 2
    k1 = np.arange(n1)[:, None]
    a = 2.0 * math.pi * k1 * np.arange(h)[None, :] / n1
    c, s = np.cos(a), np.sin(a)
    f1c = np.block([[c, s], [-s, c]])
    a = 2.0 * math.pi * k1 * np.arange(n1)[None, :] / n1
    f1r = np.concatenate([np.cos(a), -np.sin(a)], axis=0)
    kk = np.arange(n1)[:, None, None] + n1 * np.arange(n2)[None, :, None]
    a = 2.0 * math.pi * kk * np.arange(n2)[None, None, :] / n
    c, s = np.cos(a), np.sin(a)
    gf = np.concatenate([np.concatenate([c, s], axis=2), np.concatenate([-s, c], axis=2)], axis=1)
    a = 2.0 * math.pi * np.arange(h)[:, None] * np.arange(n1)[None, :] / n1
    c, s = np.cos(a), np.sin(a)
    if1 = np.block([[c, -s], [s, c]])
    return tuple(jnp.asarray(m, BF16) for m in (f1c, f1r, gf, if1))


def _rows8(start, size):
    return pl.ds(pl.multiple_of(start, 8), size)


def _hyspec_body(k_ref, f1_ref, gf_ref, o_ref, as_ref, *, n, n2, ap):
    n1 = n // n2

    def stage1(j, carry):
        r = k_ref[0, pl.ds(j, n1, stride=n2), :].astype(BF16)
        as_ref[_rows8(j * ap, 2 * n1), :] = _dot(f1_ref[...], r)
        return carry

    lax.fori_loop(0, n2, stage1, 0, unroll=STAGE_UNROLL)

    def stage2(k1, carry):
        r = jnp.concatenate([as_ref[pl.ds(k1, n2, stride=ap), :],
                             as_ref[pl.ds(n1 + k1, n2, stride=ap), :]], axis=0).astype(BF16)
        o_ref[0, k1] = (_dot(gf_ref[k1], r) * (1.0 / n)).astype(BF16)
        return carry

    lax.fori_loop(0, n1, stage2, 0, unroll=STAGE_UNROLL)


def _hyspec(kern, f1r, gf, n2):
    order, n, c = kern.shape
    n1 = n // n2
    ap = 2 * n1 + STRIDE_PAD
    return pl.pallas_call(
        functools.partial(_hyspec_body, n=n, n2=n2, ap=ap),
        grid=(order, c // LANES),
        in_specs=[pl.BlockSpec((1, n, LANES), lambda o, j: (o, 0, j)),
                  pl.BlockSpec(f1r.shape, lambda o, j: (0, 0)),
                  pl.BlockSpec(gf.shape, lambda o, j: (0, 0, 0))],
        out_specs=pl.BlockSpec((1, n1, 2 * n2, LANES), lambda o, j: (o, 0, 0, j)),
        out_shape=jax.ShapeDtypeStruct((order, n1, 2 * n2, c), BF16),
        scratch_shapes=[pltpu.VMEM((n2 * ap, LANES), F32)],
        compiler_params=_params(("arbitrary", "arbitrary"),
                                2 * n * LANES * 4 + 2 * gf.size * 2 + n2 * ap * LANES * 4 + 2 * n * LANES * 2),
        name="hyspec",
    )(kern, f1r, gf)


def _short_conv(u, w_ref, b_ref):
    r = u.shape[0]
    row = lax.broadcasted_iota(jnp.int32, u.shape, 0)
    up = jnp.where(row == 0, 0.0, pltpu.roll(u, 1, axis=0))
    dn = jnp.where(row == r - 1, 0.0, pltpu.roll(u, r - 1, axis=0))
    return up * w_ref[0:1, :] + u * w_ref[1:2, :] + dn * w_ref[2:3, :] + b_ref[...]


def _hyconv_body(z_ref, g_ref, zw_ref, zb_ref, gw_ref, gb_ref, sp_ref, hb_ref, f1_ref, gf_ref, if1_ref,
                 o_ref, x_ref, as_ref, bs_ref, y_ref, *, conv_z, n2, xp, ap):
    n1h = z_ref.shape[1] // n2
    n1 = 2 * n1h
    half = n1h * xp

    def fill(i, carry):
        for p in range(2):
            u = z_ref[p, pl.ds(pl.multiple_of(i * n2, n2), n2), :].astype(F32)
            x_ref[_rows8(p * half + i * xp, n2), :] = _short_conv(u, zw_ref, zb_ref) if conv_z else u
        return carry

    lax.fori_loop(0, n1h, fill, 0, unroll=ROW_UNROLL)

    def stage1(j, carry):
        r = jnp.concatenate([x_ref[pl.ds(j, n1h, stride=xp), :],
                             x_ref[pl.ds(half + j, n1h, stride=xp), :]], axis=0).astype(BF16)
        as_ref[_rows8(j * ap, 2 * n1), :] = _dot(f1_ref[...], r)
        return carry

    lax.fori_loop(0, n2, stage1, 0, unroll=STAGE_UNROLL)

    def stage2(k1, carry):
        r = jnp.concatenate([as_ref[pl.ds(k1, n2, stride=ap), :],
                             as_ref[pl.ds(n1 + k1, n2, stride=ap), :]], axis=0).astype(BF16)
        gk = gf_ref[k1]
        xk = _dot(gk, r)
        xr, xi = xk[0:n2], xk[n2:2 * n2]
        sp = sp_ref[0, k1].astype(F32)
        sr, si = sp[0:n2], sp[n2:2 * n2]
        yk = jnp.concatenate([xr * sr - xi * si, xr * si + xi * sr], axis=0).astype(BF16)
        bk = _tn(gk, yk)
        bs_ref[pl.ds(k1, n2, stride=ap), :] = bk[0:n2]
        bs_ref[pl.ds(n1 + k1, n2, stride=ap), :] = bk[n2:2 * n2]
        return carry

    lax.fori_loop(0, n1, stage2, 0, unroll=2 * STAGE_UNROLL)

    def stage3(j, carry):
        yn = _dot(if1_ref[...], bs_ref[_rows8(j * ap, 2 * n1), :].astype(BF16))
        y_ref[pl.ds(j, n1h, stride=xp), :] = yn[0:n1h]
        y_ref[pl.ds(half + j, n1h, stride=xp), :] = yn[n1h:n1]
        return carry

    lax.fori_loop(0, n2, stage3, 0, unroll=STAGE_UNROLL)

    def finish(i, carry):
        rows = pl.ds(pl.multiple_of(i * n2, n2), n2)
        for p in range(2):
            gate = _short_conv(g_ref[p, rows, :].astype(F32), gw_ref, gb_ref)
            z = x_ref[_rows8(p * half + i * xp, n2), :]
            y = y_ref[_rows8(p * half + i * xp, n2), :]
            o_ref[p, rows, :] = (gate * (y + z * hb_ref[...])).astype(BF16)
        return carry

    lax.fori_loop(0, n1h, finish, 0, unroll=ROW_UNROLL)


def _hyconv(z, z_col, g, g_col, conv_w, conv_b, zw_col, gw_col, spec, order, hy_bias, tabs, conv_z):
    b, l, _ = z.shape
    f1c, _, gf, if1 = tabs
    n1, n2x2 = gf.shape[0], gf.shape[1]
    n2 = n2x2 // 2
    n1h = n1 // 2
    c = spec.shape[-1]
    ncb = c // LANES
    xp = n2 + STRIDE_PAD
    ap = 2 * n1 + STRIDE_PAD
    vm = (2 * 2 * 2 * l * LANES * 2 + 2 * 2 * l * LANES * 2 + 2 * n1 * n2x2 * LANES * 2
          + 2 * gf.size * 2 + 2 * 2 * n1h * xp * LANES * 4 + 2 * n2 * ap * LANES * 4)
    return pl.pallas_call(
        functools.partial(_hyconv_body, conv_z=conv_z, n2=n2, xp=xp, ap=ap),
        grid=(ncb, b // 2),
        in_specs=[pl.BlockSpec((2, l, LANES), lambda j, p: (p, 0, z_col // LANES + j)),
                  pl.BlockSpec((2, l, LANES), lambda j, p: (p, 0, g_col // LANES + j)),
                  pl.BlockSpec((3, LANES), lambda j, p: (0, zw_col // LANES + j)),
                  pl.BlockSpec((1, LANES), lambda j, p: (0, zw_col // LANES + j)),
                  pl.BlockSpec((3, LANES), lambda j, p: (0, gw_col // LANES + j)),
                  pl.BlockSpec((1, LANES), lambda j, p: (0, gw_col // LANES + j)),
                  pl.BlockSpec((1, n1, n2x2, LANES), lambda j, p: (order, 0, 0, j)),
                  pl.BlockSpec((1, LANES), lambda j, p: (0, j)),
                  pl.BlockSpec(f1c.shape, lambda j, p: (0, 0)),
                  pl.BlockSpec(gf.shape, lambda j, p: (0, 0, 0)),
                  pl.BlockSpec(if1.shape, lambda j, p: (0, 0))],
        out_specs=pl.BlockSpec((2, l, LANES), lambda j, p: (p, 0, j)),
        out_shape=jax.ShapeDtypeStruct((b, l, c), BF16),
        scratch_shapes=[pltpu.VMEM((2 * n1h * xp, LANES), F32),
                        pltpu.VMEM((n2 * ap, LANES), F32),
                        pltpu.VMEM((n2 * ap, LANES), F32),
                        pltpu.VMEM((2 * n1h * xp, LANES), F32)],
        compiler_params=_params(("arbitrary", "arbitrary"), vm),
        name="hyconv%d" % order,
    )(z, g, conv_w, conv_b, conv_w, conv_b, spec, hy_bias, f1c, gf, if1)


def _mix_body(x_ref, of_ref, ob_ref, ug_ref, ugate_ref, yhy_ref, gnw_ref, phy_ref, pgla_ref, wout_ref,
              g1_ref, n2w_ref, sh2_ref, sc2_ref, rwt_ref, x1_ref, xn2_ref, lg_ref, *, heads):
    d = x_ref.shape[2]
    o = of_ref[0].astype(F32) + ob_ref[0].astype(F32)
    dv = o.shape[1] // heads
    parts = []
    for h in range(heads):
        seg = o[:, h * dv:(h + 1) * dv]
        parts.append(seg * lax.rsqrt(jnp.mean(seg * seg, axis=-1, keepdims=True) + EPS))
    y_gla = jnp.concatenate(parts, axis=1) * gnw_ref[...] * _silu(ug_ref[0].astype(F32))
    gates = jax.nn.sigmoid(ugate_ref[0].astype(F32))
    merged = (gates[:, :d] * _dot(yhy_ref[0], phy_ref[...])
              + gates[:, d:] * _dot(y_gla.astype(BF16), pgla_ref[...]))
    x1 = x_ref[0] + g1_ref[0] * _dot(merged.astype(BF16), wout_ref[...])
    x1_ref[0] = x1
    xn2 = _norm_mod(x1, n2w_ref[...], sh2_ref[0], sc2_ref[0])
    xn2_ref[0] = xn2.astype(BF16)
    lg_ref[0] = _nt(rwt_ref[...], xn2, precision=HIGHEST)


def _mix(x, o_f, o_b, u, g_col, gate_col, y_hy, gnw, phy, pgla, wout, g1, n2w, sh2, sc2, rwt, tm):
    b, l, d = x.shape
    vw = o_f.shape[2]
    ne = rwt.shape[0]
    tok = lambda w: pl.BlockSpec((1, tm, w), lambda bi, i: (bi, i, 0))
    per_b = pl.BlockSpec((1, 1, d), lambda bi, i: (bi, 0, 0))
    const = lambda shape: pl.BlockSpec(shape, lambda bi, i: (0, 0))
    vm = (2 * tm * d * 4 * 2 + 2 * tm * (3 * vw + 2 * d + 2 * d) * 2 + 2 * 3 * d * d * 2 + 12 * tm * d * 4)
    return pl.pallas_call(
        functools.partial(_mix_body, heads=GLA_HEADS),
        grid=(b, l // tm),
        in_specs=[tok(d), tok(vw), tok(vw),
                  pl.BlockSpec((1, tm, vw), lambda bi, i: (bi, i, g_col // vw)),
                  pl.BlockSpec((1, tm, 2 * d), lambda bi, i: (bi, i, gate_col // (2 * d))),
                  tok(d), const((1, vw)), const((d, d)), const((vw, d)), const((d, d)),
                  per_b, const((1, d)), per_b, per_b, const((ne, d))],
        out_specs=(tok(d), tok(d), pl.BlockSpec((1, ne, tm), lambda bi, i: (bi, 0, i))),
        out_shape=(jax.ShapeDtypeStruct((b, l, d), F32), jax.ShapeDtypeStruct((b, l, d), BF16),
                   jax.ShapeDtypeStruct((b, ne, l), F32)),
        compiler_params=_params(("arbitrary", "arbitrary"), vm),
        name="mix",
    )(x, o_f, o_b, u, u, y_hy, gnw, phy, pgla, wout, g1, n2w, sh2, sc2, rwt)


def _select_body(lg_ref, bias_ref, o_ref, n_ref):
    ne, tn = lg_ref.shape[1], lg_ref.shape[2]
    ng = N_GROUPS
    pg = ne // ng
    scores = jax.nn.sigmoid(lg_ref[0]).reshape(ng, pg, tn)
    sel = scores + bias_ref[...]
    ie = lax.broadcasted_iota(jnp.int32, sel.shape, 1)
    m1 = jnp.max(sel, axis=1, keepdims=True)
    i1 = jnp.min(jnp.where(sel == m1, ie, pg), axis=1, keepdims=True)
    m2 = jnp.max(jnp.where(ie == i1, -jnp.inf, sel), axis=1, keepdims=True)
    grp = m1 + m2
    ig = lax.broadcasted_iota(jnp.int32, grp.shape, 0)
    rank = jnp.zeros(grp.shape, jnp.int32)
    for g in range(ng):
        other = grp[g:g + 1]
        rank = rank + jnp.where((other > grp) | ((other == grp) & (g < ig)), 1, 0)
    cand = jnp.where(rank < TOPK_GROUPS, sel, -jnp.inf)
    flat = ig * pg + ie
    w = jnp.zeros(sel.shape, F32)
    for _ in range(TOP_K):
        best = jnp.max(jnp.max(cand, axis=1, keepdims=True), axis=0, keepdims=True)
        first = jnp.min(jnp.min(jnp.where(cand == best, flat, ne), axis=1, keepdims=True), axis=0, keepdims=True)
        hit = flat == first
        w = jnp.where(hit, scores, w)
        cand = jnp.where(hit, -jnp.inf, cand)
    tot = jnp.sum(jnp.sum(w, axis=1, keepdims=True), axis=0, keepdims=True)
    comb = (w / tot * ROUTED_SCALE).reshape(ne, tn)
    o_ref[0] = comb
    sub = tn // n_ref.shape[1]
    for s in range(n_ref.shape[1]):
        n_ref[0, s] = jnp.sum(jnp.where(comb[:, s * sub:(s + 1) * sub] > 0.0, 1.0, 0.0), axis=1, keepdims=True)


def _select(logits_t, bias, tn, sub):
    b, ne, l = logits_t.shape
    nb = l // tn
    return pl.pallas_call(
        _select_body,
        grid=(b, nb),
        in_specs=[pl.BlockSpec((1, ne, tn), lambda bi, i: (bi, 0, i)),
                  pl.BlockSpec(bias.shape, lambda bi, i: (0, 0, 0))],
        out_specs=(pl.BlockSpec((1, ne, tn), lambda bi, i: (bi, 0, i)),
                   pl.BlockSpec((1, tn // sub, ne, 1), lambda bi, i: (bi, i, 0, 0))),
        out_shape=(jax.ShapeDtypeStruct((b, ne, l), F32), jax.ShapeDtypeStruct((b, l // sub, ne, 1), F32)),
        compiler_params=_params(("arbitrary", "arbitrary"), 64 * ne * tn * 4),
        name="select",
    )(logits_t, bias)


def _moe_body(order_ref, rounds_ref, x_ref, ct_ref, tri_ref, *refs, per, sub, cap):
    w1_refs, w3_refs, w2_refs = refs[:per], refs[per:2 * per], refs[2 * per:3 * per]
    sw1_ref, sw3_ref, sw2_ref, x1_ref, g2_ref, fnw_ref, o_ref, acc_ref, rank_ref = refs[3 * per:]
    g = pl.program_id(1)
    tm = x_ref.shape[0]
    ns = tm // sub

    @pl.when(g == 0)
    def _():
        x = x_ref[...]
        hs = _silu(_dot(x, sw1_ref[...])) * _dot(x, sw3_ref[...])
        acc_ref[...] = _dot(hs.astype(BF16), sw2_ref[...])
        for s in range(ns):
            chosen = ct_ref[0, :, s * sub:(s + 1) * sub] > 0.0
            before = _dot(jnp.where(chosen, 1.0, 0.0).astype(BF16), tri_ref[...])
            rank_ref[s] = jnp.where(chosen, before, -1.0)

    ids = [order_ref[pl.program_id(0), g * per + i] for i in range(per)]
    ranks = [[rank_ref[s, pl.ds(ids[i], 1), :] for i in range(per)] for s in range(ns)]
    wts = [[ct_ref[0, pl.ds(ids[i], 1), s * sub:(s + 1) * sub] for i in range(per)] for s in range(ns)]
    n_rounds = rounds_ref[pl.program_id(0), g]
    slot = lax.broadcasted_iota(jnp.int32, (cap, sub), 0).astype(F32)

    def one_round(r, carry):
        base = slot + (r * cap).astype(F32)
        packed, spread = [], []
        for s in range(ns):
            hits = [base == ranks[s][i] for i in range(per)]
            pack = jnp.concatenate([jnp.where(h, 1.0, 0.0).astype(BF16) for h in hits], axis=0)
            spread.append(jnp.concatenate([jnp.where(h, wts[s][i], 0.0).astype(BF16)
                                           for i, h in enumerate(hits)], axis=0))
            packed.append(_dot(pack, x_ref[s * sub:(s + 1) * sub, :]).astype(BF16))
        outs = []
        for i in range(per):
            ze = jnp.concatenate([packed[s][i * cap:(i + 1) * cap] for s in range(ns)], axis=0)
            h = _silu(_dot(ze, w1_refs[i][0])) * _dot(ze, w3_refs[i][0])
            outs.append(_dot(h.astype(BF16), w2_refs[i][0]).astype(BF16))
        for s in range(ns):
            ys = jnp.concatenate([outs[i][s * cap:(s + 1) * cap] for i in range(per)], axis=0)
            acc_ref[s * sub:(s + 1) * sub, :] += _tn(spread[s], ys)
        return carry

    lax.fori_loop(0, n_rounds, one_round, 0)

    @pl.when(g == pl.num_programs(1) - 1)
    def _():
        y = x1_ref[...] + g2_ref[0] * acc_ref[...]
        ms = jnp.mean(y * y, axis=-1, keepdims=True)
        o_ref[...] = y * lax.rsqrt(ms + EPS) * fnw_ref[...]


def _moe(xn2, comb_t, counts, w1, w3, w2, sw1, sw3, sw2, x1, g2, fnw, tm):
    t, d = xn2.shape
    ne, _, f = w1.shape
    l = comb_t.shape[2]
    per = EXP_PER_STEP
    gpb = l // tm
    sub, cap = MOE_SUB, MOE_CAP
    tri = jnp.asarray(np.triu(np.ones((sub, sub), np.float32), 1), BF16)
    load = jnp.max(counts.reshape(comb_t.shape[0] * gpb, tm // sub, ne), axis=1).astype(jnp.int32)
    order = jnp.argsort(load, axis=-1).astype(jnp.int32)
    step_load = jnp.max(jnp.take_along_axis(load, order, axis=-1).reshape(-1, ne // per, per), axis=-1)
    rounds = (step_load + (cap - 1)) // cap
    tok = lambda w: pl.BlockSpec((tm, w), lambda i, g, o, r: (i, 0))
    const = lambda shape: pl.BlockSpec(shape, lambda i, g, o, r: (0,) * len(shape))

    def expert(shape):
        return [pl.BlockSpec((1,) + shape, lambda i, g, o, r, k=k: (o[i, g * per + k], 0, 0)) for k in range(per)]

    vm = (2 * tm * d * 2 + 2 * ne * tm * 4 + 2 * 3 * per * d * f * 2 + 2 * 3 * d * f * 2
          + 2 * tm * d * 4 * 2 + tm * d * 4 + 16 * per * cap * (tm // sub) * d)
    grid_spec = pltpu.PrefetchScalarGridSpec(
        num_scalar_prefetch=2,
        grid=(t // tm, ne // per),
        in_specs=([tok(d),
                   pl.BlockSpec((1, ne, tm), lambda i, g, o, r: (i // gpb, 0, i % gpb)),
                   const(tri.shape)]
                  + expert((d, f)) + expert((d, f)) + expert((f, d))
                  + [const(sw1.shape), const(sw3.shape), const(sw2.shape),
                     tok(d),
                     pl.BlockSpec((1, 1, d), lambda i, g, o, r: (i // gpb, 0, 0)),
                     const((1, d))]),
        out_specs=tok(d),
        scratch_shapes=[pltpu.VMEM((tm, d), F32), pltpu.VMEM((tm // sub, ne, sub), F32)])
    return pl.pallas_call(
        functools.partial(_moe_body, per=per, sub=sub, cap=cap),
        grid_spec=grid_spec,
        out_shape=jax.ShapeDtypeStruct((t, d), F32),
        compiler_params=_params(("arbitrary", "arbitrary"), vm),
        name="moe",
    )(order, rounds, xn2, comb_t, tri, *([w1] * per), *([w3] * per), *([w2] * per), sw1, sw3, sw2, x1, g2, fnw)


def _pad_to(a, rows, cols):
    return jnp.pad(a, ((0, rows - a.shape[0]), (0, cols - a.shape[1])))


def kernel(x, c, ctx, c_ctx, ada_w, ada_b, norm1_w, norm2_w, w_in, hy_conv_w, hy_conv_b, hy_w1, hy_b1, hy_freq, hy_w2, hy_b2, hy_w3, hy_bias, gla_a_w2, gla_a_b, gla_norm_w, proj_hy, proj_gla, w_out, router_w, router_bias, exp_w1, exp_w3, exp_w2, sh_w1, sh_w3, sh_w2, final_norm_w):
    b, l, d = x.shape
    assert ada_w.shape[0] == 1, "single-layer block"
    assert l // GRID_W * GRID_W == l and FFT_N2 == GRID_W
    heads = GLA_HEADS
    qk_w = d // 2
    dk = qk_w // heads
    v_w = d
    dv = v_w // heads
    a_w = 2 * GLA_RANK
    hy_w = d
    hy_cols = (HY_ORDER + 1) * hy_w

    rows = -(-(b + 1) // 8) * 8
    cc = jnp.zeros((rows, d), F32).at[:b].set(c).at[b].set(c_ctx)
    mods = _mods(cc, ada_w[0], ada_b[0][None])
    sh1, sc1, g1, sh2, sc2, g2 = [m[:b, None, :] for m in jnp.split(mods, 6, axis=-1)]
    csh1, csc1 = [jnp.broadcast_to(m[b][None, None, :], (b, 1, d)) for m in jnp.split(mods, 6, axis=-1)[:2]]

    w = w_in[0]
    o_a = qk_w + v_w
    o_q = o_a + a_w
    o_g = o_q + qk_w
    o_hy = o_g + v_w
    o_gate = o_hy + hy_cols
    w_k, w_v = w[:, :qk_w], w[:, qk_w:o_a]
    w_a = jnp.pad(w[:, o_a:o_q], ((0, 0), (0, LANES - a_w)))
    wp = jnp.concatenate([w[:, o_gate:], w[:, o_g:o_hy], w_v, w_k, w[:, o_q:o_g], w[:, o_hy:o_gate], w_a],
                         axis=1).astype(BF16)
    p_gate = 0
    p_g = 2 * d
    p_v = p_g + v_w
    p_k = p_v + v_w
    p_q = p_k + qk_w
    p_hy = p_q + qk_w
    p_a = p_hy + hy_cols
    n_all = p_a + LANES
    w_ctx = jnp.concatenate([w_k, w_v, w_a], axis=1).astype(BF16)
    cols_ctx = (0, qk_w, qk_w + v_w, None)
    cols = (p_k, p_v, p_a, p_q)

    nw1 = norm1_w[0][None]
    u_ctx = _inproj(ctx, nw1, csh1, csc1, w_ctx, ctx.shape[1], w_ctx.shape[1])
    u = _inproj(x, nw1, sh1, sc1, wp, 1024, n_all // 5)

    wa = gla_a_w2[0].reshape(2, GLA_RANK, heads, dk).transpose(0, 2, 1, 3)
    waf = jnp.pad(wa[0], ((0, 0), (0, LANES - GLA_RANK), (0, 0))).astype(BF16)
    wab = jnp.pad(wa[1], ((0, 0), (GLA_RANK, LANES - 2 * GLA_RANK), (0, 0))).astype(BF16)
    ba = gla_a_b[0].reshape(2, heads, 1, dk)
    zeros_state = jnp.zeros((b, heads, dv, dk), F32)
    _, _, s_f, s_b = _gla(u_ctx, cols_ctx, waf, wab, ba[0], ba[1], zeros_state, zeros_state, dk, dv, False,
                          ctx.shape[1])
    o_f, o_b, _, _ = _gla(u, cols, waf, wab, ba[0], ba[1], s_f, s_b, dk, dv, True, GLA_BLOCK)

    n = 2 * l
    max_decay = math.log(HY_TARGET) / HY_FAST_DECAY
    min_decay = math.log(HY_TARGET) / HY_SLOW_DECAY
    deltas = jnp.asarray(np.abs(np.linspace(min_decay, max_decay, hy_w, dtype=np.float32))[None])
    ffn = hy_w1.shape[2]
    assert max(hy_w1.shape[1], ffn) <= LANES // 2, "filter MLP widths must fit half a lane tile"
    kern = _hyfilt(l, hy_w,
                   _pad_to(hy_w1[0], LANES, LANES), _pad_to(hy_b1[0][None], 1, LANES),
                   _pad_to(hy_freq[0, 0][None], 1, LANES),
                   _pad_to(hy_w2[0], LANES, LANES), _pad_to(hy_b2[0][None], 1, LANES),
                   _pad_to(hy_freq[0, 1][None], 1, LANES),
                   jnp.pad(hy_w3[0], ((0, LANES - ffn), (0, 0))), deltas)
    tabs = _fft_tables(n, FFT_N2)
    spec = _hyspec(kern, tabs[1], tabs[2], FFT_N2)
    cw, cb = hy_conv_w[0], hy_conv_b[0][None]
    z1 = _hyconv(u, p_hy, u, p_hy + hy_w, cw, cb, 0, hy_w, spec, 0, hy_bias[0, 0][None], tabs, True)
    y_hy = _hyconv(z1, 0, u, p_hy + 2 * hy_w, cw, cb, 0, 2 * hy_w, spec, 1, hy_bias[0, 1][None], tabs, False)

    x1, xn2, logits_t = _mix(x, o_f, o_b, u, p_g, p_gate, y_hy, gla_norm_w[0][None],
                             proj_hy[0].astype(BF16), proj_gla[0].astype(BF16), w_out[0].astype(BF16),
                             g1, norm2_w[0][None], sh2, sc2, router_w[0].T, 512)
    comb, counts = _select(logits_t, router_bias[0].reshape(N_GROUPS, N_EXPERTS // N_GROUPS, 1), 512, MOE_SUB)
    out = _moe(xn2.reshape(b * l, d), comb, counts, exp_w1[0].astype(BF16), exp_w3[0].astype(BF16),
               exp_w2[0].astype(BF16), sh_w1[0].astype(BF16), sh_w3[0].astype(BF16), sh_w2[0].astype(BF16),
               x1.reshape(b * l, d), g2, final_norm_w[None], 1024)
    return out.reshape(b, l, d)
```
